```python
import math
import jax
import jax.numpy as jnp
from jax import lax
import numpy as np

D_MODEL = 1024
BATCH = 8
SEQ = 2048
DEPTH = 1
DEC_BATCH = 128
DEC_SEQ = 4
PAST_LEN = 16384
PAGE_SIZE = 128

HG_HEADS = 4
HG_DK = 128
HG_DV = 128
HG_WIDTH = HG_HEADS * HG_DK
GDN_HEADS = 4
GDN_DK = 128
GDN_DV = 128
GDN_QK = GDN_HEADS * GDN_DK
GDN_V = GDN_HEADS * GDN_DV
CONV_W = 4
CONV_CH = 2 * GDN_QK + GDN_V
MEM_LEN = 256
MEM_HEADS = 4
MEM_HD = 128
MEM_WIDTH = MEM_HEADS * MEM_HD
N_BRANCH = 3
FF_RAW = -(-8 * D_MODEL // 3)
FFN_HIDDEN = -(-FF_RAW // 256) * 256
IN_WIDTH = 4 * HG_WIDTH + CONV_CH + GDN_V + 2 * GDN_HEADS + MEM_WIDTH + N_BRANCH * D_MODEL
CHUNK = 64
EPS = 1e-6

kernel_name = "hgrn2_gdn_memxattn_gated_parallel_decoder_step"

F32 = jnp.float32


def rmsnorm(x, g):
    xf = x.astype(F32)
    y = xf * lax.rsqrt(jnp.mean(xf * xf, axis=-1, keepdims=True) + EPS)
    return (y * g.astype(F32)).astype(x.dtype)


def l2norm(x):
    xf = x.astype(F32)
    return xf * lax.rsqrt(jnp.sum(xf * xf, axis=-1, keepdims=True) + EPS)


def _chunk_dims(T):
    c = min(CHUNK, T)
    return c, -(-T // c)


def _to_chunks(a, c, n):
    B, T = a.shape[0], a.shape[1]
    a = jnp.pad(a, [(0, 0), (0, n * c - T)] + [(0, 0)] * (a.ndim - 2))
    a = a.reshape((B, n, c) + a.shape[2:])
    return jnp.moveaxis(jnp.moveaxis(a, 1, 0), 2, 3)


def _from_chunks(o, T):
    n, B, H, c, d = o.shape
    return o.transpose(1, 0, 3, 2, 4).reshape(B, n * c, H, d)[:, :T]


def hgrn2_chunked(q, k, v, logf, s0):
    T = q.shape[1]
    c, n = _chunk_dims(T)
    tri = jnp.tril(jnp.ones((c, c), bool))

    def step(S, inp):
        qc, kc, vc, lc = inp
        G = jnp.cumsum(lc, axis=2)
        diff = G[:, :, :, None, :] - G[:, :, None, :, :]
        dec = jnp.exp(jnp.where(tri[:, :, None], diff, -jnp.inf))
        A = jnp.einsum('bhtd,bhsd,bhtsd->bhts', qc, kc, dec)
        o = (jnp.einsum('bhts,bhsv->bhtv', A, vc)
             + jnp.einsum('bhtd,bhdv->bhtv', qc * jnp.exp(G), S))
        GC = G[:, :, -1:]
        S = (jnp.exp(GC[:, :, 0])[..., None] * S
             + jnp.einsum('bhsd,bhsv->bhdv', kc * jnp.exp(GC - G), vc))
        return S, o

    xs = tuple(_to_chunks(a.astype(F32), c, n) for a in (q, k, v, logf))
    S, o = lax.scan(step, s0.astype(F32), xs)
    return _from_chunks(o, T), S


def gdn_chunked(q, k, v, beta, g, s0):
    T = q.shape[1]
    c, n = _chunk_dims(T)
    tri = jnp.tril(jnp.ones((c, c), bool))
    strict = jnp.tril(jnp.ones((c, c), bool), k=-1)
    eye = jnp.eye(c, dtype=F32)

    def step(S, inp):
        qc, kc, vc, bc, gc = inp
        G = jnp.cumsum(gc, axis=-1)
        L = jnp.exp(jnp.where(tri, G[..., :, None] - G[..., None, :], -jnp.inf))
        kb = kc * bc[..., None]
        M = jnp.where(strict, jnp.einsum('bhtd,bhsd->bhts', kb, kc) * L, 0.0)
        rhs = jnp.concatenate([vc * bc[..., None], kb * jnp.exp(G)[..., None]], axis=-1)
        X = lax.linalg.triangular_solve(eye + M, rhs, left_side=True, lower=True,
                                        unit_diagonal=True)
        u, w = X[..., :GDN_DV], X[..., GDN_DV:]
        v_new = u - jnp.einsum('bhtd,bhdv->bhtv', w, S)
        Aqk = jnp.where(tri, jnp.einsum('bhtd,bhsd->bhts', qc, kc) * L, 0.0)
        o = (jnp.einsum('bhtd,bhdv->bhtv', qc * jnp.exp(G)[..., None], S)
             + jnp.einsum('bhts,bhsv->bhtv', Aqk, v_new))
        GC = G[..., -1:]
        S = (jnp.exp(GC)[..., None] * S
             + jnp.einsum('bhsd,bhsv->bhdv', kc * jnp.exp(GC - G)[..., None], v_new))
        return S, o

    xs = (_to_chunks(q.astype(F32), c, n), _to_chunks(k.astype(F32), c, n),
          _to_chunks(v.astype(F32), c, n), _to_chunks(beta.astype(F32), c, n),
          _to_chunks(g.astype(F32), c, n))
    S, o = lax.scan(step, s0.astype(F32), xs)
    return _from_chunks(o, T), S


def causal_conv(u, buf, w):
    T = u.shape[1]
    up = jnp.concatenate([buf.astype(u.dtype), u], axis=1)
    out = up[:, 0:T] * w[0]
    for j in range(1, CONV_W):
        out = out + up[:, j:j + T] * w[j]
    return out, up[:, -(CONV_W - 1):]


def _split_in(proj):
    sizes = [HG_WIDTH] * 4 + [CONV_CH, GDN_V, GDN_HEADS, GDN_HEADS, MEM_WIDTH, N_BRANCH * D_MODEL]
    points = []
    acc = 0
    for s in sizes[:-1]:
        acc += s
        points.append(acc)
    return jnp.split(proj, points, axis=-1)


def _mem_kv(mem, g_mem, w_mem_kv):
    B, M = mem.shape[0], mem.shape[1]
    kv = rmsnorm(mem, g_mem) @ w_mem_kv
    k, v = jnp.split(kv, 2, axis=-1)
    return k.reshape(B, M, MEM_HEADS, MEM_HD), v.reshape(B, M, MEM_HEADS, MEM_HD)


def _layer(x, mem_k, mem_v, conv_buf, s_hg, s_gdn, lb,
           g_pre_mix, w_in, w_conv, a_log, dt_bias, g_hg_out, g_gdn_out,
           w_br_hg, w_br_gdn, w_br_mem, w_out, g_post_mix, g_pre_ffn,
           w_ffn_in, w_ffn_out, g_post_ffn):
    B, T, _ = x.shape
    dt = x.dtype
    xn = rmsnorm(x, g_pre_mix)
    (hg_q, hg_f, hg_i, hg_gate, gdn_qkv, gdn_z, gdn_a, gdn_b,
     mem_q, gates) = _split_in(xn @ w_in)

    f = lb + (1.0 - lb) * jax.nn.sigmoid(hg_f.astype(F32))
    logf = jnp.log(f)
    hk = 1.0 - f
    o_hg, s_hg_new = hgrn2_chunked(hg_q.reshape(B, T, HG_HEADS, HG_DK),
                                   hk.reshape(B, T, HG_HEADS, HG_DK),
                                   hg_i.reshape(B, T, HG_HEADS, HG_DV),
                                   logf.reshape(B, T, HG_HEADS, HG_DK), s_hg)
    o_hg = (rmsnorm(o_hg, g_hg_out.reshape(HG_HEADS, HG_DV))
            * jax.nn.silu(hg_gate.astype(F32)).reshape(B, T, HG_HEADS, HG_DV))
    o_hg = o_hg.reshape(B, T, HG_WIDTH).astype(dt)

    qkv, conv_new = causal_conv(gdn_qkv, conv_buf, w_conv)
    qkv = jax.nn.silu(qkv)
    gq, gk, gv = jnp.split(qkv, [GDN_QK, 2 * GDN_QK], axis=-1)
    gq = l2norm(gq.reshape(B, T, GDN_HEADS, GDN_DK)) * (GDN_DK ** -0.5)
    gk = l2norm(gk.reshape(B, T, GDN_HEADS, GDN_DK))
    gv = gv.reshape(B, T, GDN_HEADS, GDN_DV)
    beta = jax.nn.sigmoid(gdn_b.astype(F32))
    glog = -jnp.exp(a_log.astype(F32)) * jax.nn.softplus(gdn_a.astype(F32) + dt_bias.astype(F32))
    o_gdn, s_gdn_new = gdn_chunked(gq, gk, gv, beta, glog, s_gdn)
    o_gdn = (rmsnorm(o_gdn, g_gdn_out)
             * jax.nn.silu(gdn_z.astype(F32)).reshape(B, T, GDN_HEADS, GDN_DV))
    o_gdn = o_gdn.reshape(B, T, GDN_V).astype(dt)

    mq = mem_q.reshape(B, T, MEM_HEADS, MEM_HD).astype(F32)
    s = jnp.einsum('bthd,bmhd->bhtm', mq, mem_k.astype(F32)) * (MEM_HD ** -0.5)
    p = jax.nn.softmax(s, axis=-1)
    o_mem = jnp.einsum('bhtm,bmhd->bthd', p, mem_v.astype(F32)).reshape(B, T, MEM_WIDTH).astype(dt)

    g_hg, g_gdn, g_mem = jnp.split(jax.nn.sigmoid(gates), N_BRANCH, axis=-1)
    merged = g_hg * (o_hg @ w_br_hg) + g_gdn * (o_gdn @ w_br_gdn) + g_mem * (o_mem @ w_br_mem)
    h = x + rmsnorm(merged @ w_out, g_post_mix)

    gate, up = jnp.split(rmsnorm(h, g_pre_ffn) @ w_ffn_in, 2, axis=-1)
    ff = (jax.nn.silu(gate) * up) @ w_ffn_out
    y = h + rmsnorm(ff, g_post_ffn)
    return y.astype(dt), conv_new.astype(dt), s_hg_new.astype(dt), s_gdn_new.astype(dt)


def setup_inputs(seed: int = 0) -> dict:
    key = jax.random.key(seed)
    ks = iter(jax.random.split(key, 32))

    def nrm(shape, scale):
        return jax.random.normal(next(ks), shape, F32) * scale

    def gain(shape):
        return 1.0 + nrm(shape, 0.05)

    a_log = jnp.log(jax.random.uniform(next(ks), (DEPTH, GDN_HEADS), F32, 1.0, 16.0))
    dtv = jnp.exp(jax.random.uniform(next(ks), (DEPTH, GDN_HEADS), F32,
                                     math.log(1e-3), math.log(1e-1)))
    dt_bias = dtv + jnp.log(-jnp.expm1(-dtv))
    return {
        "x_prompt": nrm((BATCH, SEQ, D_MODEL), 1.0),
        "x_sample": nrm((DEC_BATCH, DEC_SEQ, D_MODEL), 1.0),
        "mem_prompt": nrm((BATCH, MEM_LEN, D_MODEL), 1.0),
        "cache_mem_k": nrm((DEPTH, DEC_BATCH, MEM_LEN, MEM_HEADS, MEM_HD), 1.0),
        "cache_mem_v": nrm((DEPTH, DEC_BATCH, MEM_LEN, MEM_HEADS, MEM_HD), 1.0),
        "state_hgrn": nrm((DEPTH, DEC_BATCH, HG_HEADS, HG_DK, HG_DV), 0.5),
        "state_gdn": nrm((DEPTH, DEC_BATCH, GDN_HEADS, GDN_DK, GDN_DV), 0.1),
        "state_gdn_conv": nrm((DEPTH, DEC_BATCH, CONV_W - 1, CONV_CH), 1.0),
        "hg_lb_logits": nrm((DEPTH + 1, HG_WIDTH), 0.5),
        "g_pre_mix": gain((DEPTH, D_MODEL)),
        "w_in": nrm((DEPTH, D_MODEL, IN_WIDTH), D_MODEL ** -0.5),
        "w_conv": nrm((DEPTH, CONV_W, CONV_CH), 0.5),
        "a_log": a_log,
        "dt_bias": dt_bias,
        "g_hg_out": gain((DEPTH, HG_WIDTH)),
        "g_gdn_out": gain((DEPTH, GDN_DV)),
        "g_mem": gain((DEPTH, D_MODEL)),
        "w_mem_kv": nrm((DEPTH, D_MODEL, 2 * MEM_WIDTH), D_MODEL ** -0.5),
        "w_br_hg": nrm((DEPTH, HG_WIDTH, D_MODEL), HG_WIDTH ** -0.5),
        "w_br_gdn": nrm((DEPTH, GDN_V, D_MODEL), GDN_V ** -0.5),
        "w_br_mem": nrm((DEPTH, MEM_WIDTH, D_MODEL), MEM_WIDTH ** -0.5),
        "w_out": nrm((DEPTH, D_MODEL, D_MODEL), D_MODEL ** -0.5),
        "g_post_mix": gain((DEPTH, D_MODEL)),
        "g_pre_ffn": gain((DEPTH, D_MODEL)),
        "w_ffn_in": nrm((DEPTH, D_MODEL, 2 * FFN_HIDDEN), D_MODEL ** -0.5),
        "w_ffn_out": nrm((DEPTH, FFN_HIDDEN, D_MODEL), FFN_HIDDEN ** -0.5),
        "g_post_ffn": gain((DEPTH, D_MODEL)),
    }


def reference(x_prompt, x_sample, mem_prompt, cache_mem_k, cache_mem_v, state_hgrn,
              state_gdn, state_gdn_conv, hg_lb_logits, g_pre_mix, w_in, w_conv, a_log,
              dt_bias, g_hg_out, g_gdn_out, g_mem, w_mem_kv, w_br_hg, w_br_gdn, w_br_mem,
              w_out, g_post_mix, g_pre_ffn, w_ffn_in, w_ffn_out, g_post_ffn):
    lb_all = jnp.cumsum(jax.nn.softmax(hg_lb_logits.astype(F32), axis=0), axis=0)
    yp, ys = x_prompt, x_sample
    Bp = x_prompt.shape[0]
    mk_p, mv_p, hg_p, gdn_p, conv_p = [], [], [], [], []
    hg_s, gdn_s, conv_s = [], [], []
    for l in range(DEPTH):
        lw = (g_pre_mix[l], w_in[l], w_conv[l], a_log[l], dt_bias[l], g_hg_out[l],
              g_gdn_out[l], w_br_hg[l], w_br_gdn[l], w_br_mem[l], w_out[l], g_post_mix[l],
              g_pre_ffn[l], w_ffn_in[l], w_ffn_out[l], g_post_ffn[l])
        mk, mv = _mem_kv(mem_prompt, g_mem[l], w_mem_kv[l])
        conv0 = jnp.zeros((Bp, CONV_W - 1, CONV_CH), x_prompt.dtype)
        hg0 = jnp.zeros((Bp, HG_HEADS, HG_DK, HG_DV), F32)
        gdn0 = jnp.zeros((Bp, GDN_HEADS, GDN_DK, GDN_DV), F32)
        yp, cb, sh, sg = _layer(yp, mk, mv, conv0, hg0, gdn0, lb_all[l], *lw)
        mk_p.append(mk.astype(x_prompt.dtype))
        mv_p.append(mv.astype(x_prompt.dtype))
        hg_p.append(sh)
        gdn_p.append(sg)
        conv_p.append(cb)
        ys, cb2, sh2, sg2 = _layer(ys, cache_mem_k[l], cache_mem_v[l], state_gdn_conv[l],
                                   state_hgrn[l], state_gdn[l], lb_all[l], *lw)
        hg_s.append(sh2)
        gdn_s.append(sg2)
        conv_s.append(cb2)
    return (yp, ys, jnp.stack(mk_p), jnp.stack(mv_p), jnp.stack(hg_p), jnp.stack(gdn_p),
            jnp.stack(conv_p), jnp.stack(hg_s), jnp.stack(gdn_s), jnp.stack(conv_s))
```

```python
import functools
import math

import numpy as np
import jax
import jax.numpy as jnp
from jax import lax
from jax.experimental import pallas as pl
from jax.experimental.pallas import tpu as pltpu

F32 = jnp.float32
BF16 = jnp.bfloat16

EPS = 1e-6
HEADS = 4
HEAD_DIM = 128
CONV_W = 4
N_BRANCH = 3

LANES = 128
SUBLANES = 8
VMEM_LIMIT = 56 * 1024 * 1024

HG_CHUNK = 128
GDN_CHUNK = 64
TOKEN_TILE = 256
ATTN_TILE = 512
FFN_CHUNK = 256
DECODE_BLOCK = 8


def _cparams(*sem):
    return pltpu.CompilerParams(dimension_semantics=sem, vmem_limit_bytes=VMEM_LIMIT)


def _resident(shape):
    nd = len(shape)
    return pl.BlockSpec(shape, lambda *_: (0,) * nd, pipeline_mode=pl.Buffered(1))


def _bf(x):
    return x.astype(BF16)


def _dot(a, b):
    return jnp.dot(a, b, preferred_element_type=F32)


def _dot_nt(a, b):
    return lax.dot_general(a, b, (((1,), (1,)), ((), ())), preferred_element_type=F32)


def _dot_tn(a, b):
    return lax.dot_general(a, b, (((0,), (0,)), ((), ())), preferred_element_type=F32)


def _split2(x):
    hi = _bf(x)
    lo = _bf(x - hi.astype(F32))
    return hi, lo


def _dot_exact_lhs(w_bf16, x):
    hi, lo = _split2(x)
    return _dot(w_bf16, hi) + _dot(w_bf16, lo)


def _rms(x, g):
    return x * lax.rsqrt(jnp.mean(x * x, axis=-1, keepdims=True) + EPS) * g


def _norm_proj_kernel(x_ref, g_ref, w_ref, *out_refs):
    xn = _bf(_rms(x_ref[...], g_ref[...]))
    off = 0
    for o_ref in out_refs:
        width = o_ref.shape[-1]
        for c0 in range(0, width, 512):
            cw = min(512, width - c0)
            o_ref[:, c0:c0 + cw] = _dot(xn, w_ref[:, off + c0:off + c0 + cw])
        off += width


def _norm_proj(x2d, g, w_bf16, widths, name):
    n, d = x2d.shape
    tm = min(TOKEN_TILE, n)
    assert n % tm == 0 and sum(widths) == w_bf16.shape[1]
    return pl.pallas_call(
        _norm_proj_kernel,
        grid=(n // tm,),
        in_specs=[pl.BlockSpec((tm, d), lambda i: (i, 0)),
                  _resident((1, d)),
                  _resident(w_bf16.shape)],
        out_specs=[pl.BlockSpec((tm, w), lambda i: (i, 0)) for w in widths],
        out_shape=[jax.ShapeDtypeStruct((n, w), F32) for w in widths],
        compiler_params=_cparams("arbitrary"),
        name=name,
    )(x2d, g.reshape(1, d), w_bf16)


def _forget_lower_bound(logits, layer):
    m = jnp.max(logits, axis=0, keepdims=True)
    e = jnp.exp(logits - m)
    return jnp.sum(e[:layer + 1], axis=0, keepdims=True) / jnp.sum(e, axis=0, keepdims=True)


def _col_to_matrix(row):
    n = row.shape[-1]
    return jnp.broadcast_to(row, (n, n)).T


def _gdn_inputs(ext_ref, c, wconv, ab, alog, dtb, t_valid):
    acc = ext_ref[pl.ds(SUBLANES - 3, c), :] * wconv[0:1, :]
    for j in range(1, CONV_W):
        acc = acc + ext_ref[pl.ds(SUBLANES - 3 + j, c), :] * wconv[j:j + 1, :]
    qkv = jax.nn.silu(acc)
    w = HEADS * HEAD_DIM
    qs, ks = [], []
    for h in range(HEADS):
        qh = qkv[:, h * HEAD_DIM:(h + 1) * HEAD_DIM]
        kh = qkv[:, w + h * HEAD_DIM:w + (h + 1) * HEAD_DIM]
        qs.append(qh * lax.rsqrt(jnp.sum(qh * qh, axis=-1, keepdims=True) + EPS) * (HEAD_DIM ** -0.5))
        ks.append(kh * lax.rsqrt(jnp.sum(kh * kh, axis=-1, keepdims=True) + EPS))
    v = qkv[:, 2 * w:3 * w]
    glog = -jnp.exp(alog) * jax.nn.softplus(ab + dtb)
    beta = jax.nn.sigmoid(ab)
    if t_valid is not None:
        valid = lax.broadcasted_iota(jnp.int32, ab.shape, 0) < t_valid
        glog = jnp.where(valid, glog, 0.0)
        beta = jnp.where(valid, beta, 0.0)
    return qs, ks, v, glog, beta


def _hier_tables(c):
    t = np.arange(c)
    u = np.arange(c)
    tri = (u[None, :] <= t[:, None]).astype(np.float32)
    rows, masks = [], []
    b = c // 2
    while b >= 1:
        ref = (t // (2 * b)) * 2 * b + b - 1
        rows.append(tri - (u[None, :] <= ref[:, None]).astype(np.float32))
        right = (t % (2 * b)) >= b
        left = (t % (2 * b)) < b
        same = (t[:, None] // (2 * b)) == (t[None, :] // (2 * b))
        masks.append((right[:, None] & left[None, :] & same).astype(np.float32))
        b //= 2
    rows.append(tri)
    return np.concatenate(rows, 0), np.stack(masks, 0)


def _hgrn_chunk(q, hf, v, lb, s, wall_ref, mask_ref, c):
    f = lb + (1.0 - lb) * jax.nn.sigmoid(hf)
    logf = jnp.log(f)
    k = 1.0 - f
    nlev = mask_ref.shape[0]
    d = _dot_exact_lhs(wall_ref[...], logf)
    g = d[nlev * c:(nlev + 1) * c]
    gc = g[c - 1:c, :]
    o = _dot(_bf(q * jnp.exp(g)), _bf(s))
    a = None
    for l in range(nlev):
        e = jnp.exp(-jnp.abs(d[l * c:(l + 1) * c]))
        p = _dot_nt(_bf(q * e), _bf(k * e)) * mask_ref[l]
        a = p if a is None else a + p
    o = o + _dot(_bf(a), _bf(v)) + jnp.sum(q * k, axis=-1, keepdims=True) * v
    s_new = _col_to_matrix(jnp.exp(gc)) * s + _dot_tn(_bf(k * jnp.exp(gc - g)), _bf(v))
    return o, s_new


def _hgrn_kernel(q_ref, f_ref, i_ref, gate_ref, lbl_ref, gout_ref, wall_ref, mask_ref, s0_ref,
                 o_ref, sn_ref, s_scr, *, c, layer):
    n = pl.program_id(1)

    @pl.when(n == 0)
    def _():
        s_scr[...] = s0_ref[0]

    lb = _forget_lower_bound(lbl_ref[...], layer)
    gout = gout_ref[...]
    for h in range(HEADS):
        sl = slice(h * HEAD_DIM, (h + 1) * HEAD_DIM)
        o, s_new = _hgrn_chunk(q_ref[0, :, sl], f_ref[0, :, sl], i_ref[0, :, sl], lb[:, sl],
                               s_scr[h], wall_ref, mask_ref, c)
        s_scr[h] = s_new
        o_ref[0, :, sl] = _rms(o, gout[:, sl]) * jax.nn.silu(gate_ref[0, :, sl])

    @pl.when(n == pl.num_programs(1) - 1)
    def _():
        sn_ref[0] = s_scr[...]


def _hgrn_long(hgp, lb_logits, g_out, s0, layer):
    bsz, t, _ = hgp.shape
    c = HG_CHUNK
    assert t % c == 0
    w = HEADS * HEAD_DIM
    wall, masks = _hier_tables(c)
    col = lambda j: pl.BlockSpec((1, c, w), lambda b, n, j=j: (b, n, j))
    st = pl.BlockSpec((1, HEADS, HEAD_DIM, HEAD_DIM), lambda b, n: (b, 0, 0, 0))
    return pl.pallas_call(
        functools.partial(_hgrn_kernel, c=c, layer=layer),
        grid=(bsz, t // c),
        in_specs=[col(0), col(1), col(2), col(3),
                  _resident(lb_logits.shape), _resident((1, w)),
                  _resident(wall.shape), _resident(masks.shape), st],
        out_specs=[pl.BlockSpec((1, c, w), lambda b, n: (b, n, 0)), st],
        out_shape=[jax.ShapeDtypeStruct((bsz, t, w), F32),
                   jax.ShapeDtypeStruct(s0.shape, F32)],
        scratch_shapes=[pltpu.VMEM((HEADS, HEAD_DIM, HEAD_DIM), F32)],
        compiler_params=_cparams("arbitrary", "arbitrary"),
        name="hgrn_long",
    )(hgp, hgp, hgp, hgp, lb_logits, g_out.reshape(1, w),
      jnp.asarray(wall, BF16), jnp.asarray(masks, F32), s0)


def _gdn_chunk(q, k, v, g_b, beta_b, s, tri_bf, c):
    ti = lax.broadcasted_iota(jnp.int32, (c, c), 0)
    si = lax.broadcasted_iota(jnp.int32, (c, c), 1)
    g = _dot_exact_lhs(tri_bf, g_b)
    gdiff = _dot_exact_lhs(tri_bf, jnp.where(ti > si, g_b[:, :c], 0.0))
    decay = jnp.exp(gdiff)
    kb = k * beta_b
    kq = _dot_nt(_bf(jnp.concatenate([kb, q], axis=0)), _bf(k))
    m = jnp.where(ti > si, kq[:c] * decay, 0.0)
    aqk = jnp.where(ti >= si, kq[c:] * decay, 0.0)
    eg = jnp.exp(g)
    rhs = jnp.concatenate([v * beta_b, kb * eg], axis=-1)
    npow = -m
    poff = npow
    for _ in range(int(math.log2(c)) - 1):
        npow = _dot(_bf(npow), _bf(npow))
        poff = poff + npow + _dot(_bf(poff), _bf(npow))
    x = rhs + _dot(_bf(poff), _bf(rhs))
    u, w = x[:, :HEAD_DIM], x[:, HEAD_DIM:]
    sb = _bf(s)
    ws_qs = _dot(_bf(jnp.concatenate([w, q * eg], axis=0)), sb)
    v_new = u - ws_qs[:c]
    o = ws_qs[c:] + _dot(_bf(aqk), _bf(v_new))
    gc = g[c - 1:c, :]
    s_new = jnp.exp(gc) * s + _dot_tn(_bf(k * jnp.exp(gc - g)), _bf(v_new))
    return o, s_new


def _gdn_kernel(qkv_ref, z_ref, ab_ref, buf_ref, wconv_ref, alog_ref, dtb_ref, gout_ref, tri_ref,
                s0_ref, o_ref, sn_ref, s_scr, ext_scr, *, c):
    n = pl.program_id(1)

    @pl.when(n == 0)
    def _():
        s_scr[...] = s0_ref[0]
        ext_scr[SUBLANES - 3:SUBLANES, :] = buf_ref[0]

    @pl.when(n > 0)
    def _():
        ext_scr[0:SUBLANES, :] = ext_scr[c:c + SUBLANES, :]

    ext_scr[SUBLANES:SUBLANES + c, :] = qkv_ref[0]
    qs, ks, v, glog, beta = _gdn_inputs(ext_scr, c, wconv_ref[...], ab_ref[0], alog_ref[...],
                                        dtb_ref[...], None)
    tri_bf = tri_ref[...]
    gout = gout_ref[...]
    for h in range(HEADS):
        sl = slice(h * HEAD_DIM, (h + 1) * HEAD_DIM)
        g_b = jnp.broadcast_to(glog[:, h:h + 1], (c, LANES))
        beta_b = jnp.broadcast_to(beta[:, HEADS + h:HEADS + h + 1], (c, LANES))
        o, s_new = _gdn_chunk(qs[h], ks[h], v[:, sl], g_b, beta_b, s_scr[h], tri_bf, c)
        s_scr[h] = s_new
        o_ref[0, :, sl] = _rms(o, gout) * jax.nn.silu(z_ref[0, :, sl])

    @pl.when(n == pl.num_programs(1) - 1)
    def _():
        sn_ref[0] = s_scr[...]


def _pad_lanes(x):
    return jnp.zeros((1, LANES), F32).at[0, :x.shape[0]].set(x.astype(F32))


def _gdn_long(gqkv, gz, ab, conv_buf, w_conv, a_log, dt_bias, g_out, s0):
    bsz, t, wq = gqkv.shape
    c = GDN_CHUNK
    assert t % c == 0
    w = HEADS * HEAD_DIM
    tri = np.tril(np.ones((c, c), np.float32))
    st = pl.BlockSpec((1, HEADS, HEAD_DIM, HEAD_DIM), lambda b, n: (b, 0, 0, 0))
    return pl.pallas_call(
        functools.partial(_gdn_kernel, c=c),
        grid=(bsz, t // c),
        in_specs=[pl.BlockSpec((1, c, wq), lambda b, n: (b, n, 0)),
                  pl.BlockSpec((1, c, w), lambda b, n: (b, n, 0)),
                  pl.BlockSpec((1, c, LANES), lambda b, n: (b, n, 0)),
                  pl.BlockSpec((1, CONV_W - 1, wq), lambda b, n: (b, 0, 0)),
                  _resident((CONV_W, wq)), _resident((1, LANES)), _resident((1, LANES)),
                  _resident((1, HEAD_DIM)), _resident((c, c)), st],
        out_specs=[pl.BlockSpec((1, c, w), lambda b, n: (b, n, 0)), st],
        out_shape=[jax.ShapeDtypeStruct((bsz, t, w), F32),
                   jax.ShapeDtypeStruct(s0.shape, F32)],
        scratch_shapes=[pltpu.VMEM((HEADS, HEAD_DIM, HEAD_DIM), F32),
                        pltpu.VMEM((c + SUBLANES, wq), F32)],
        compiler_params=_cparams("arbitrary", "arbitrary"),
        name="gdn_long",
    )(gqkv, gz, ab, conv_buf, w_conv,
      _pad_lanes(a_log), _pad_lanes(jnp.concatenate([dt_bias, jnp.zeros_like(dt_bias)])),
      g_out.reshape(1, HEAD_DIM), jnp.asarray(tri, BF16), s0)


def _shift_rows(x, d):
    return x if d == 0 else pltpu.roll(x, d, axis=0)


def _cumsum_rows8(x, rows):
    for d in (1, 2, 4):
        x = x + jnp.where(rows >= d, pltpu.roll(x, d, axis=0), 0.0)
    return x


def _hgrn_short(q, hf, v, lb, s, rows, t_valid):
    f = lb + (1.0 - lb) * jax.nn.sigmoid(hf)
    valid = rows < t_valid
    logf = jnp.where(valid, jnp.log(f), 0.0)
    k = jnp.where(valid, 1.0 - f, 0.0)
    g = _cumsum_rows8(logf, rows)
    gc = g[SUBLANES - 1:SUBLANES, :]
    o = _dot(_bf(q * jnp.exp(g)), _bf(s))
    for d in range(t_valid):
        ok = rows >= d
        dec = jnp.exp(jnp.where(ok, g - _shift_rows(g, d), 0.0))
        a = jnp.sum(jnp.where(ok, q * _shift_rows(k, d) * dec, 0.0), axis=-1, keepdims=True)
        o = o + a * _shift_rows(v, d)
    s_new = _col_to_matrix(jnp.exp(gc)) * s + _dot_tn(_bf(k * jnp.exp(gc - g)), _bf(v))
    return o, s_new


def _gdn_short(q, k, v, g_b, beta_b, s, rows, t_valid):
    g = _cumsum_rows8(g_b, rows)
    gc = g[SUBLANES - 1:SUBLANES, :]
    eg = jnp.exp(g)
    kb = k * beta_b
    dec = [None] + [jnp.exp(jnp.where(rows >= d, g - _shift_rows(g, d), 0.0)) for d in range(1, t_valid)]
    mcol = [None] + [jnp.where(rows >= d, jnp.sum(kb * _shift_rows(k, d), axis=-1, keepdims=True) * dec[d], 0.0)
                     for d in range(1, t_valid)]
    ru, rw = v * beta_b, kb * eg
    xu, xw = ru, rw
    for _ in range(t_valid - 1):
        nu, nw = ru, rw
        for d in range(1, t_valid):
            nu = nu - mcol[d] * _shift_rows(xu, d)
            nw = nw - mcol[d] * _shift_rows(xw, d)
        xu, xw = nu, nw
    ws_qs = _dot(_bf(jnp.concatenate([xw, q * eg], axis=0)), _bf(s))
    v_new = xu - ws_qs[:SUBLANES]
    o = ws_qs[SUBLANES:] + jnp.sum(q * k, axis=-1, keepdims=True) * v_new
    for d in range(1, t_valid):
        a = jnp.where(rows >= d, jnp.sum(q * _shift_rows(k, d), axis=-1, keepdims=True) * dec[d], 0.0)
        o = o + a * _shift_rows(v_new, d)
    s_new = jnp.exp(gc) * s + _dot_tn(_bf(k * jnp.exp(gc - g)), _bf(v_new))
    return o, s_new


def _short_kernel(hq_ref, hf_ref, hi_ref, hgate_ref, qkv_ref, z_ref, ab_ref, buf_ref,
                  lbl_ref, hgout_ref, wconv_ref, alog_ref, dtb_ref, ggout_ref, shg0_ref, sgd0_ref,
                  ohg_ref, ogd_ref, shg_ref, sgd_ref, ext_scr, *, t_valid, layer):
    bb = hq_ref.shape[0]
    rows = lax.broadcasted_iota(jnp.int32, (SUBLANES, LANES), 0)
    lb = _forget_lower_bound(lbl_ref[...], layer)
    hgout = hgout_ref[...]
    ggout = ggout_ref[...]
    wconv = wconv_ref[...]
    alog = alog_ref[...]
    dtb = dtb_ref[...]

    def per_seq(i, carry):
        ext_scr[SUBLANES - 3:SUBLANES, :] = buf_ref[i]
        ext_scr[SUBLANES:2 * SUBLANES, :] = qkv_ref[i]
        qs, ks, v, glog, beta = _gdn_inputs(ext_scr, SUBLANES, wconv, ab_ref[i], alog, dtb, t_valid)
        for h in range(HEADS):
            sl = slice(h * HEAD_DIM, (h + 1) * HEAD_DIM)
            o, s_new = _hgrn_short(hq_ref[i, :, sl], hf_ref[i, :, sl], hi_ref[i, :, sl], lb[:, sl],
                                   shg0_ref[i, h], rows, t_valid)
            shg_ref[i, h] = s_new
            ohg_ref[i, :, sl] = _rms(o, hgout[:, sl]) * jax.nn.silu(hgate_ref[i, :, sl])
            g_b = jnp.broadcast_to(glog[:, h:h + 1], (SUBLANES, LANES))
            beta_b = jnp.broadcast_to(beta[:, HEADS + h:HEADS + h + 1], (SUBLANES, LANES))
            o, s_new = _gdn_short(qs[h], ks[h], v[:, sl], g_b, beta_b, sgd0_ref[i, h], rows, t_valid)
            sgd_ref[i, h] = s_new
            ogd_ref[i, :, sl] = _rms(o, ggout) * jax.nn.silu(z_ref[i, :, sl])
        return carry

    lax.fori_loop(0, bb, per_seq, 0)


def _mixers_short(hgp, gqkv, gz, ab, conv_buf, lb_logits, hg_out, w_conv, a_log, dt_bias, gdn_out,
                  s_hg0, s_gdn0, t_valid, layer):
    bsz, t, wq = gqkv.shape
    assert t == SUBLANES and CONV_W - 1 <= t_valid <= SUBLANES
    bb = DECODE_BLOCK
    assert bsz % bb == 0
    w = HEADS * HEAD_DIM
    col = lambda j: pl.BlockSpec((bb, t, w), lambda b, j=j: (b, 0, j))
    tok = lambda width: pl.BlockSpec((bb, t, width), lambda b: (b, 0, 0))
    st = pl.BlockSpec((bb, HEADS, HEAD_DIM, HEAD_DIM), lambda b: (b, 0, 0, 0))
    return pl.pallas_call(
        functools.partial(_short_kernel, t_valid=t_valid, layer=layer),
        grid=(bsz // bb,),
        in_specs=[col(0), col(1), col(2), col(3), tok(wq), tok(w), tok(LANES),
                  pl.BlockSpec((bb, CONV_W - 1, wq), lambda b: (b, 0, 0)),
                  _resident(lb_logits.shape), _resident((1, w)), _resident((CONV_W, wq)),
                  _resident((1, LANES)), _resident((1, LANES)), _resident((1, HEAD_DIM)), st, st],
        out_specs=[tok(w), tok(w), st, st],
        out_shape=[jax.ShapeDtypeStruct((bsz, t, w), F32), jax.ShapeDtypeStruct((bsz, t, w), F32),
                   jax.ShapeDtypeStruct(s_hg0.shape, F32), jax.ShapeDtypeStruct(s_gdn0.shape, F32)],
        scratch_shapes=[pltpu.VMEM((2 * SUBLANES, wq), F32)],
        compiler_params=_cparams("arbitrary"),
        name="mixers_short",
    )(hgp, hgp, hgp, hgp, gqkv, gz, ab, conv_buf, lb_logits, hg_out.reshape(1, w), w_conv,
      _pad_lanes(a_log), _pad_lanes(jnp.concatenate([dt_bias, jnp.zeros_like(dt_bias)])),
      gdn_out.reshape(1, HEAD_DIM), s_hg0, s_gdn0)


def _mem_attn_kernel(q_ref, k_ref, v_ref, o_ref):
    bb = q_ref.shape[0]

    def per_seq(i, carry):
        for h in range(HEADS):
            sl = slice(h * HEAD_DIM, (h + 1) * HEAD_DIM)
            s = _dot_nt(_bf(q_ref[i, :, sl]), _bf(k_ref[i, :, sl])) * (HEAD_DIM ** -0.5)
            e = jnp.exp(s - jnp.max(s, axis=-1, keepdims=True))
            o = _dot(_bf(e), _bf(v_ref[i, :, sl]))
            o_ref[i, :, sl] = o / jnp.sum(e, axis=-1, keepdims=True)
        return carry

    lax.fori_loop(0, bb, per_seq, 0)


def _mem_attn(mq, mem_k, mem_v, bb, tq):
    bsz, t, w = mq.shape
    m = mem_k.shape[1]
    assert bsz % bb == 0 and t % tq == 0
    kv = pl.BlockSpec((bb, m, w), lambda b, n: (b, 0, 0))
    return pl.pallas_call(
        _mem_attn_kernel,
        grid=(bsz // bb, t // tq),
        in_specs=[pl.BlockSpec((bb, tq, w), lambda b, n: (b, n, 0)), kv, kv],
        out_specs=pl.BlockSpec((bb, tq, w), lambda b, n: (b, n, 0)),
        out_shape=jax.ShapeDtypeStruct((bsz, t, w), F32),
        compiler_params=_cparams("arbitrary", "arbitrary"),
        name="mem_attn",
    )(mq, mem_k, mem_v)


def _merge_ffn_kernel(x_ref, ohg_ref, ogd_ref, omem_ref, gates_ref,
                      wbh_ref, wbg_ref, wbm_ref, wout_ref, gpm_ref, gpf_ref,
                      wgate_ref, wup_ref, wdown_ref, gpo_ref, y_ref):
    d = x_ref.shape[-1]
    gates = gates_ref[...]
    merged = (jax.nn.sigmoid(gates[:, 0:d]) * _dot(_bf(ohg_ref[...]), wbh_ref[...])
              + jax.nn.sigmoid(gates[:, d:2 * d]) * _dot(_bf(ogd_ref[...]), wbg_ref[...])
              + jax.nn.sigmoid(gates[:, 2 * d:3 * d]) * _dot(_bf(omem_ref[...]), wbm_ref[...]))
    h = x_ref[...] + _rms(_dot(_bf(merged), wout_ref[...]), gpm_ref[...])
    hn = _bf(_rms(h, gpf_ref[...]))
    hidden = wgate_ref.shape[1]
    ff = None
    for c0 in range(0, hidden, FFN_CHUNK):
        a = jax.nn.silu(_dot(hn, wgate_ref[:, c0:c0 + FFN_CHUNK])) * _dot(hn, wup_ref[:, c0:c0 + FFN_CHUNK])
        p = _dot(_bf(a), wdown_ref[c0:c0 + FFN_CHUNK, :])
        ff = p if ff is None else ff + p
    y_ref[...] = h + _rms(ff, gpo_ref[...])


def _merge_ffn(x2d, ohg, ogd, omem, gates, wbh, wbg, wbm, wout, gpm, gpf, wgate, wup, wdown, gpo):
    n, d = x2d.shape
    tm = min(TOKEN_TILE, n)
    hidden = wgate.shape[1]
    assert n % tm == 0 and hidden % FFN_CHUNK == 0
    tok = lambda width: pl.BlockSpec((tm, width), lambda i: (i, 0))
    row = lambda a: a.reshape(1, d)
    return pl.pallas_call(
        _merge_ffn_kernel,
        grid=(n // tm,),
        in_specs=[tok(d), tok(ohg.shape[1]), tok(ogd.shape[1]), tok(omem.shape[1]), tok(gates.shape[1]),
                  _resident(wbh.shape), _resident(wbg.shape), _resident(wbm.shape), _resident(wout.shape),
                  _resident((1, d)), _resident((1, d)),
                  _resident(wgate.shape), _resident(wup.shape), _resident(wdown.shape), _resident((1, d))],
        out_specs=tok(d),
        out_shape=jax.ShapeDtypeStruct((n, d), F32),
        compiler_params=_cparams("arbitrary"),
        name="merge_ffn",
    )(x2d, ohg, ogd, omem, gates, wbh, wbg, wbm, wout, row(gpm), row(gpf), wgate, wup, wdown, row(gpo))


def _pack_w_in(w_in):
    d = w_in.shape[0]
    w = HEADS * HEAD_DIM
    c_ab = 4 * w + 3 * w + w
    c_mq = c_ab + 2 * HEADS
    c_gt = c_mq + w
    ab = jnp.zeros((d, LANES), w_in.dtype).at[:, :2 * HEADS].set(w_in[:, c_ab:c_mq])
    packed = jnp.concatenate([w_in[:, :c_ab], w_in[:, c_mq:c_gt], w_in[:, c_gt:], ab], axis=1)
    widths = (4 * w, 3 * w, w, w, N_BRANCH * d, LANES)
    return _bf(packed), widths


def _layer(x, mem_k, mem_v, conv_buf, s_hg, s_gdn, layer, lb_logits, p):
    bsz, t_valid, d = x.shape
    short = t_valid <= SUBLANES
    t = SUBLANES if short else t_valid
    if t != t_valid:
        x = jnp.pad(x, ((0, 0), (0, t - t_valid), (0, 0)))
    x2d = x.reshape(bsz * t, d)
    w = HEADS * HEAD_DIM

    hgp, gqkv, gz, mq, gates, ab = _norm_proj(x2d, p["g_pre_mix"], p["w_in"], p["w_in_widths"], "in_proj")
    as3d = lambda a: a.reshape(bsz, t, a.shape[-1])
    gqkv3 = as3d(gqkv)
    if short:
        o_hg, o_gdn, s_hg_new, s_gdn_new = _mixers_short(
            as3d(hgp), gqkv3, as3d(gz), as3d(ab), conv_buf, lb_logits, p["g_hg_out"], p["w_conv"],
            p["a_log"], p["dt_bias"], p["g_gdn_out"], s_hg, s_gdn, t_valid, layer)
        o_mem = _mem_attn(as3d(mq), mem_k, mem_v, DECODE_BLOCK, t)
    else:
        o_hg, s_hg_new = _hgrn_long(as3d(hgp), lb_logits, p["g_hg_out"], s_hg, layer)
        o_gdn, s_gdn_new = _gdn_long(gqkv3, as3d(gz), as3d(ab), conv_buf, p["w_conv"], p["a_log"],
                                     p["dt_bias"], p["g_gdn_out"], s_gdn)
        o_mem = _mem_attn(as3d(mq), mem_k, mem_v, 1, min(ATTN_TILE, t))
    assert t_valid >= CONV_W - 1
    conv_new = gqkv3[:, t_valid - (CONV_W - 1):t_valid, :]

    flat = lambda a: a.reshape(bsz * t, w)
    y = _merge_ffn(x2d, flat(o_hg), flat(o_gdn), flat(o_mem), gates,
                   p["w_br_hg"], p["w_br_gdn"], p["w_br_mem"], p["w_out"], p["g_post_mix"],
                   p["g_pre_ffn"], p["w_ffn_gate"], p["w_ffn_up"], p["w_ffn_out"], p["g_post_ffn"])
    y = y.reshape(bsz, t, d)[:, :t_valid]
    return y, conv_new, s_hg_new, s_gdn_new


def kernel(x_prompt, x_sample, mem_prompt, cache_mem_k, cache_mem_v, state_hgrn, state_gdn, state_gdn_conv, hg_lb_logits, g_pre_mix, w_in, w_conv, a_log, dt_bias, g_hg_out, g_gdn_out, g_mem, w_mem_kv, w_br_hg, w_br_gdn, w_br_mem, w_out, g_post_mix, g_pre_ffn, w_ffn_in, w_ffn_out, g_post_ffn):
    depth = w_in.shape[0]
    bp, _, d = x_prompt.shape
    m = mem_prompt.shape[1]
    w = HEADS * HEAD_DIM
    yp, ys = x_prompt, x_sample
    outs = [[] for _ in range(8)]
    for l in range(depth):
        hidden = w_ffn_out.shape[1]
        w_in_packed, widths = _pack_w_in(w_in[l])
        p = dict(g_pre_mix=g_pre_mix[l], w_in=w_in_packed, w_in_widths=widths, w_conv=w_conv[l],
                 a_log=a_log[l], dt_bias=dt_bias[l], g_hg_out=g_hg_out[l], g_gdn_out=g_gdn_out[l],
                 w_br_hg=_bf(w_br_hg[l]), w_br_gdn=_bf(w_br_gdn[l]), w_br_mem=_bf(w_br_mem[l]),
                 w_out=_bf(w_out[l]), g_post_mix=g_post_mix[l], g_pre_ffn=g_pre_ffn[l],
                 w_ffn_gate=_bf(w_ffn_in[l][:, :hidden]), w_ffn_up=_bf(w_ffn_in[l][:, hidden:]),
                 w_ffn_out=_bf(w_ffn_out[l]), g_post_ffn=g_post_ffn[l])
        mk, mv = _norm_proj(mem_prompt.reshape(bp * m, d), g_mem[l], _bf(w_mem_kv[l]), (w, w), "mem_kv")
        mk, mv = mk.reshape(bp, m, w), mv.reshape(bp, m, w)
        zeros_state = jnp.zeros((bp, HEADS, HEAD_DIM, HEAD_DIM), F32)
        yp, cb, sh, sg = _layer(yp, mk, mv, jnp.zeros((bp, CONV_W - 1, 3 * w), F32),
                                zeros_state, zeros_state, l, hg_lb_logits, p)
        bs = x_sample.shape[0]
        ys, cb2, sh2, sg2 = _layer(ys, cache_mem_k[l].reshape(bs, m, w), cache_mem_v[l].reshape(bs, m, w),
                                   state_gdn_conv[l], state_hgrn[l], state_gdn[l], l, hg_lb_logits, p)
        for lst, val in zip(outs, (mk.reshape(bp, m, HEADS, HEAD_DIM), mv.reshape(bp, m, HEADS, HEAD_DIM),
                                   sh, sg, cb, sh2, sg2, cb2)):
            lst.append(val)
    return (yp, ys) + tuple(jnp.stack(o) for o in outs)
```

```python
import functools
import math

import numpy as np
import jax
import jax.numpy as jnp
from jax import lax
from jax.experimental import pallas as pl
from jax.experimental.pallas import tpu as pltpu

F32 = jnp.float32
BF16 = jnp.bfloat16

EPS = 1e-6
HEADS = 4
HEAD_DIM = 128
CONV_W = 4
N_BRANCH = 3

LANES = 128
SUBLANES = 8
VMEM_LIMIT = 56 * 1024 * 1024

HG_CHUNK = 128
GDN_CHUNK = 64
GDN_CHUNKS_PER_STEP = 4
TOKEN_TILE = 256
ATTN_TILE = 512
FFN_CHUNK = 256
DECODE_BLOCK = 8


def _cparams(*sem):
    return pltpu.CompilerParams(dimension_semantics=sem, vmem_limit_bytes=VMEM_LIMIT)


def _resident(shape):
    nd = len(shape)
    return pl.BlockSpec(shape, lambda *_: (0,) * nd, pipeline_mode=pl.Buffered(1))


def _bf(x):
    return x.astype(BF16)


def _dot(a, b):
    return jnp.dot(a, b, preferred_element_type=F32)


def _dot_nt(a, b):
    return lax.dot_general(a, b, (((1,), (1,)), ((), ())), preferred_element_type=F32)


def _dot_tn(a, b):
    return lax.dot_general(a, b, (((0,), (0,)), ((), ())), preferred_element_type=F32)


def _split2(x):
    hi = _bf(x)
    lo = _bf(x - hi.astype(F32))
    return hi, lo


def _dot_exact_lhs(w_bf16, x):
    hi, lo = _split2(x)
    return _dot(w_bf16, hi) + _dot(w_bf16, lo)


def _rms(x, g):
    return x * lax.rsqrt(jnp.mean(x * x, axis=-1, keepdims=True) + EPS) * g


def _norm_proj_kernel(x_ref, g_ref, w_ref, *out_refs):
    xn = _bf(_rms(x_ref[...], g_ref[...]))
    off = 0
    for o_ref in out_refs:
        width = o_ref.shape[-1]
        for c0 in range(0, width, 512):
            cw = min(512, width - c0)
            o_ref[:, c0:c0 + cw] = _dot(xn, w_ref[:, off + c0:off + c0 + cw])
        off += width


def _norm_proj(x2d, g, w_bf16, widths, name):
    n, d = x2d.shape
    tm = min(TOKEN_TILE, n)
    assert n % tm == 0 and sum(widths) == w_bf16.shape[1]
    return pl.pallas_call(
        _norm_proj_kernel,
        grid=(n // tm,),
        in_specs=[pl.BlockSpec((tm, d), lambda i: (i, 0)),
                  _resident((1, d)),
                  _resident(w_bf16.shape)],
        out_specs=[pl.BlockSpec((tm, w), lambda i: (i, 0)) for w in widths],
        out_shape=[jax.ShapeDtypeStruct((n, w), F32) for w in widths],
        compiler_params=_cparams("arbitrary"),
        name=name,
    )(x2d, g.reshape(1, d), w_bf16)


def _forget_lower_bound(logits, layer):
    m = jnp.max(logits, axis=0, keepdims=True)
    e = jnp.exp(logits - m)
    return jnp.sum(e[:layer + 1], axis=0, keepdims=True) / jnp.sum(e, axis=0, keepdims=True)


def _col_to_matrix(row):
    n = row.shape[-1]
    return jnp.broadcast_to(row, (n, n)).T


def _gdn_inputs(ext_ref, c, wconv, ab, alog, dtb, t_valid):
    acc = ext_ref[pl.ds(SUBLANES - 3, c), :] * wconv[0:1, :]
    for j in range(1, CONV_W):
        acc = acc + ext_ref[pl.ds(SUBLANES - 3 + j, c), :] * wconv[j:j + 1, :]
    qkv = jax.nn.silu(acc)
    w = HEADS * HEAD_DIM
    qs, ks = [], []
    for h in range(HEADS):
        qh = qkv[:, h * HEAD_DIM:(h + 1) * HEAD_DIM]
        kh = qkv[:, w + h * HEAD_DIM:w + (h + 1) * HEAD_DIM]
        qs.append(qh * lax.rsqrt(jnp.sum(qh * qh, axis=-1, keepdims=True) + EPS) * (HEAD_DIM ** -0.5))
        ks.append(kh * lax.rsqrt(jnp.sum(kh * kh, axis=-1, keepdims=True) + EPS))
    v = qkv[:, 2 * w:3 * w]
    glog = -jnp.exp(alog) * jax.nn.softplus(ab + dtb)
    beta = jax.nn.sigmoid(ab)
    if t_valid is not None:
        valid = lax.broadcasted_iota(jnp.int32, ab.shape, 0) < t_valid
        glog = jnp.where(valid, glog, 0.0)
        beta = jnp.where(valid, beta, 0.0)
    return qs, ks, v, glog, beta


def _hier_tables(c):
    t = np.arange(c)
    u = np.arange(c)
    tri = (u[None, :] <= t[:, None]).astype(np.float32)
    rows, masks = [], []
    b = c // 2
    while b >= 1:
        ref = (t // (2 * b)) * 2 * b + b - 1
        rows.append(tri - (u[None, :] <= ref[:, None]).astype(np.float32))
        right = (t % (2 * b)) >= b
        left = (t % (2 * b)) < b
        same = (t[:, None] // (2 * b)) == (t[None, :] // (2 * b))
        masks.append((right[:, None] & left[None, :] & same).astype(np.float32))
        b //= 2
    rows.append(tri)
    return np.concatenate(rows, 0), np.stack(masks, 0)


def _hgrn_chunk(q, hf, v, lb, s, wall_ref, mask_ref, c):
    f = lb + (1.0 - lb) * jax.nn.sigmoid(hf)
    logf = jnp.log(f)
    k = 1.0 - f
    nlev = mask_ref.shape[0]
    d = _dot_exact_lhs(wall_ref[...], logf)
    g = d[nlev * c:(nlev + 1) * c]
    gc = g[c - 1:c, :]
    o = _dot(_bf(q * jnp.exp(g)), _bf(s))
    a = None
    for l in range(nlev):
        e = jnp.exp(-jnp.abs(d[l * c:(l + 1) * c]))
        p = _dot_nt(_bf(q * e), _bf(k * e)) * mask_ref[l]
        a = p if a is None else a + p
    o = o + _dot(_bf(a), _bf(v)) + jnp.sum(q * k, axis=-1, keepdims=True) * v
    s_new = _col_to_matrix(jnp.exp(gc)) * s + _dot_tn(_bf(k * jnp.exp(gc - g)), _bf(v))
    return o, s_new


def _hgrn_kernel(q_ref, f_ref, i_ref, gate_ref, lbl_ref, gout_ref, wall_ref, mask_ref, s0_ref,
                 o_ref, sn_ref, s_scr, *, c, layer):
    n = pl.program_id(1)

    @pl.when(n == 0)
    def _():
        s_scr[...] = s0_ref[0]

    lb = _forget_lower_bound(lbl_ref[...], layer)
    gout = gout_ref[...]
    for h in range(HEADS):
        sl = slice(h * HEAD_DIM, (h + 1) * HEAD_DIM)
        o, s_new = _hgrn_chunk(q_ref[0, :, sl], f_ref[0, :, sl], i_ref[0, :, sl], lb[:, sl],
                               s_scr[h], wall_ref, mask_ref, c)
        s_scr[h] = s_new
        o_ref[0, :, sl] = _rms(o, gout[:, sl]) * jax.nn.silu(gate_ref[0, :, sl])

    @pl.when(n == pl.num_programs(1) - 1)
    def _():
        sn_ref[0] = s_scr[...]


def _hgrn_long(hgp, lb_logits, g_out, s0, layer):
    bsz, t, _ = hgp.shape
    c = HG_CHUNK
    assert t % c == 0
    w = HEADS * HEAD_DIM
    wall, masks = _hier_tables(c)
    col = lambda j: pl.BlockSpec((1, c, w), lambda b, n, j=j: (b, n, j))
    st = pl.BlockSpec((1, HEADS, HEAD_DIM, HEAD_DIM), lambda b, n: (b, 0, 0, 0))
    return pl.pallas_call(
        functools.partial(_hgrn_kernel, c=c, layer=layer),
        grid=(bsz, t // c),
        in_specs=[col(0), col(1), col(2), col(3),
                  _resident(lb_logits.shape), _resident((1, w)),
                  _resident(wall.shape), _resident(masks.shape), st],
        out_specs=[pl.BlockSpec((1, c, w), lambda b, n: (b, n, 0)), st],
        out_shape=[jax.ShapeDtypeStruct((bsz, t, w), F32),
                   jax.ShapeDtypeStruct(s0.shape, F32)],
        scratch_shapes=[pltpu.VMEM((HEADS, HEAD_DIM, HEAD_DIM), F32)],
        compiler_params=_cparams("arbitrary", "arbitrary"),
        name="hgrn_long",
    )(hgp, hgp, hgp, hgp, lb_logits, g_out.reshape(1, w),
      jnp.asarray(wall, BF16), jnp.asarray(masks, F32), s0)


def _bmm(a, b):
    return jnp.einsum("nij,njk->nik", a, b, preferred_element_type=F32)


def _bmm_nt(a, b):
    return jnp.einsum("nid,njd->nij", a, b, preferred_element_type=F32)


def _bmm_tn(a, b):
    return jnp.einsum("nci,ncj->nij", a, b, preferred_element_type=F32)


def _gdn_prepare(qs, ks, v, glog, beta, tri_bf, c, nch):
    nb = nch * HEADS
    rows = lambda x, ch: x[ch * c:(ch + 1) * c]
    per = lambda f: jnp.stack([f(ch, h) for ch in range(nch) for h in range(HEADS)], axis=0)
    q = per(lambda ch, h: rows(qs[h], ch))
    k = per(lambda ch, h: rows(ks[h], ch))
    vv = per(lambda ch, h: rows(v, ch)[:, h * HEAD_DIM:(h + 1) * HEAD_DIM])
    beta_b = per(lambda ch, h: jnp.broadcast_to(rows(beta, ch)[:, HEADS + h:HEADS + h + 1], (c, LANES)))
    lane = lax.broadcasted_iota(jnp.int32, (c, LANES), 1)
    g_small = None
    for ch in range(nch):
        part = jnp.where(lane < HEADS, rows(glog, ch), 0.0)
        part = part if ch == 0 else pltpu.roll(part, ch * HEADS, axis=1)
        g_small = part if g_small is None else g_small + part
    gcum = _dot_exact_lhs(tri_bf, g_small)
    gcum_t = jnp.concatenate([gcum, jnp.zeros((LANES - c, LANES), F32)], axis=0).T
    g = jnp.stack([jnp.broadcast_to(gcum[:, n:n + 1], (c, LANES)) for n in range(nb)], axis=0)
    g_row = jnp.stack([jnp.broadcast_to(gcum_t[n:n + 1, :c], (c, c)) for n in range(nb)], axis=0)
    ti = lax.broadcasted_iota(jnp.int32, (nb, c, c), 1)
    si = lax.broadcasted_iota(jnp.int32, (nb, c, c), 2)
    decay = jnp.exp(jnp.minimum(g[:, :, :c] - g_row, 0.0))
    kb = k * beta_b
    kq = _bmm_nt(_bf(jnp.concatenate([kb, q], axis=1)), _bf(k))
    m = jnp.where(ti > si, kq[:, :c] * decay, 0.0)
    aqk = jnp.where(ti >= si, kq[:, c:] * decay, 0.0)
    eg = jnp.exp(g)
    rhs = jnp.concatenate([vv * beta_b, kb * eg], axis=-1)
    npow = -m
    poff = npow
    for _ in range(int(math.log2(c)) - 1):
        npow = _bmm(_bf(npow), _bf(npow))
        poff = poff + npow + _bmm(_bf(poff), _bf(npow))
    x = rhs + _bmm(_bf(poff), _bf(rhs))
    u, w = x[:, :, :HEAD_DIM], x[:, :, HEAD_DIM:]
    gc = g[:, c - 1:c, :]
    return u, _bf(jnp.concatenate([w, q * eg], axis=1)), _bf(aqk), _bf(k * jnp.exp(gc - g)), jnp.exp(gc)


def _gdn_kernel(qkv_ref, z_ref, ab_ref, buf_ref, wconv_ref, alog_ref, dtb_ref, gout_ref, tri_ref,
                s0_ref, o_ref, sn_ref, s_scr, ext_scr, *, c, nch):
    n = pl.program_id(1)
    tb = c * nch

    @pl.when(n == 0)
    def _():
        s_scr[...] = s0_ref[0]
        ext_scr[SUBLANES - 3:SUBLANES, :] = buf_ref[0]

    @pl.when(n > 0)
    def _():
        ext_scr[0:SUBLANES, :] = ext_scr[tb:tb + SUBLANES, :]

    ext_scr[SUBLANES:SUBLANES + tb, :] = qkv_ref[0]
    qs, ks, v, glog, beta = _gdn_inputs(ext_scr, tb, wconv_ref[...], ab_ref[0], alog_ref[...],
                                        dtb_ref[...], None)
    u, wq, aqk, kg, egc = _gdn_prepare(qs, ks, v, glog, beta, tri_ref[...], c, nch)
    gout = gout_ref[...]
    s = s_scr[...]
    for ch in range(nch):
        sel = slice(ch * HEADS, (ch + 1) * HEADS)
        ws_qs = _bmm(wq[sel], _bf(s))
        v_new = u[sel] - ws_qs[:, :c]
        o = ws_qs[:, c:] + _bmm(aqk[sel], _bf(v_new))
        s = egc[sel] * s + _bmm_tn(kg[sel], _bf(v_new))
        for h in range(HEADS):
            sl = slice(h * HEAD_DIM, (h + 1) * HEAD_DIM)
            o_ref[0, ch * c:(ch + 1) * c, sl] = _rms(o[h], gout) * jax.nn.silu(z_ref[0, ch * c:(ch + 1) * c, sl])
    s_scr[...] = s

    @pl.when(n == pl.num_programs(1) - 1)
    def _():
        sn_ref[0] = s


def _pad_lanes(x):
    return jnp.zeros((1, LANES), F32).at[0, :x.shape[0]].set(x.astype(F32))


def _gdn_long(gqkv, gz, ab, conv_buf, w_conv, a_log, dt_bias, g_out, s0):
    bsz, t, wq = gqkv.shape
    c = GDN_CHUNK
    nch = GDN_CHUNKS_PER_STEP
    tb = c * nch
    assert t % tb == 0
    w = HEADS * HEAD_DIM
    tri = np.tril(np.ones((c, c), np.float32))
    st = pl.BlockSpec((1, HEADS, HEAD_DIM, HEAD_DIM), lambda b, n: (b, 0, 0, 0))
    return pl.pallas_call(
        functools.partial(_gdn_kernel, c=c, nch=nch),
        grid=(bsz, t // tb),
        in_specs=[pl.BlockSpec((1, tb, wq), lambda b, n: (b, n, 0)),
                  pl.BlockSpec((1, tb, w), lambda b, n: (b, n, 0)),
                  pl.BlockSpec((1, tb, LANES), lambda b, n: (b, n, 0)),
                  pl.BlockSpec((1, CONV_W - 1, wq), lambda b, n: (b, 0, 0)),
                  _resident((CONV_W, wq)), _resident((1, LANES)), _resident((1, LANES)),
                  _resident((1, HEAD_DIM)), _resident((c, c)), st],
        out_specs=[pl.BlockSpec((1, tb, w), lambda b, n: (b, n, 0)), st],
        out_shape=[jax.ShapeDtypeStruct((bsz, t, w), F32),
                   jax.ShapeDtypeStruct((bsz, HEADS, HEAD_DIM, HEAD_DIM), F32)],
        scratch_shapes=[pltpu.VMEM((HEADS, HEAD_DIM, HEAD_DIM), F32),
                        pltpu.VMEM((tb + SUBLANES, wq), F32)],
        compiler_params=_cparams("arbitrary", "arbitrary"),
        name="gdn_long",
    )(gqkv, gz, ab, conv_buf, w_conv,
      _pad_lanes(a_log), _pad_lanes(dt_bias),
      g_out.reshape(1, HEAD_DIM), jnp.asarray(tri, BF16), s0)


def _shift_rows(x, d):
    return x if d == 0 else pltpu.roll(x, d, axis=0)


def _cumsum_rows8(x, rows):
    for d in (1, 2, 4):
        x = x + jnp.where(rows >= d, pltpu.roll(x, d, axis=0), 0.0)
    return x


def _hgrn_short(q, hf, v, lb, s, rows, t_valid):
    f = lb + (1.0 - lb) * jax.nn.sigmoid(hf)
    valid = rows < t_valid
    logf = jnp.where(valid, jnp.log(f), 0.0)
    k = jnp.where(valid, 1.0 - f, 0.0)
    g = _cumsum_rows8(logf, rows)
    gc = g[SUBLANES - 1:SUBLANES, :]
    o = _dot(_bf(q * jnp.exp(g)), _bf(s))
    for d in range(t_valid):
        ok = rows >= d
        dec = jnp.exp(jnp.where(ok, g - _shift_rows(g, d), 0.0))
        a = jnp.sum(jnp.where(ok, q * _shift_rows(k, d) * dec, 0.0), axis=-1, keepdims=True)
        o = o + a * _shift_rows(v, d)
    s_new = _col_to_matrix(jnp.exp(gc)) * s + _dot_tn(_bf(k * jnp.exp(gc - g)), _bf(v))
    return o, s_new


def _gdn_short(q, k, v, g_b, beta_b, s, rows, t_valid):
    g = _cumsum_rows8(g_b, rows)
    gc = g[SUBLANES - 1:SUBLANES, :]
    eg = jnp.exp(g)
    kb = k * beta_b
    dec = [None] + [jnp.exp(jnp.where(rows >= d, g - _shift_rows(g, d), 0.0)) for d in range(1, t_valid)]
    mcol = [None] + [jnp.where(rows >= d, jnp.sum(kb * _shift_rows(k, d), axis=-1, keepdims=True) * dec[d], 0.0)
                     for d in range(1, t_valid)]
    ru, rw = v * beta_b, kb * eg
    xu, xw = ru, rw
    for _ in range(t_valid - 1):
        nu, nw = ru, rw
        for d in range(1, t_valid):
            nu = nu - mcol[d] * _shift_rows(xu, d)
            nw = nw - mcol[d] * _shift_rows(xw, d)
        xu, xw = nu, nw
    ws_qs = _dot(_bf(jnp.concatenate([xw, q * eg], axis=0)), _bf(s))
    v_new = xu - ws_qs[:SUBLANES]
    o = ws_qs[SUBLANES:] + jnp.sum(q * k, axis=-1, keepdims=True) * v_new
    for d in range(1, t_valid):
        a = jnp.where(rows >= d, jnp.sum(q * _shift_rows(k, d), axis=-1, keepdims=True) * dec[d], 0.0)
        o = o + a * _shift_rows(v_new, d)
    s_new = jnp.exp(gc) * s + _dot_tn(_bf(k * jnp.exp(gc - g)), _bf(v_new))
    return o, s_new


def _short_kernel(hq_ref, hf_ref, hi_ref, hgate_ref, qkv_ref, z_ref, ab_ref, buf_ref,
                  lbl_ref, hgout_ref, wconv_ref, alog_ref, dtb_ref, ggout_ref, shg0_ref, sgd0_ref,
                  ohg_ref, ogd_ref, shg_ref, sgd_ref, ext_scr, *, t_valid, layer):
    bb = hq_ref.shape[0]
    rows = lax.broadcasted_iota(jnp.int32, (SUBLANES, LANES), 0)
    lb = _forget_lower_bound(lbl_ref[...], layer)
    hgout = hgout_ref[...]
    ggout = ggout_ref[...]
    wconv = wconv_ref[...]
    alog = alog_ref[...]
    dtb = dtb_ref[...]

    def per_seq(i, carry):
        ext_scr[SUBLANES - 3:SUBLANES, :] = buf_ref[i]
        ext_scr[SUBLANES:2 * SUBLANES, :] = qkv_ref[i]
        qs, ks, v, glog, beta = _gdn_inputs(ext_scr, SUBLANES, wconv, ab_ref[i], alog, dtb, t_valid)
        for h in range(HEADS):
            sl = slice(h * HEAD_DIM, (h + 1) * HEAD_DIM)
            o, s_new = _hgrn_short(hq_ref[i, :, sl], hf_ref[i, :, sl], hi_ref[i, :, sl], lb[:, sl],
                                   shg0_ref[i, h], rows, t_valid)
            shg_ref[i, h] = s_new
            ohg_ref[i, :, sl] = _rms(o, hgout[:, sl]) * jax.nn.silu(hgate_ref[i, :, sl])
            g_b = jnp.broadcast_to(glog[:, h:h + 1], (SUBLANES, LANES))
            beta_b = jnp.broadcast_to(beta[:, HEADS + h:HEADS + h + 1], (SUBLANES, LANES))
            o, s_new = _gdn_short(qs[h], ks[h], v[:, sl], g_b, beta_b, sgd0_ref[i, h], rows, t_valid)
            sgd_ref[i, h] = s_new
            ogd_ref[i, :, sl] = _rms(o, ggout) * jax.nn.silu(z_ref[i, :, sl])
        return carry

    lax.fori_loop(0, bb, per_seq, 0)


def _mixers_short(hgp, gqkv, gz, ab, conv_buf, lb_logits, hg_out, w_conv, a_log, dt_bias, gdn_out,
                  s_hg0, s_gdn0, t_valid, layer):
    bsz, t, wq = gqkv.shape
    assert t == SUBLANES and CONV_W - 1 <= t_valid <= SUBLANES
    bb = DECODE_BLOCK
    assert bsz % bb == 0
    w = HEADS * HEAD_DIM
    col = lambda j: pl.BlockSpec((bb, t, w), lambda b, j=j: (b, 0, j))
    tok = lambda width: pl.BlockSpec((bb, t, width), lambda b: (b, 0, 0))
    st = pl.BlockSpec((bb, HEADS, HEAD_DIM, HEAD_DIM), lambda b: (b, 0, 0, 0))
    return pl.pallas_call(
        functools.partial(_short_kernel, t_valid=t_valid, layer=layer),
        grid=(bsz // bb,),
        in_specs=[col(0), col(1), col(2), col(3), tok(wq), tok(w), tok(LANES),
                  pl.BlockSpec((bb, CONV_W - 1, wq), lambda b: (b, 0, 0)),
                  _resident(lb_logits.shape), _resident((1, w)), _resident((CONV_W, wq)),
                  _resident((1, LANES)), _resident((1, LANES)), _resident((1, HEAD_DIM)), st, st],
        out_specs=[tok(w), tok(w), st, st],
        out_shape=[jax.ShapeDtypeStruct((bsz, t, w), F32), jax.ShapeDtypeStruct((bsz, t, w), F32),
                   jax.ShapeDtypeStruct(s_hg0.shape, F32), jax.ShapeDtypeStruct(s_gdn0.shape, F32)],
        scratch_shapes=[pltpu.VMEM((2 * SUBLANES, wq), F32)],
        compiler_params=_cparams("arbitrary"),
        name="mixers_short",
    )(hgp, hgp, hgp, hgp, gqkv, gz, ab, conv_buf, lb_logits, hg_out.reshape(1, w), w_conv,
      _pad_lanes(a_log), _pad_lanes(jnp.concatenate([dt_bias, jnp.zeros_like(dt_bias)])),
      gdn_out.reshape(1, HEAD_DIM), s_hg0, s_gdn0)


def _mem_attn_kernel(q_ref, k_ref, v_ref, o_ref):
    bb = q_ref.shape[0]

    def per_seq(i, carry):
        for h in range(HEADS):
            sl = slice(h * HEAD_DIM, (h + 1) * HEAD_DIM)
            s = _dot_nt(_bf(q_ref[i, :, sl]), _bf(k_ref[i, :, sl])) * (HEAD_DIM ** -0.5)
            e = jnp.exp(s - jnp.max(s, axis=-1, keepdims=True))
            o = _dot(_bf(e), _bf(v_ref[i, :, sl]))
            o_ref[i, :, sl] = o / jnp.sum(e, axis=-1, keepdims=True)
        return carry

    lax.fori_loop(0, bb, per_seq, 0)


def _mem_attn(mq, mem_k, mem_v, bb, tq):
    bsz, t, w = mq.shape
    m = mem_k.shape[1]
    assert bsz % bb == 0 and t % tq == 0
    kv = pl.BlockSpec((bb, m, w), lambda b, n: (b, 0, 0))
    return pl.pallas_call(
        _mem_attn_kernel,
        grid=(bsz // bb, t // tq),
        in_specs=[pl.BlockSpec((bb, tq, w), lambda b, n: (b, n, 0)), kv, kv],
        out_specs=pl.BlockSpec((bb, tq, w), lambda b, n: (b, n, 0)),
        out_shape=jax.ShapeDtypeStruct((bsz, t, w), F32),
        compiler_params=_cparams("arbitrary", "arbitrary"),
        name="mem_attn",
    )(mq, mem_k, mem_v)


def _merge_ffn_kernel(x_ref, ohg_ref, ogd_ref, omem_ref, gates_ref,
                      wbh_ref, wbg_ref, wbm_ref, wout_ref, gpm_ref, gpf_ref,
                      wffn_ref, wdown_ref, gpo_ref, y_ref):
    d = x_ref.shape[-1]
    gates = gates_ref[...]
    merged = (jax.nn.sigmoid(gates[:, 0:d]) * _dot(_bf(ohg_ref[...]), wbh_ref[...])
              + jax.nn.sigmoid(gates[:, d:2 * d]) * _dot(_bf(ogd_ref[...]), wbg_ref[...])
              + jax.nn.sigmoid(gates[:, 2 * d:3 * d]) * _dot(_bf(omem_ref[...]), wbm_ref[...]))
    h = x_ref[...] + _rms(_dot(_bf(merged), wout_ref[...]), gpm_ref[...])
    hn = _bf(_rms(h, gpf_ref[...]))
    hidden = wdown_ref.shape[0]
    ff = None
    for c0 in range(0, hidden, FFN_CHUNK):
        a = (jax.nn.silu(_dot(hn, wffn_ref[:, c0:c0 + FFN_CHUNK]))
             * _dot(hn, wffn_ref[:, hidden + c0:hidden + c0 + FFN_CHUNK]))
        p = _dot(_bf(a), wdown_ref[c0:c0 + FFN_CHUNK, :])
        ff = p if ff is None else ff + p
    y_ref[...] = h + _rms(ff, gpo_ref[...])


def _merge_ffn(x2d, ohg, ogd, omem, gates, wbh, wbg, wbm, wout, gpm, gpf, wffn, wdown, gpo):
    n, d = x2d.shape
    tm = min(TOKEN_TILE, n)
    hidden = wdown.shape[0]
    assert n % tm == 0 and hidden % FFN_CHUNK == 0 and wffn.shape[1] == 2 * hidden
    tok = lambda width: pl.BlockSpec((tm, width), lambda i: (i, 0))
    row = lambda a: a.reshape(1, d)
    return pl.pallas_call(
        _merge_ffn_kernel,
        grid=(n // tm,),
        in_specs=[tok(d), tok(ohg.shape[1]), tok(ogd.shape[1]), tok(omem.shape[1]), tok(gates.shape[1]),
                  _resident(wbh.shape), _resident(wbg.shape), _resident(wbm.shape), _resident(wout.shape),
                  _resident((1, d)), _resident((1, d)),
                  _resident(wffn.shape), _resident(wdown.shape), _resident((1, d))],
        out_specs=tok(d),
        out_shape=jax.ShapeDtypeStruct((n, d), F32),
        compiler_params=_cparams("arbitrary"),
        name="merge_ffn",
    )(x2d, ohg, ogd, omem, gates, wbh, wbg, wbm, wout, row(gpm), row(gpf), wffn, wdown, row(gpo))


def _pack_w_in(w_in):
    d = w_in.shape[0]
    w = HEADS * HEAD_DIM
    c_ab = 4 * w + 3 * w + w
    c_mq = c_ab + 2 * HEADS
    c_gt = c_mq + w
    ab = jnp.zeros((d, LANES), w_in.dtype).at[:, :2 * HEADS].set(w_in[:, c_ab:c_mq])
    packed = jnp.concatenate([w_in[:, :c_ab], w_in[:, c_mq:c_gt], w_in[:, c_gt:], ab], axis=1)
    widths = (4 * w, 3 * w, w, w, N_BRANCH * d, LANES)
    return _bf(packed), widths


def _of_layer(a, l):
    return a.reshape(a.shape[1:]) if a.shape[0] == 1 else a[l]


def _layer(x, mem_k, mem_v, conv_buf, s_hg, s_gdn, layer, lb_logits, p):
    bsz, t_valid, d = x.shape
    short = t_valid <= SUBLANES
    t = SUBLANES if short else t_valid
    if t != t_valid:
        x = jnp.pad(x, ((0, 0), (0, t - t_valid), (0, 0)))
    x2d = x.reshape(bsz * t, d)
    w = HEADS * HEAD_DIM

    hgp, gqkv, gz, mq, gates, ab = _norm_proj(x2d, p["g_pre_mix"], p["w_in"], p["w_in_widths"], "in_proj")
    as3d = lambda a: a.reshape(bsz, t, a.shape[-1])
    gqkv3 = as3d(gqkv)
    if short:
        o_hg, o_gdn, s_hg_new, s_gdn_new = _mixers_short(
            as3d(hgp), gqkv3, as3d(gz), as3d(ab), conv_buf, lb_logits, p["g_hg_out"], p["w_conv"],
            p["a_log"], p["dt_bias"], p["g_gdn_out"], s_hg, s_gdn, t_valid, layer)
        o_mem = _mem_attn(as3d(mq), mem_k, mem_v, DECODE_BLOCK, t)
    else:
        o_hg, s_hg_new = _hgrn_long(as3d(hgp), lb_logits, p["g_hg_out"], s_hg, layer)
        o_gdn, s_gdn_new = _gdn_long(gqkv3, as3d(gz), as3d(ab), conv_buf, p["w_conv"], p["a_log"],
                                     p["dt_bias"], p["g_gdn_out"], s_gdn)
        o_mem = _mem_attn(as3d(mq), mem_k, mem_v, 1, min(ATTN_TILE, t))
    assert t_valid >= CONV_W - 1
    conv_new = gqkv3[:, t_valid - (CONV_W - 1):t_valid, :]

    flat = lambda a: a.reshape(bsz * t, w)
    y = _merge_ffn(x2d, flat(o_hg), flat(o_gdn), flat(o_mem), gates,
                   p["w_br_hg"], p["w_br_gdn"], p["w_br_mem"], p["w_out"], p["g_post_mix"],
                   p["g_pre_ffn"], p["w_ffn_in"], p["w_ffn_out"], p["g_post_ffn"])
    y = y.reshape(bsz, t, d)[:, :t_valid]
    return y, conv_new, s_hg_new, s_gdn_new


def kernel(x_prompt, x_sample, mem_prompt, cache_mem_k, cache_mem_v, state_hgrn, state_gdn, state_gdn_conv, hg_lb_logits, g_pre_mix, w_in, w_conv, a_log, dt_bias, g_hg_out, g_gdn_out, g_mem, w_mem_kv, w_br_hg, w_br_gdn, w_br_mem, w_out, g_post_mix, g_pre_ffn, w_ffn_in, w_ffn_out, g_post_ffn):
    depth = w_in.shape[0]
    bp, _, d = x_prompt.shape
    m = mem_prompt.shape[1]
    w = HEADS * HEAD_DIM
    yp, ys = x_prompt, x_sample
    outs = [[] for _ in range(8)]
    for l in range(depth):
        w_in_packed, widths = _pack_w_in(w_in[l])
        p = dict(g_pre_mix=g_pre_mix[l], w_in=w_in_packed, w_in_widths=widths, w_conv=w_conv[l],
                 a_log=a_log[l], dt_bias=dt_bias[l], g_hg_out=g_hg_out[l], g_gdn_out=g_gdn_out[l],
                 w_br_hg=_bf(w_br_hg[l]), w_br_gdn=_bf(w_br_gdn[l]), w_br_mem=_bf(w_br_mem[l]),
                 w_out=_bf(w_out[l]), g_post_mix=g_post_mix[l], g_pre_ffn=g_pre_ffn[l],
                 w_ffn_in=_bf(w_ffn_in[l]), w_ffn_out=_bf(w_ffn_out[l]), g_post_ffn=g_post_ffn[l])
        mk, mv = _norm_proj(mem_prompt.reshape(bp * m, d), g_mem[l], _bf(w_mem_kv[l]), (w, w), "mem_kv")
        mk, mv = mk.reshape(bp, m, w), mv.reshape(bp, m, w)
        zeros_state = jnp.zeros((bp, HEADS, HEAD_DIM, HEAD_DIM), F32)
        yp, cb, sh, sg = _layer(yp, mk, mv, jnp.zeros((bp, CONV_W - 1, 3 * w), F32),
                                zeros_state, zeros_state, l, hg_lb_logits, p)
        bs = x_sample.shape[0]
        ys, cb2, sh2, sg2 = _layer(ys, _of_layer(cache_mem_k, l).reshape(bs, m, w),
                                   _of_layer(cache_mem_v, l).reshape(bs, m, w),
                                   _of_layer(state_gdn_conv, l), _of_layer(state_hgrn, l),
                                   _of_layer(state_gdn, l), l, hg_lb_logits, p)
        for lst, val in zip(outs, (mk.reshape(bp, m, HEADS, HEAD_DIM), mv.reshape(bp, m, HEADS, HEAD_DIM),
                                   sh, sg, cb, sh2, sg2, cb2)):
            lst.append(val)
    return (yp, ys) + tuple(o[0].reshape((1,) + o[0].shape) if depth == 1 else jnp.stack(o) for o in outs)
```

```python
import functools
import math

import numpy as np
import jax
import jax.numpy as jnp
from jax import lax
from jax.experimental import pallas as pl
from jax.experimental.pallas import tpu as pltpu

F32 = jnp.float32
BF16 = jnp.bfloat16

EPS = 1e-6
HEADS = 4
HEAD_DIM = 128
CONV_W = 4
N_BRANCH = 3

LANES = 128
SUBLANES = 8
VMEM_LIMIT = 56 * 1024 * 1024

HG_CHUNK = 128
HG_CHUNKS_PER_STEP = 2
GDN_CHUNK = 64
GDN_CHUNKS_PER_STEP = 4
TOKEN_TILE = 512
ATTN_TILE = 512
FFN_CHUNK = 256
DECODE_BLOCK = 8
DECODE_INTERLEAVE = 2
ATTN_DECODE_BLOCK = 8


def _cparams(*sem):
    return pltpu.CompilerParams(dimension_semantics=sem, vmem_limit_bytes=VMEM_LIMIT)


def _resident(shape):
    nd = len(shape)
    return pl.BlockSpec(shape, lambda *_: (0,) * nd, pipeline_mode=pl.Buffered(1))


def _bf(x):
    return x.astype(BF16)


def _dot(a, b):
    return jnp.dot(a, b, preferred_element_type=F32)


def _dot_nt(a, b):
    return lax.dot_general(a, b, (((1,), (1,)), ((), ())), preferred_element_type=F32)


def _dot_tn(a, b):
    return lax.dot_general(a, b, (((0,), (0,)), ((), ())), preferred_element_type=F32)


def _split2(x):
    hi = _bf(x)
    lo = _bf(x - hi.astype(F32))
    return hi, lo


def _dot_exact_lhs(w_bf16, x):
    hi, lo = _split2(x)
    return _dot(w_bf16, hi) + _dot(w_bf16, lo)


def _rms(x, g):
    return x * lax.rsqrt(jnp.mean(x * x, axis=-1, keepdims=True) + EPS) * g


def _norm_proj_kernel(x_ref, g_ref, w_ref, *out_refs):
    xn = _bf(_rms(x_ref[...], g_ref[...]))
    off = 0
    for o_ref in out_refs:
        width = o_ref.shape[-1]
        for c0 in range(0, width, 512):
            cw = min(512, width - c0)
            o_ref[:, c0:c0 + cw] = _dot(xn, w_ref[:, off + c0:off + c0 + cw])
        off += width


def _norm_proj(x2d, g, w_bf16, widths, name):
    n, d = x2d.shape
    tm = min(TOKEN_TILE, n)
    assert n % tm == 0 and sum(widths) == w_bf16.shape[1]
    return pl.pallas_call(
        _norm_proj_kernel,
        grid=(n // tm,),
        in_specs=[pl.BlockSpec((tm, d), lambda i: (i, 0)),
                  _resident((1, d)),
                  _resident(w_bf16.shape)],
        out_specs=[pl.BlockSpec((tm, w), lambda i: (i, 0)) for w in widths],
        out_shape=[jax.ShapeDtypeStruct((n, w), F32) for w in widths],
        compiler_params=_cparams("arbitrary"),
        name=name,
    )(x2d, g.reshape(1, d), w_bf16)


def _forget_lower_bound(logits, layer):
    m = jnp.max(logits, axis=0, keepdims=True)
    e = jnp.exp(logits - m)
    return jnp.sum(e[:layer + 1], axis=0, keepdims=True) / jnp.sum(e, axis=0, keepdims=True)


def _col_to_matrix(row):
    n = row.shape[-1]
    return jnp.broadcast_to(row, (n, n)).T


def _gdn_inputs(ext_ref, c, wconv, ab, alog, dtb, t_valid):
    acc = ext_ref[pl.ds(SUBLANES - 3, c), :] * wconv[0:1, :]
    for j in range(1, CONV_W):
        acc = acc + ext_ref[pl.ds(SUBLANES - 3 + j, c), :] * wconv[j:j + 1, :]
    qkv = jax.nn.silu(acc)
    w = HEADS * HEAD_DIM
    qs, ks = [], []
    for h in range(HEADS):
        qh = qkv[:, h * HEAD_DIM:(h + 1) * HEAD_DIM]
        kh = qkv[:, w + h * HEAD_DIM:w + (h + 1) * HEAD_DIM]
        qs.append(qh * lax.rsqrt(jnp.sum(qh * qh, axis=-1, keepdims=True) + EPS) * (HEAD_DIM ** -0.5))
        ks.append(kh * lax.rsqrt(jnp.sum(kh * kh, axis=-1, keepdims=True) + EPS))
    v = qkv[:, 2 * w:3 * w]
    glog = -jnp.exp(alog) * jax.nn.softplus(ab + dtb)
    beta = jax.nn.sigmoid(ab)
    if t_valid is not None:
        valid = lax.broadcasted_iota(jnp.int32, ab.shape, 0) < t_valid
        glog = jnp.where(valid, glog, 0.0)
        beta = jnp.where(valid, beta, 0.0)
    return qs, ks, v, glog, beta


def _hier_tables(c):
    t = np.arange(c)
    u = np.arange(c)
    tri = (u[None, :] <= t[:, None]).astype(np.float32)
    rows, masks = [], []
    b = c // 2
    while b >= 1:
        ref = (t // (2 * b)) * 2 * b + b - 1
        rows.append(tri - (u[None, :] <= ref[:, None]).astype(np.float32))
        right = (t % (2 * b)) >= b
        left = (t % (2 * b)) < b
        same = (t[:, None] // (2 * b)) == (t[None, :] // (2 * b))
        masks.append((right[:, None] & left[None, :] & same).astype(np.float32))
        b //= 2
    rows.append(tri)
    return np.concatenate(rows, 0), np.stack(masks, 0)


def _hgrn_chunk(q, k, v, logf_hi, logf_lo, s, wall_ref, mask_ref, c):
    nlev = mask_ref.shape[0]
    d = _dot(wall_ref[...], jnp.concatenate([logf_hi, logf_lo], axis=0))
    g = d[nlev * c:(nlev + 1) * c]
    gc = g[c - 1:c, :]
    qg = _bf(q * jnp.exp(g))
    kg = _bf(k * jnp.exp(gc - g))
    egc = jnp.exp(gc)
    vb = _bf(v)
    a = [None] * HEADS
    for l in range(nlev):
        e = jnp.exp(-jnp.abs(d[l * c:(l + 1) * c]))
        ql, kl = _bf(q * e), _bf(k * e)
        for h in range(HEADS):
            sl = slice(h * HEAD_DIM, (h + 1) * HEAD_DIM)
            p = _dot_nt(ql[:, sl], kl[:, sl]) * mask_ref[l]
            a[h] = p if a[h] is None else a[h] + p
    qk = q * k
    outs, s_new = [], []
    for h in range(HEADS):
        sl = slice(h * HEAD_DIM, (h + 1) * HEAD_DIM)
        o = _dot(jnp.concatenate([qg[:, sl], _bf(a[h])], axis=1),
                 jnp.concatenate([_bf(s[h]), vb[:, sl]], axis=0))
        outs.append(o + jnp.sum(qk[:, sl], axis=-1, keepdims=True) * v[:, sl])
        s_new.append(_col_to_matrix(egc[:, sl]) * s[h] + _dot_tn(kg[:, sl], vb[:, sl]))
    return outs, s_new


def _hgrn_kernel(q_ref, f_ref, i_ref, gate_ref, lbl_ref, gout_ref, wall_ref, mask_ref, s0_ref,
                 o_ref, sn_ref, s_scr, *, c, nch, layer):
    n = pl.program_id(1)

    @pl.when(n == 0)
    def _():
        s_scr[...] = s0_ref[0]

    lb = _forget_lower_bound(lbl_ref[...], layer)
    gout = gout_ref[...]
    f = lb + (1.0 - lb) * jax.nn.sigmoid(f_ref[0])
    hi, lo = _split2(jnp.log(f))
    k = 1.0 - f
    s = [s_scr[h] for h in range(HEADS)]
    for ch in range(nch):
        r = slice(ch * c, (ch + 1) * c)
        outs, s = _hgrn_chunk(q_ref[0, r, :], k[r], i_ref[0, r, :], hi[r], lo[r], s, wall_ref, mask_ref, c)
        for h in range(HEADS):
            sl = slice(h * HEAD_DIM, (h + 1) * HEAD_DIM)
            o_ref[0, r, sl] = _rms(outs[h], gout[:, sl]) * jax.nn.silu(gate_ref[0, r, sl])
    for h in range(HEADS):
        s_scr[h] = s[h]

    @pl.when(n == pl.num_programs(1) - 1)
    def _():
        sn_ref[0] = s_scr[...]


def _hgrn_long(hgp, lb_logits, g_out, s0, layer):
    bsz, t, _ = hgp.shape
    c = HG_CHUNK
    nch = HG_CHUNKS_PER_STEP
    tb = c * nch
    assert t % tb == 0
    w = HEADS * HEAD_DIM
    wall, masks = _hier_tables(c)
    wall = np.concatenate([wall, wall], axis=1)
    col = lambda j: pl.BlockSpec((1, tb, w), lambda b, n, j=j: (b, n, j))
    st = pl.BlockSpec((1, HEADS, HEAD_DIM, HEAD_DIM), lambda b, n: (b, 0, 0, 0))
    return pl.pallas_call(
        functools.partial(_hgrn_kernel, c=c, nch=nch, layer=layer),
        grid=(bsz, t // tb),
        in_specs=[col(0), col(1), col(2), col(3),
                  _resident(lb_logits.shape), _resident((1, w)),
                  _resident(wall.shape), _resident(masks.shape), st],
        out_specs=[pl.BlockSpec((1, tb, w), lambda b, n: (b, n, 0)), st],
        out_shape=[jax.ShapeDtypeStruct((bsz, t, w), F32),
                   jax.ShapeDtypeStruct(s0.shape, F32)],
        scratch_shapes=[pltpu.VMEM((HEADS, HEAD_DIM, HEAD_DIM), F32)],
        compiler_params=_cparams("arbitrary", "arbitrary"),
        name="hgrn_long",
    )(hgp, hgp, hgp, hgp, lb_logits, g_out.reshape(1, w),
      jnp.asarray(wall, BF16), jnp.asarray(masks, F32), s0)


def _bmm(a, b):
    return jnp.einsum("nij,njk->nik", a, b, preferred_element_type=F32)


def _bmm_nt(a, b):
    return jnp.einsum("nid,njd->nij", a, b, preferred_element_type=F32)


def _bmm_tn(a, b):
    return jnp.einsum("nci,ncj->nij", a, b, preferred_element_type=F32)


def _gdn_prepare(qs, ks, v, glog, beta, tri_bf, c, nch):
    nb = nch * HEADS
    rows = lambda x, ch: x[ch * c:(ch + 1) * c]
    per = lambda f: jnp.stack([f(ch, h) for ch in range(nch) for h in range(HEADS)], axis=0)
    q = per(lambda ch, h: rows(qs[h], ch))
    k = per(lambda ch, h: rows(ks[h], ch))
    vv = per(lambda ch, h: rows(v, ch)[:, h * HEAD_DIM:(h + 1) * HEAD_DIM])
    beta_b = per(lambda ch, h: jnp.broadcast_to(rows(beta, ch)[:, HEADS + h:HEADS + h + 1], (c, LANES)))
    lane = lax.broadcasted_iota(jnp.int32, (c, LANES), 1)
    g_small = None
    for ch in range(nch):
        part = jnp.where(lane < HEADS, rows(glog, ch), 0.0)
        part = part if ch == 0 else pltpu.roll(part, ch * HEADS, axis=1)
        g_small = part if g_small is None else g_small + part
    gcum = _dot_exact_lhs(tri_bf, g_small)
    gcum_t = jnp.concatenate([gcum, jnp.zeros((LANES - c, LANES), F32)], axis=0).T
    g = jnp.stack([jnp.broadcast_to(gcum[:, n:n + 1], (c, LANES)) for n in range(nb)], axis=0)
    g_row = jnp.stack([jnp.broadcast_to(gcum_t[n:n + 1, :c], (c, c)) for n in range(nb)], axis=0)
    ti = lax.broadcasted_iota(jnp.int32, (nb, c, c), 1)
    si = lax.broadcasted_iota(jnp.int32, (nb, c, c), 2)
    decay = jnp.exp(jnp.minimum(g[:, :, :c] - g_row, 0.0))
    kb = k * beta_b
    kq = _bmm_nt(_bf(jnp.concatenate([kb, q], axis=1)), _bf(k))
    m = jnp.where(ti > si, kq[:, :c] * decay, 0.0)
    aqk = jnp.where(ti >= si, kq[:, c:] * decay, 0.0)
    eg = jnp.exp(g)
    rhs = jnp.concatenate([vv * beta_b, kb * eg], axis=-1)
    npow = -m
    poff = npow
    for _ in range(int(math.log2(c)) - 1):
        npow = _bmm(_bf(npow), _bf(npow))
        poff = poff + npow + _bmm(_bf(poff), _bf(npow))
    x = rhs + _bmm(_bf(poff), _bf(rhs))
    u, w = x[:, :, :HEAD_DIM], x[:, :, HEAD_DIM:]
    gc = g[:, c - 1:c, :]
    return u, _bf(jnp.concatenate([w, q * eg], axis=1)), _bf(aqk), _bf(k * jnp.exp(gc - g)), jnp.exp(gc)


def _gdn_kernel(qkv_ref, z_ref, ab_ref, buf_ref, wconv_ref, alog_ref, dtb_ref, gout_ref, tri_ref,
                s0_ref, o_ref, sn_ref, s_scr, ext_scr, *, c, nch):
    n = pl.program_id(1)
    tb = c * nch

    @pl.when(n == 0)
    def _():
        s_scr[...] = s0_ref[0]
        ext_scr[SUBLANES - 3:SUBLANES, :] = buf_ref[0]

    @pl.when(n > 0)
    def _():
        ext_scr[0:SUBLANES, :] = ext_scr[tb:tb + SUBLANES, :]

    ext_scr[SUBLANES:SUBLANES + tb, :] = qkv_ref[0]
    qs, ks, v, glog, beta = _gdn_inputs(ext_scr, tb, wconv_ref[...], ab_ref[0], alog_ref[...],
                                        dtb_ref[...], None)
    u, wq, aqk, kg, egc = _gdn_prepare(qs, ks, v, glog, beta, tri_ref[...], c, nch)
    gout = gout_ref[...]
    s = s_scr[...]
    for ch in range(nch):
        sel = slice(ch * HEADS, (ch + 1) * HEADS)
        ws_qs = _bmm(wq[sel], _bf(s))
        v_new = u[sel] - ws_qs[:, :c]
        o = ws_qs[:, c:] + _bmm(aqk[sel], _bf(v_new))
        s = egc[sel] * s + _bmm_tn(kg[sel], _bf(v_new))
        for h in range(HEADS):
            sl = slice(h * HEAD_DIM, (h + 1) * HEAD_DIM)
            o_ref[0, ch * c:(ch + 1) * c, sl] = _rms(o[h], gout) * jax.nn.silu(z_ref[0, ch * c:(ch + 1) * c, sl])
    s_scr[...] = s

    @pl.when(n == pl.num_programs(1) - 1)
    def _():
        sn_ref[0] = s


def _pad_lanes(x):
    return jnp.zeros((1, LANES), F32).at[0, :x.shape[0]].set(x.astype(F32))


def _gdn_long(gqkv, gz, ab, conv_buf, w_conv, a_log, dt_bias, g_out, s0):
    bsz, t, wq = gqkv.shape
    c = GDN_CHUNK
    nch = GDN_CHUNKS_PER_STEP
    tb = c * nch
    assert t % tb == 0
    w = HEADS * HEAD_DIM
    tri = np.tril(np.ones((c, c), np.float32))
    st = pl.BlockSpec((1, HEADS, HEAD_DIM, HEAD_DIM), lambda b, n: (b, 0, 0, 0))
    return pl.pallas_call(
        functools.partial(_gdn_kernel, c=c, nch=nch),
        grid=(bsz, t // tb),
        in_specs=[pl.BlockSpec((1, tb, wq), lambda b, n: (b, n, 0)),
                  pl.BlockSpec((1, tb, w), lambda b, n: (b, n, 0)),
                  pl.BlockSpec((1, tb, LANES), lambda b, n: (b, n, 0)),
                  pl.BlockSpec((1, CONV_W - 1, wq), lambda b, n: (b, 0, 0)),
                  _resident((CONV_W, wq)), _resident((1, LANES)), _resident((1, LANES)),
                  _resident((1, HEAD_DIM)), _resident((c, c)), st],
        out_specs=[pl.BlockSpec((1, tb, w), lambda b, n: (b, n, 0)), st],
        out_shape=[jax.ShapeDtypeStruct((bsz, t, w), F32),
                   jax.ShapeDtypeStruct((bsz, HEADS, HEAD_DIM, HEAD_DIM), F32)],
        scratch_shapes=[pltpu.VMEM((HEADS, HEAD_DIM, HEAD_DIM), F32),
                        pltpu.VMEM((tb + SUBLANES, wq), F32)],
        compiler_params=_cparams("arbitrary", "arbitrary"),
        name="gdn_long",
    )(gqkv, gz, ab, conv_buf, w_conv,
      _pad_lanes(a_log), _pad_lanes(dt_bias),
      g_out.reshape(1, HEAD_DIM), jnp.asarray(tri, BF16), s0)


def _shift_rows(x, d):
    return x if d == 0 else pltpu.roll(x, d, axis=0)


def _cumsum_rows8(x, rows):
    for d in (1, 2, 4):
        x = x + jnp.where(rows >= d, pltpu.roll(x, d, axis=0), 0.0)
    return x


def _hgrn_short(q, hf, v, lb, s, rows, t_valid):
    f = lb + (1.0 - lb) * jax.nn.sigmoid(hf)
    valid = rows < t_valid
    logf = jnp.where(valid, jnp.log(f), 0.0)
    k = jnp.where(valid, 1.0 - f, 0.0)
    g = _cumsum_rows8(logf, rows)
    gc = g[SUBLANES - 1:SUBLANES, :]
    o = _dot(_bf(q * jnp.exp(g)), _bf(s))
    for d in range(t_valid):
        ok = rows >= d
        dec = jnp.exp(jnp.where(ok, g - _shift_rows(g, d), 0.0))
        a = jnp.sum(jnp.where(ok, q * _shift_rows(k, d) * dec, 0.0), axis=-1, keepdims=True)
        o = o + a * _shift_rows(v, d)
    s_new = _col_to_matrix(jnp.exp(gc)) * s + _dot_tn(_bf(k * jnp.exp(gc - g)), _bf(v))
    return o, s_new


def _gdn_short(q, k, v, g_b, beta_b, s, rows, t_valid):
    g = _cumsum_rows8(g_b, rows)
    gc = g[SUBLANES - 1:SUBLANES, :]
    eg = jnp.exp(g)
    kb = k * beta_b
    dec = [None] + [jnp.exp(jnp.where(rows >= d, g - _shift_rows(g, d), 0.0)) for d in range(1, t_valid)]
    mcol = [None] + [jnp.where(rows >= d, jnp.sum(kb * _shift_rows(k, d), axis=-1, keepdims=True) * dec[d], 0.0)
                     for d in range(1, t_valid)]
    ru, rw = v * beta_b, kb * eg
    xu, xw = ru, rw
    for _ in range(t_valid - 1):
        nu, nw = ru, rw
        for d in range(1, t_valid):
            nu = nu - mcol[d] * _shift_rows(xu, d)
            nw = nw - mcol[d] * _shift_rows(xw, d)
        xu, xw = nu, nw
    ws_qs = _dot(_bf(jnp.concatenate([xw, q * eg], axis=0)), _bf(s))
    v_new = xu - ws_qs[:SUBLANES]
    o = ws_qs[SUBLANES:] + jnp.sum(q * k, axis=-1, keepdims=True) * v_new
    for d in range(1, t_valid):
        a = jnp.where(rows >= d, jnp.sum(q * _shift_rows(k, d), axis=-1, keepdims=True) * dec[d], 0.0)
        o = o + a * _shift_rows(v_new, d)
    s_new = jnp.exp(gc) * s + _dot_tn(_bf(k * jnp.exp(gc - g)), _bf(v_new))
    return o, s_new


def _short_kernel(hq_ref, hf_ref, hi_ref, hgate_ref, qkv_ref, z_ref, ab_ref, buf_ref,
                  lbl_ref, hgout_ref, wconv_ref, alog_ref, dtb_ref, ggout_ref, shg0_ref, sgd0_ref,
                  ohg_ref, ogd_ref, shg_ref, sgd_ref, ext_scr, *, t_valid, layer):
    bb = hq_ref.shape[0]
    rows = lax.broadcasted_iota(jnp.int32, (SUBLANES, LANES), 0)
    lb = _forget_lower_bound(lbl_ref[...], layer)
    hgout = hgout_ref[...]
    ggout = ggout_ref[...]
    wconv = wconv_ref[...]
    alog = alog_ref[...]
    dtb = dtb_ref[...]

    def per_seq(i, ext):
        ext[SUBLANES - 3:SUBLANES, :] = buf_ref[i]
        ext[SUBLANES:2 * SUBLANES, :] = qkv_ref[i]
        qs, ks, v, glog, beta = _gdn_inputs(ext, SUBLANES, wconv, ab_ref[i], alog, dtb, t_valid)
        for h in range(HEADS):
            sl = slice(h * HEAD_DIM, (h + 1) * HEAD_DIM)
            o, s_new = _hgrn_short(hq_ref[i, :, sl], hf_ref[i, :, sl], hi_ref[i, :, sl], lb[:, sl],
                                   shg0_ref[i, h], rows, t_valid)
            shg_ref[i, h] = s_new
            ohg_ref[i, :, sl] = _rms(o, hgout[:, sl]) * jax.nn.silu(hgate_ref[i, :, sl])
            g_b = jnp.broadcast_to(glog[:, h:h + 1], (SUBLANES, LANES))
            beta_b = jnp.broadcast_to(beta[:, HEADS + h:HEADS + h + 1], (SUBLANES, LANES))
            o, s_new = _gdn_short(qs[h], ks[h], v[:, sl], g_b, beta_b, sgd0_ref[i, h], rows, t_valid)
            sgd_ref[i, h] = s_new
            ogd_ref[i, :, sl] = _rms(o, ggout) * jax.nn.silu(z_ref[i, :, sl])

    def per_group(j, carry):
        for u in range(DECODE_INTERLEAVE):
            per_seq(j * DECODE_INTERLEAVE + u, ext_scr.at[u])
        return carry

    lax.fori_loop(0, bb // DECODE_INTERLEAVE, per_group, 0)


def _mixers_short(hgp, gqkv, gz, ab, conv_buf, lb_logits, hg_out, w_conv, a_log, dt_bias, gdn_out,
                  s_hg0, s_gdn0, t_valid, layer):
    bsz, t, wq = gqkv.shape
    assert t == SUBLANES and CONV_W - 1 <= t_valid <= SUBLANES
    bb = DECODE_BLOCK
    assert bsz % bb == 0
    w = HEADS * HEAD_DIM
    col = lambda j: pl.BlockSpec((bb, t, w), lambda b, j=j: (b, 0, j))
    tok = lambda width: pl.BlockSpec((bb, t, width), lambda b: (b, 0, 0))
    st = pl.BlockSpec((bb, HEADS, HEAD_DIM, HEAD_DIM), lambda b: (b, 0, 0, 0))
    return pl.pallas_call(
        functools.partial(_short_kernel, t_valid=t_valid, layer=layer),
        grid=(bsz // bb,),
        in_specs=[col(0), col(1), col(2), col(3), tok(wq), tok(w), tok(LANES),
                  pl.BlockSpec((bb, CONV_W - 1, wq), lambda b: (b, 0, 0)),
                  _resident(lb_logits.shape), _resident((1, w)), _resident((CONV_W, wq)),
                  _resident((1, LANES)), _resident((1, LANES)), _resident((1, HEAD_DIM)), st, st],
        out_specs=[tok(w), tok(w), st, st],
        out_shape=[jax.ShapeDtypeStruct((bsz, t, w), F32), jax.ShapeDtypeStruct((bsz, t, w), F32),
                   jax.ShapeDtypeStruct(s_hg0.shape, F32), jax.ShapeDtypeStruct(s_gdn0.shape, F32)],
        scratch_shapes=[pltpu.VMEM((DECODE_INTERLEAVE, 2 * SUBLANES, wq), F32)],
        compiler_params=_cparams("arbitrary"),
        name="mixers_short",
    )(hgp, hgp, hgp, hgp, gqkv, gz, ab, conv_buf, lb_logits, hg_out.reshape(1, w), w_conv,
      _pad_lanes(a_log), _pad_lanes(jnp.concatenate([dt_bias, jnp.zeros_like(dt_bias)])),
      gdn_out.reshape(1, HEAD_DIM), s_hg0, s_gdn0)


def _mem_attn_kernel(q_ref, k_ref, v_ref, o_ref):
    for h in range(HEADS):
        sl = slice(h * HEAD_DIM, (h + 1) * HEAD_DIM)
        s = _bmm_nt(_bf(q_ref[:, :, sl]), _bf(k_ref[:, :, sl])) * (HEAD_DIM ** -0.5)
        e = jnp.exp(s - jnp.max(s, axis=-1, keepdims=True))
        o_ref[:, :, sl] = _bmm(_bf(e), _bf(v_ref[:, :, sl])) / jnp.sum(e, axis=-1, keepdims=True)


def _mem_attn(mq, mem_k, mem_v, bb, tq):
    bsz, t, w = mq.shape
    m = mem_k.shape[1]
    assert bsz % bb == 0 and t % tq == 0
    kv = pl.BlockSpec((bb, m, w), lambda b, n: (b, 0, 0))
    return pl.pallas_call(
        _mem_attn_kernel,
        grid=(bsz // bb, t // tq),
        in_specs=[pl.BlockSpec((bb, tq, w), lambda b, n: (b, n, 0)), kv, kv],
        out_specs=pl.BlockSpec((bb, tq, w), lambda b, n: (b, n, 0)),
        out_shape=jax.ShapeDtypeStruct((bsz, t, w), F32),
        compiler_params=_cparams("arbitrary", "arbitrary"),
        name="mem_attn",
    )(mq, mem_k, mem_v)


def _merge_ffn_kernel(x_ref, ohg_ref, ogd_ref, omem_ref, gpre_ref, wgates_ref,
                      wbh_ref, wbg_ref, wbm_ref, wout_ref, gpm_ref, gpf_ref,
                      wffn_ref, wdown_ref, gpo_ref, y_ref):
    d = x_ref.shape[-1]
    x = x_ref[...]
    xn = _bf(_rms(x, gpre_ref[...]))
    merged = None
    for j, (o_ref, wb_ref) in enumerate(((ohg_ref, wbh_ref), (ogd_ref, wbg_ref), (omem_ref, wbm_ref))):
        gate = jax.nn.sigmoid(_dot(xn, wgates_ref[:, j * d:(j + 1) * d]))
        term = gate * _dot(_bf(o_ref[...]), wb_ref[...])
        merged = term if merged is None else merged + term
    h = x + _rms(_dot(_bf(merged), wout_ref[...]), gpm_ref[...])
    hn = _bf(_rms(h, gpf_ref[...]))
    hidden = wdown_ref.shape[0]
    ff = None
    for c0 in range(0, hidden, FFN_CHUNK):
        a = (jax.nn.silu(_dot(hn, wffn_ref[:, c0:c0 + FFN_CHUNK]))
             * _dot(hn, wffn_ref[:, hidden + c0:hidden + c0 + FFN_CHUNK]))
        p = _dot(_bf(a), wdown_ref[c0:c0 + FFN_CHUNK, :])
        ff = p if ff is None else ff + p
    y_ref[...] = h + _rms(ff, gpo_ref[...])


def _merge_ffn(x2d, ohg, ogd, omem, gpre, wgates, wbh, wbg, wbm, wout, gpm, gpf, wffn, wdown, gpo):
    n, d = x2d.shape
    tm = min(TOKEN_TILE, n)
    hidden = wdown.shape[0]
    assert n % tm == 0 and hidden % FFN_CHUNK == 0 and wffn.shape[1] == 2 * hidden
    assert wgates.shape == (d, N_BRANCH * d)
    tok = lambda width: pl.BlockSpec((tm, width), lambda i: (i, 0))
    row = lambda a: a.reshape(1, d)
    return pl.pallas_call(
        _merge_ffn_kernel,
        grid=(n // tm,),
        in_specs=[tok(d), tok(ohg.shape[1]), tok(ogd.shape[1]), tok(omem.shape[1]),
                  _resident((1, d)), _resident(wgates.shape),
                  _resident(wbh.shape), _resident(wbg.shape), _resident(wbm.shape), _resident(wout.shape),
                  _resident((1, d)), _resident((1, d)),
                  _resident(wffn.shape), _resident(wdown.shape), _resident((1, d))],
        out_specs=tok(d),
        out_shape=jax.ShapeDtypeStruct((n, d), F32),
        compiler_params=_cparams("arbitrary"),
        name="merge_ffn",
    )(x2d, ohg, ogd, omem, row(gpre), wgates, wbh, wbg, wbm, wout, row(gpm), row(gpf), wffn, wdown, row(gpo))


def _pack_w_in(w_in):
    d = w_in.shape[0]
    w = HEADS * HEAD_DIM
    c_ab = 4 * w + 3 * w + w
    c_mq = c_ab + 2 * HEADS
    c_gt = c_mq + w
    ab = jnp.zeros((d, LANES), w_in.dtype).at[:, :2 * HEADS].set(w_in[:, c_ab:c_mq])
    mixers = jnp.concatenate([w_in[:, :c_ab], w_in[:, c_mq:c_gt], ab], axis=1)
    widths = (4 * w, 3 * w, w, w, LANES)
    return _bf(mixers), widths, _bf(w_in[:, c_gt:])


def _of_layer(a, l):
    return a.reshape(a.shape[1:]) if a.shape[0] == 1 else a[l]


def _layer(x, mem_k, mem_v, conv_buf, s_hg, s_gdn, layer, lb_logits, p):
    bsz, t_valid, d = x.shape
    short = t_valid <= SUBLANES
    t = SUBLANES if short else t_valid
    if t != t_valid:
        x = jnp.pad(x, ((0, 0), (0, t - t_valid), (0, 0)))
    x2d = x.reshape(bsz * t, d)
    w = HEADS * HEAD_DIM

    hgp, gqkv, gz, mq, ab = _norm_proj(x2d, p["g_pre_mix"], p["w_in"], p["w_in_widths"], "in_proj")
    as3d = lambda a: a.reshape(bsz, t, a.shape[-1])
    gqkv3 = as3d(gqkv)
    if short:
        o_hg, o_gdn, s_hg_new, s_gdn_new = _mixers_short(
            as3d(hgp), gqkv3, as3d(gz), as3d(ab), conv_buf, lb_logits, p["g_hg_out"], p["w_conv"],
            p["a_log"], p["dt_bias"], p["g_gdn_out"], s_hg, s_gdn, t_valid, layer)
        o_mem = _mem_attn(as3d(mq), mem_k, mem_v, ATTN_DECODE_BLOCK, t)
    else:
        o_hg, s_hg_new = _hgrn_long(as3d(hgp), lb_logits, p["g_hg_out"], s_hg, layer)
        o_gdn, s_gdn_new = _gdn_long(gqkv3, as3d(gz), as3d(ab), conv_buf, p["w_conv"], p["a_log"],
                                     p["dt_bias"], p["g_gdn_out"], s_gdn)
        o_mem = _mem_attn(as3d(mq), mem_k, mem_v, 1, min(ATTN_TILE, t))
    assert t_valid >= CONV_W - 1
    conv_new = gqkv3[:, t_valid - (CONV_W - 1):t_valid, :]

    flat = lambda a: a.reshape(bsz * t, w)
    y = _merge_ffn(x2d, flat(o_hg), flat(o_gdn), flat(o_mem), p["g_pre_mix"], p["w_gates"],
                   p["w_br_hg"], p["w_br_gdn"], p["w_br_mem"], p["w_out"], p["g_post_mix"],
                   p["g_pre_ffn"], p["w_ffn_in"], p["w_ffn_out"], p["g_post_ffn"])
    y = y.reshape(bsz, t, d)[:, :t_valid]
    return y, conv_new, s_hg_new, s_gdn_new


def kernel(x_prompt, x_sample, mem_prompt, cache_mem_k, cache_mem_v, state_hgrn, state_gdn, state_gdn_conv, hg_lb_logits, g_pre_mix, w_in, w_conv, a_log, dt_bias, g_hg_out, g_gdn_out, g_mem, w_mem_kv, w_br_hg, w_br_gdn, w_br_mem, w_out, g_post_mix, g_pre_ffn, w_ffn_in, w_ffn_out, g_post_ffn):
    depth = w_in.shape[0]
    bp, _, d = x_prompt.shape
    m = mem_prompt.shape[1]
    w = HEADS * HEAD_DIM
    yp, ys = x_prompt, x_sample
    outs = [[] for _ in range(8)]
    for l in range(depth):
        w_in_packed, widths, w_gates = _pack_w_in(w_in[l])
        p = dict(g_pre_mix=g_pre_mix[l], w_in=w_in_packed, w_in_widths=widths, w_gates=w_gates,
                 w_conv=w_conv[l],
                 a_log=a_log[l], dt_bias=dt_bias[l], g_hg_out=g_hg_out[l], g_gdn_out=g_gdn_out[l],
                 w_br_hg=_bf(w_br_hg[l]), w_br_gdn=_bf(w_br_gdn[l]), w_br_mem=_bf(w_br_mem[l]),
                 w_out=_bf(w_out[l]), g_post_mix=g_post_mix[l], g_pre_ffn=g_pre_ffn[l],
                 w_ffn_in=_bf(w_ffn_in[l]), w_ffn_out=_bf(w_ffn_out[l]), g_post_ffn=g_post_ffn[l])
        mk, mv = _norm_proj(mem_prompt.reshape(bp * m, d), g_mem[l], _bf(w_mem_kv[l]), (w, w), "mem_kv")
        mk, mv = mk.reshape(bp, m, w), mv.reshape(bp, m, w)
        zeros_state = jnp.zeros((bp, HEADS, HEAD_DIM, HEAD_DIM), F32)
        yp, cb, sh, sg = _layer(yp, mk, mv, jnp.zeros((bp, CONV_W - 1, 3 * w), F32),
                                zeros_state, zeros_state, l, hg_lb_logits, p)
        bs = x_sample.shape[0]
        ys, cb2, sh2, sg2 = _layer(ys, _of_layer(cache_mem_k, l).reshape(bs, m, w),
                                   _of_layer(cache_mem_v, l).reshape(bs, m, w),
                                   _of_layer(state_gdn_conv, l), _of_layer(state_hgrn, l),
                                   _of_layer(state_gdn, l), l, hg_lb_logits, p)
        for lst, val in zip(outs, (mk.reshape(bp, m, HEADS, HEAD_DIM), mv.reshape(bp, m, HEADS, HEAD_DIM),
                                   sh, sg, cb, sh2, sg2, cb2)):
            lst.append(val)
    return (yp, ys) + tuple(o[0].reshape((1,) + o[0].shape) if depth == 1 else jnp.stack(o) for o in outs)
```

```python
import functools
import math

import numpy as np
import jax
import jax.numpy as jnp
from jax import lax
from jax.experimental import pallas as pl
from jax.experimental.pallas import tpu as pltpu

F32 = jnp.float32
BF16 = jnp.bfloat16

EPS = 1e-6
HEADS = 4
HEAD_DIM = 128
CONV_W = 4
N_BRANCH = 3

LANES = 128
SUBLANES = 8
VMEM_LIMIT = 56 * 1024 * 1024

HG_CHUNK = 128
HG_CHUNKS_PER_STEP = 4
GDN_CHUNK = 64
GDN_CHUNKS_PER_STEP = 8
TOKEN_TILE = 512
ATTN_TILE = 512
FFN_CHUNK = 256
DECODE_BLOCK = 8
DECODE_INTERLEAVE = 2
ATTN_DECODE_BLOCK = 8


def _cparams(*sem):
    return pltpu.CompilerParams(dimension_semantics=sem, vmem_limit_bytes=VMEM_LIMIT)


def _resident(shape):
    nd = len(shape)
    return pl.BlockSpec(shape, lambda *_: (0,) * nd, pipeline_mode=pl.Buffered(1))


def _bf(x):
    return x.astype(BF16)


def _dot(a, b):
    return jnp.dot(a, b, preferred_element_type=F32)


def _dot_nt(a, b):
    return lax.dot_general(a, b, (((1,), (1,)), ((), ())), preferred_element_type=F32)


def _dot_tn(a, b):
    return lax.dot_general(a, b, (((0,), (0,)), ((), ())), preferred_element_type=F32)


def _bmm(a, b):
    return jnp.einsum("nij,njk->nik", a, b, preferred_element_type=F32)


def _bmm_nt(a, b):
    return jnp.einsum("nid,njd->nij", a, b, preferred_element_type=F32)


def _bmm_tn(a, b):
    return jnp.einsum("nci,ncj->nij", a, b, preferred_element_type=F32)


def _split2(x):
    hi = _bf(x)
    lo = _bf(x - hi.astype(F32))
    return hi, lo


def _dot_exact_lhs(w_bf16, x):
    hi, lo = _split2(x)
    return _dot(w_bf16, hi) + _dot(w_bf16, lo)


def _rms(x, g):
    return x * lax.rsqrt(jnp.mean(x * x, axis=-1, keepdims=True) + EPS) * g


def _head(h):
    return slice(h * HEAD_DIM, (h + 1) * HEAD_DIM)


def _block_diag(x, nblk):
    wblk = x.shape[-1] // nblk
    if wblk % LANES == 0:
        zeros = lambda n: [jnp.zeros(x.shape[:-1] + (n * wblk,), x.dtype)] if n else []
        parts = [jnp.concatenate(zeros(j) + [x[..., j * wblk:(j + 1) * wblk]] + zeros(nblk - 1 - j), axis=-1)
                 for j in range(nblk)]
    else:
        lane = lax.broadcasted_iota(jnp.int32, x.shape, x.ndim - 1)
        parts = [jnp.where((lane >= j * wblk) & (lane < (j + 1) * wblk), x, jnp.zeros_like(x))
                 for j in range(nblk)]
    return jnp.concatenate(parts, axis=-2)


def _norm_proj_kernel(x_ref, g_ref, w_ref, *out_refs):
    xn = _bf(_rms(x_ref[...], g_ref[...]))
    off = 0
    for o_ref in out_refs:
        width = o_ref.shape[-1]
        for c0 in range(0, width, 512):
            cw = min(512, width - c0)
            o_ref[:, c0:c0 + cw] = _dot(xn, w_ref[:, off + c0:off + c0 + cw])
        off += width


def _norm_proj(x2d, g, w_bf16, widths, name):
    n, d = x2d.shape
    tm = min(TOKEN_TILE, n)
    assert n % tm == 0 and sum(widths) == w_bf16.shape[1]
    return pl.pallas_call(
        _norm_proj_kernel,
        grid=(n // tm,),
        in_specs=[pl.BlockSpec((tm, d), lambda i: (i, 0)),
                  _resident((1, d)),
                  _resident(w_bf16.shape)],
        out_specs=[pl.BlockSpec((tm, w), lambda i: (i, 0)) for w in widths],
        out_shape=[jax.ShapeDtypeStruct((n, w), F32) for w in widths],
        compiler_params=_cparams("arbitrary"),
        name=name,
    )(x2d, g.reshape(1, d), w_bf16)


def _forget_lower_bound(logits, layer):
    m = jnp.max(logits, axis=0, keepdims=True)
    e = jnp.exp(logits - m)
    return jnp.sum(e[:layer + 1], axis=0, keepdims=True) / jnp.sum(e, axis=0, keepdims=True)


def _col_to_matrix(row):
    n = row.shape[-1]
    return jnp.broadcast_to(row, (n, n)).T


def _gdn_inputs(ext_ref, c, wconv, ab, alog, dtb, t_valid):
    if c > SUBLANES:
        full = ext_ref[...]
        acc = full * wconv[CONV_W - 1:CONV_W, :]
        for j in range(CONV_W - 1):
            acc = acc + pltpu.roll(full, CONV_W - 1 - j, axis=0) * wconv[j:j + 1, :]
        acc = acc[SUBLANES:SUBLANES + c]
    else:
        acc = ext_ref[pl.ds(SUBLANES - 3, c), :] * wconv[0:1, :]
        for j in range(1, CONV_W):
            acc = acc + ext_ref[pl.ds(SUBLANES - 3 + j, c), :] * wconv[j:j + 1, :]
    qkv = jax.nn.silu(acc)
    w = HEADS * HEAD_DIM
    qs, ks = [], []
    for h in range(HEADS):
        qh = qkv[:, h * HEAD_DIM:(h + 1) * HEAD_DIM]
        kh = qkv[:, w + h * HEAD_DIM:w + (h + 1) * HEAD_DIM]
        qs.append(qh * lax.rsqrt(jnp.sum(qh * qh, axis=-1, keepdims=True) + EPS) * (HEAD_DIM ** -0.5))
        ks.append(kh * lax.rsqrt(jnp.sum(kh * kh, axis=-1, keepdims=True) + EPS))
    v = qkv[:, 2 * w:3 * w]
    glog = -jnp.exp(alog) * jax.nn.softplus(ab + dtb)
    beta = jax.nn.sigmoid(ab)
    if t_valid is not None:
        valid = lax.broadcasted_iota(jnp.int32, ab.shape, 0) < t_valid
        glog = jnp.where(valid, glog, 0.0)
        beta = jnp.where(valid, beta, 0.0)
    return qs, ks, v, glog, beta


def _hier_tables(c):
    t = np.arange(c)
    u = np.arange(c)
    tri = (u[None, :] <= t[:, None]).astype(np.float32)
    rows, masks = [], []
    b = c // 2
    while b >= 1:
        ref = (t // (2 * b)) * 2 * b + b - 1
        rows.append(tri - (u[None, :] <= ref[:, None]).astype(np.float32))
        right = (t % (2 * b)) >= b
        left = (t % (2 * b)) < b
        same = (t[:, None] // (2 * b)) == (t[None, :] // (2 * b))
        masks.append((right[:, None] & left[None, :] & same).astype(np.float32))
        b //= 2
    rows.append(tri)
    return np.concatenate(rows, 0), np.stack(masks, 0)


def _hgrn_chunk(q, k, v, logf_hi, logf_lo, s, wall_ref, masks, c):
    nlev = len(masks)
    d = _dot(wall_ref[...], jnp.concatenate([logf_hi, logf_lo], axis=0))
    g = d[nlev * c:(nlev + 1) * c]
    gc = g[c - 1:c, :]
    qg = _bf(q * jnp.exp(g))
    kg = _bf(k * jnp.exp(gc - g))
    egc = jnp.exp(gc)
    vb = _bf(v)
    npair = HEADS // 2
    a = [None] * npair
    for l in range(nlev):
        e = jnp.exp(-jnp.abs(d[l * c:(l + 1) * c]))
        ql, kl = _bf(q * e), _bf(k * e)
        for pr in range(npair):
            pair = slice(2 * pr * HEAD_DIM, (2 * pr + 2) * HEAD_DIM)
            p = _dot_nt(ql[:, pair], _block_diag(kl[:, pair], 2))
            a[pr] = jnp.where(masks[l], p, 0.0) if a[pr] is None else jnp.where(masks[l], p, a[pr])
    qk = q * k
    outs, s_new = [], []
    for h in range(HEADS):
        sl = _head(h)
        a_h = a[h // 2][:, (h % 2) * c:(h % 2 + 1) * c]
        o = _dot(jnp.concatenate([qg[:, sl], _bf(a_h)], axis=1),
                 jnp.concatenate([_bf(s[h]), vb[:, sl]], axis=0))
        outs.append(o + jnp.sum(qk[:, sl], axis=-1, keepdims=True) * v[:, sl])
        s_new.append(_col_to_matrix(egc[:, sl]) * s[h] + _dot_tn(kg[:, sl], vb[:, sl]))
    return outs, s_new


def _hgrn_kernel(q_ref, f_ref, i_ref, gate_ref, lbl_ref, gout_ref, wall_ref, mask_ref, s0_ref,
                 o_ref, sn_ref, s_scr, *, c, nch, layer):
    n = pl.program_id(1)

    @pl.when(n == 0)
    def _():
        s_scr[...] = s0_ref[0]

    lb = _forget_lower_bound(lbl_ref[...], layer)
    gout = gout_ref[...]
    f = lb + (1.0 - lb) * jax.nn.sigmoid(f_ref[0])
    hi, lo = _split2(jnp.log(f))
    k = 1.0 - f
    masks = [mask_ref[l] != 0.0 for l in range(mask_ref.shape[0])]
    s = [s_scr[h] for h in range(HEADS)]
    for ch in range(nch):
        r = slice(ch * c, (ch + 1) * c)
        outs, s = _hgrn_chunk(q_ref[0, r, :], k[r], i_ref[0, r, :], hi[r], lo[r], s, wall_ref, masks, c)
        for h in range(HEADS):
            sl = _head(h)
            o_ref[0, r, sl] = _rms(outs[h], gout[:, sl]) * jax.nn.silu(gate_ref[0, r, sl])
    for h in range(HEADS):
        s_scr[h] = s[h]

    @pl.when(n == pl.num_programs(1) - 1)
    def _():
        sn_ref[0] = s_scr[...]


def _hgrn_long(hgp, lb_logits, g_out, s0, layer):
    bsz, t, _ = hgp.shape
    c = HG_CHUNK
    nch = HG_CHUNKS_PER_STEP
    tb = c * nch
    assert t % tb == 0 and c == HEAD_DIM and HEADS % 2 == 0
    w = HEADS * HEAD_DIM
    wall, masks = _hier_tables(c)
    wall = np.concatenate([wall, wall], axis=1)
    masks = np.concatenate([masks, masks], axis=2)
    col = lambda j: pl.BlockSpec((1, tb, w), lambda b, n, j=j: (b, n, j))
    st = pl.BlockSpec((1, HEADS, HEAD_DIM, HEAD_DIM), lambda b, n: (b, 0, 0, 0))
    return pl.pallas_call(
        functools.partial(_hgrn_kernel, c=c, nch=nch, layer=layer),
        grid=(bsz, t // tb),
        in_specs=[col(0), col(1), col(2), col(3),
                  _resident(lb_logits.shape), _resident((1, w)),
                  _resident(wall.shape), _resident(masks.shape), st],
        out_specs=[pl.BlockSpec((1, tb, w), lambda b, n: (b, n, 0)), st],
        out_shape=[jax.ShapeDtypeStruct((bsz, t, w), F32),
                   jax.ShapeDtypeStruct(s0.shape, F32)],
        scratch_shapes=[pltpu.VMEM((HEADS, HEAD_DIM, HEAD_DIM), F32)],
        compiler_params=_cparams("arbitrary", "arbitrary"),
        name="hgrn_long",
    )(hgp, hgp, hgp, hgp, lb_logits, g_out.reshape(1, w),
      jnp.asarray(wall, BF16), jnp.asarray(masks, F32), s0)


def _gdn_prepare(qs, ks, v, glog, beta, tri_bf, c, nch):
    nb = nch * HEADS
    wc = HEADS * c
    rows = lambda x, ch: x[ch * c:(ch + 1) * c]
    per = lambda f: jnp.stack([f(ch, h) for ch in range(nch) for h in range(HEADS)], axis=0)
    q = per(lambda ch, h: rows(qs[h], ch))
    k = per(lambda ch, h: rows(ks[h], ch))
    vv = per(lambda ch, h: rows(v, ch)[:, _head(h)])
    beta_b = per(lambda ch, h: jnp.broadcast_to(rows(beta, ch)[:, HEADS + h:HEADS + h + 1], (c, LANES)))
    lane = lax.broadcasted_iota(jnp.int32, (c, LANES), 1)
    g_small = None
    for ch in range(nch):
        part = jnp.where(lane < HEADS, rows(glog, ch), 0.0)
        part = part if ch == 0 else pltpu.roll(part, ch * HEADS, axis=1)
        g_small = part if g_small is None else g_small + part
    gcum = _dot_exact_lhs(tri_bf, g_small)
    gcum_t = jnp.concatenate([gcum, jnp.zeros((LANES - c, LANES), F32)], axis=0).T
    g = jnp.stack([jnp.broadcast_to(gcum[:, n:n + 1], (c, LANES)) for n in range(nb)], axis=0)
    packed = lambda f: jnp.stack([jnp.concatenate([f(ch * HEADS + h) for h in range(HEADS)], axis=1)
                                  for ch in range(nch)], axis=0)
    g_col = packed(lambda n: jnp.broadcast_to(gcum[:, n:n + 1], (c, c)))
    g_row = packed(lambda n: jnp.broadcast_to(gcum_t[n:n + 1, :c], (c, c)))
    ti = lax.broadcasted_iota(jnp.int32, (nch, c, wc), 1)
    si = lax.broadcasted_iota(jnp.int32, (nch, c, wc), 2) & (c - 1)
    decay = jnp.exp(jnp.minimum(g_col - g_row, 0.0))
    kb = k * beta_b
    heads_on_lanes = lambda x: jnp.stack(
        [jnp.concatenate([x[ch * HEADS + h] for h in range(HEADS)], axis=1) for ch in range(nch)], axis=0)
    kbq = _bf(jnp.concatenate([heads_on_lanes(kb), heads_on_lanes(q)], axis=1))
    kq = _bmm_nt(kbq, _block_diag(_bf(heads_on_lanes(k)), HEADS))
    m = jnp.where(ti > si, kq[:, :c] * decay, 0.0)
    aqk = jnp.where(ti >= si, kq[:, c:] * decay, 0.0)
    eg = jnp.exp(g)
    rhs = jnp.concatenate([vv * beta_b, kb * eg], axis=-1)
    npow = -m
    poff = npow
    npow_bd = _block_diag(_bf(npow), HEADS)
    for _ in range(int(math.log2(c)) - 1):
        npow = _bmm(_bf(npow), npow_bd)
        npow_bd = _block_diag(_bf(npow), HEADS)
        poff = poff + npow + _bmm(_bf(poff), npow_bd)
    xoff = _bmm(_block_diag(_bf(poff), HEADS), _bf(rhs.reshape(nch, wc, 2 * HEAD_DIM)))
    x = rhs + xoff.reshape(nb, c, 2 * HEAD_DIM)
    u, w = x[:, :, :HEAD_DIM], x[:, :, HEAD_DIM:]
    gc = g[:, c - 1:c, :]
    return (u, _bf(jnp.concatenate([w, q * eg], axis=1)), _block_diag(_bf(aqk), HEADS),
            _bf(k * jnp.exp(gc - g)), jnp.exp(gc))


def _gdn_kernel(qkv_ref, z_ref, ab_ref, buf_ref, wconv_ref, alog_ref, dtb_ref, gout_ref, tri_ref,
                s0_ref, o_ref, sn_ref, s_scr, ext_scr, *, c, nch):
    n = pl.program_id(1)
    tb = c * nch

    @pl.when(n == 0)
    def _():
        s_scr[...] = s0_ref[0]
        ext_scr[0:SUBLANES - 3, :] = jnp.zeros((SUBLANES - 3, ext_scr.shape[1]), F32)
        ext_scr[SUBLANES - 3:SUBLANES, :] = buf_ref[0]

    @pl.when(n > 0)
    def _():
        ext_scr[0:SUBLANES, :] = ext_scr[tb:tb + SUBLANES, :]

    ext_scr[SUBLANES:SUBLANES + tb, :] = qkv_ref[0]
    qs, ks, v, glog, beta = _gdn_inputs(ext_scr, tb, wconv_ref[...], ab_ref[0], alog_ref[...],
                                        dtb_ref[...], None)
    u, wq, aqk_bd, kg, egc = _gdn_prepare(qs, ks, v, glog, beta, tri_ref[...], c, nch)
    gout = gout_ref[...]
    s = s_scr[...]
    for ch in range(nch):
        sel = slice(ch * HEADS, (ch + 1) * HEADS)
        ws_qs = _bmm(wq[sel], _bf(s))
        v_new = u[sel] - ws_qs[:, :c]
        intra = _dot(aqk_bd[ch], _bf(v_new.reshape(HEADS * c, HEAD_DIM)))
        o = ws_qs[:, c:] + intra.reshape(HEADS, c, HEAD_DIM)
        s = egc[sel] * s + _bmm_tn(kg[sel], _bf(v_new))
        for h in range(HEADS):
            sl = _head(h)
            o_ref[0, ch * c:(ch + 1) * c, sl] = _rms(o[h], gout) * jax.nn.silu(z_ref[0, ch * c:(ch + 1) * c, sl])
    s_scr[...] = s

    @pl.when(n == pl.num_programs(1) - 1)
    def _():
        sn_ref[0] = s


def _pad_lanes(x):
    return jnp.zeros((1, LANES), F32).at[0, :x.shape[0]].set(x.astype(F32))


def _gdn_long(gqkv, gz, ab, conv_buf, w_conv, a_log, dt_bias, g_out, s0):
    bsz, t, wq = gqkv.shape
    c = GDN_CHUNK
    nch = GDN_CHUNKS_PER_STEP
    tb = c * nch
    assert t % tb == 0
    w = HEADS * HEAD_DIM
    tri = np.tril(np.ones((c, c), np.float32))
    st = pl.BlockSpec((1, HEADS, HEAD_DIM, HEAD_DIM), lambda b, n: (b, 0, 0, 0))
    return pl.pallas_call(
        functools.partial(_gdn_kernel, c=c, nch=nch),
        grid=(bsz, t // tb),
        in_specs=[pl.BlockSpec((1, tb, wq), lambda b, n: (b, n, 0)),
                  pl.BlockSpec((1, tb, w), lambda b, n: (b, n, 0)),
                  pl.BlockSpec((1, tb, LANES), lambda b, n: (b, n, 0)),
                  pl.BlockSpec((1, CONV_W - 1, wq), lambda b, n: (b, 0, 0)),
                  _resident((CONV_W, wq)), _resident((1, LANES)), _resident((1, LANES)),
                  _resident((1, HEAD_DIM)), _resident((c, c)), st],
        out_specs=[pl.BlockSpec((1, tb, w), lambda b, n: (b, n, 0)), st],
        out_shape=[jax.ShapeDtypeStruct((bsz, t, w), F32),
                   jax.ShapeDtypeStruct((bsz, HEADS, HEAD_DIM, HEAD_DIM), F32)],
        scratch_shapes=[pltpu.VMEM((HEADS, HEAD_DIM, HEAD_DIM), F32),
                        pltpu.VMEM((tb + SUBLANES, wq), F32)],
        compiler_params=_cparams("arbitrary", "arbitrary"),
        name="gdn_long",
    )(gqkv, gz, ab, conv_buf, w_conv,
      _pad_lanes(a_log), _pad_lanes(dt_bias),
      g_out.reshape(1, HEAD_DIM), jnp.asarray(tri, BF16), s0)


def _shift_rows(x, d):
    return x if d == 0 else pltpu.roll(x, d, axis=0)


def _cumsum_rows8(x, rows):
    for d in (1, 2, 4):
        x = x + jnp.where(rows >= d, pltpu.roll(x, d, axis=0), 0.0)
    return x


def _hgrn_short(q, hf, v, lb, s, rows, t_valid):
    f = lb + (1.0 - lb) * jax.nn.sigmoid(hf)
    valid = rows < t_valid
    logf = jnp.where(valid, jnp.log(f), 0.0)
    k = jnp.where(valid, 1.0 - f, 0.0)
    g = _cumsum_rows8(logf, rows)
    gc = g[SUBLANES - 1:SUBLANES, :]
    o = _dot(_bf(q * jnp.exp(g)), _bf(s))
    for d in range(t_valid):
        ok = rows >= d
        dec = jnp.exp(jnp.where(ok, g - _shift_rows(g, d), 0.0))
        a = jnp.sum(jnp.where(ok, q * _shift_rows(k, d) * dec, 0.0), axis=-1, keepdims=True)
        o = o + a * _shift_rows(v, d)
    s_new = _col_to_matrix(jnp.exp(gc)) * s + _dot_tn(_bf(k * jnp.exp(gc - g)), _bf(v))
    return o, s_new


def _gdn_short(q, k, v, g_b, beta_b, s, rows, t_valid):
    g = _cumsum_rows8(g_b, rows)
    gc = g[SUBLANES - 1:SUBLANES, :]
    eg = jnp.exp(g)
    kb = k * beta_b
    dec = [None] + [jnp.exp(jnp.where(rows >= d, g - _shift_rows(g, d), 0.0)) for d in range(1, t_valid)]
    mcol = [None] + [jnp.where(rows >= d, jnp.sum(kb * _shift_rows(k, d), axis=-1, keepdims=True) * dec[d], 0.0)
                     for d in range(1, t_valid)]
    ru, rw = v * beta_b, kb * eg
    xu, xw = ru, rw
    for _ in range(t_valid - 1):
        nu, nw = ru, rw
        for d in range(1, t_valid):
            nu = nu - mcol[d] * _shift_rows(xu, d)
            nw = nw - mcol[d] * _shift_rows(xw, d)
        xu, xw = nu, nw
    ws_qs = _dot(_bf(jnp.concatenate([xw, q * eg], axis=0)), _bf(s))
    v_new = xu - ws_qs[:SUBLANES]
    o = ws_qs[SUBLANES:] + jnp.sum(q * k, axis=-1, keepdims=True) * v_new
    for d in range(1, t_valid):
        a = jnp.where(rows >= d, jnp.sum(q * _shift_rows(k, d), axis=-1, keepdims=True) * dec[d], 0.0)
        o = o + a * _shift_rows(v_new, d)
    s_new = jnp.exp(gc) * s + _dot_tn(_bf(k * jnp.exp(gc - g)), _bf(v_new))
    return o, s_new


def _short_kernel(hq_ref, hf_ref, hi_ref, hgate_ref, qkv_ref, z_ref, ab_ref, buf_ref,
                  lbl_ref, hgout_ref, wconv_ref, alog_ref, dtb_ref, ggout_ref, shg0_ref, sgd0_ref,
                  ohg_ref, ogd_ref, shg_ref, sgd_ref, ext_scr, *, t_valid, layer):
    bb = hq_ref.shape[0]
    rows = lax.broadcasted_iota(jnp.int32, (SUBLANES, LANES), 0)
    lb = _forget_lower_bound(lbl_ref[...], layer)
    hgout = hgout_ref[...]
    ggout = ggout_ref[...]
    wconv = wconv_ref[...]
    alog = alog_ref[...]
    dtb = dtb_ref[...]

    def per_seq(i, ext):
        ext[SUBLANES - 3:SUBLANES, :] = buf_ref[i]
        ext[SUBLANES:2 * SUBLANES, :] = qkv_ref[i]
        qs, ks, v, glog, beta = _gdn_inputs(ext, SUBLANES, wconv, ab_ref[i], alog, dtb, t_valid)
        for h in range(HEADS):
            sl = _head(h)
            o, s_new = _hgrn_short(hq_ref[i, :, sl], hf_ref[i, :, sl], hi_ref[i, :, sl], lb[:, sl],
                                   shg0_ref[i, h], rows, t_valid)
            shg_ref[i, h] = s_new
            ohg_ref[i, :, sl] = _rms(o, hgout[:, sl]) * jax.nn.silu(hgate_ref[i, :, sl])
            g_b = jnp.broadcast_to(glog[:, h:h + 1], (SUBLANES, LANES))
            beta_b = jnp.broadcast_to(beta[:, HEADS + h:HEADS + h + 1], (SUBLANES, LANES))
            o, s_new = _gdn_short(qs[h], ks[h], v[:, sl], g_b, beta_b, sgd0_ref[i, h], rows, t_valid)
            sgd_ref[i, h] = s_new
            ogd_ref[i, :, sl] = _rms(o, ggout) * jax.nn.silu(z_ref[i, :, sl])

    def per_group(j, carry):
        for u in range(DECODE_INTERLEAVE):
            per_seq(j * DECODE_INTERLEAVE + u, ext_scr.at[u])
        return carry

    lax.fori_loop(0, bb // DECODE_INTERLEAVE, per_group, 0)


def _mixers_short(hgp, gqkv, gz, ab, conv_buf, lb_logits, hg_out, w_conv, a_log, dt_bias, gdn_out,
                  s_hg0, s_gdn0, t_valid, layer):
    bsz, t, wq = gqkv.shape
    assert t == SUBLANES and CONV_W - 1 <= t_valid <= SUBLANES
    bb = DECODE_BLOCK
    assert bsz % bb == 0 and bb % DECODE_INTERLEAVE == 0
    w = HEADS * HEAD_DIM
    col = lambda j: pl.BlockSpec((bb, t, w), lambda b, j=j: (b, 0, j))
    tok = lambda width: pl.BlockSpec((bb, t, width), lambda b: (b, 0, 0))
    st = pl.BlockSpec((bb, HEADS, HEAD_DIM, HEAD_DIM), lambda b: (b, 0, 0, 0))
    return pl.pallas_call(
        functools.partial(_short_kernel, t_valid=t_valid, layer=layer),
        grid=(bsz // bb,),
        in_specs=[col(0), col(1), col(2), col(3), tok(wq), tok(w), tok(LANES),
                  pl.BlockSpec((bb, CONV_W - 1, wq), lambda b: (b, 0, 0)),
                  _resident(lb_logits.shape), _resident((1, w)), _resident((CONV_W, wq)),
                  _resident((1, LANES)), _resident((1, LANES)), _resident((1, HEAD_DIM)), st, st],
        out_specs=[tok(w), tok(w), st, st],
        out_shape=[jax.ShapeDtypeStruct((bsz, t, w), F32), jax.ShapeDtypeStruct((bsz, t, w), F32),
                   jax.ShapeDtypeStruct(s_hg0.shape, F32), jax.ShapeDtypeStruct(s_gdn0.shape, F32)],
        scratch_shapes=[pltpu.VMEM((DECODE_INTERLEAVE, 2 * SUBLANES, wq), F32)],
        compiler_params=_cparams("arbitrary"),
        name="mixers_short",
    )(hgp, hgp, hgp, hgp, gqkv, gz, ab, conv_buf, lb_logits, hg_out.reshape(1, w), w_conv,
      _pad_lanes(a_log), _pad_lanes(dt_bias), gdn_out.reshape(1, HEAD_DIM), s_hg0, s_gdn0)


def _mem_attn_kernel(q_ref, k_ref, v_ref, o_ref):
    for h in range(HEADS):
        sl = _head(h)
        s = _bmm_nt(_bf(q_ref[:, :, sl]), _bf(k_ref[:, :, sl])) * (HEAD_DIM ** -0.5)
        e = jnp.exp(s - jnp.max(s, axis=-1, keepdims=True))
        o_ref[:, :, sl] = _bmm(_bf(e), _bf(v_ref[:, :, sl])) / jnp.sum(e, axis=-1, keepdims=True)


def _mem_attn(mq, mem_k, mem_v, bb, tq):
    bsz, t, w = mq.shape
    m = mem_k.shape[1]
    assert bsz % bb == 0 and t % tq == 0
    kv = pl.BlockSpec((bb, m, w), lambda b, n: (b, 0, 0))
    return pl.pallas_call(
        _mem_attn_kernel,
        grid=(bsz // bb, t // tq),
        in_specs=[pl.BlockSpec((bb, tq, w), lambda b, n: (b, n, 0)), kv, kv],
        out_specs=pl.BlockSpec((bb, tq, w), lambda b, n: (b, n, 0)),
        out_shape=jax.ShapeDtypeStruct((bsz, t, w), F32),
        compiler_params=_cparams("arbitrary", "arbitrary"),
        name="mem_attn",
    )(mq, mem_k, mem_v)


def _mem_attn_cache_kernel(q_ref, k_hbm, v_hbm, o_ref, kbuf, vbuf, sem):
    i = pl.program_id(0)
    bb = q_ref.shape[0]

    def copies(step, slot):
        out = []
        for h in range(HEADS):
            for j, (src, dst) in enumerate(((k_hbm, kbuf), (v_hbm, vbuf))):
                out.append(pltpu.make_async_copy(src.at[pl.ds(step * bb, bb), :, h, :],
                                                 dst.at[slot, h], sem.at[j, slot, h]))
        return out

    slot = i % 2

    @pl.when(i == 0)
    def _():
        for cp in copies(0, 0):
            cp.start()

    @pl.when(i + 1 < pl.num_programs(0))
    def _():
        for cp in copies(i + 1, 1 - slot):
            cp.start()

    for cp in copies(i, slot):
        cp.wait()
    for h in range(HEADS):
        sl = _head(h)
        s = _bmm_nt(_bf(q_ref[:, :, sl]), _bf(kbuf[slot, h])) * (HEAD_DIM ** -0.5)
        e = jnp.exp(s - jnp.max(s, axis=-1, keepdims=True))
        o_ref[:, :, sl] = _bmm(_bf(e), _bf(vbuf[slot, h])) / jnp.sum(e, axis=-1, keepdims=True)


def _mem_attn_cache(mq, cache_k, cache_v, bb):
    bsz, t, w = mq.shape
    m = cache_k.shape[1]
    assert bsz % bb == 0 and cache_k.shape == (bsz, m, HEADS, HEAD_DIM)
    return pl.pallas_call(
        _mem_attn_cache_kernel,
        grid=(bsz // bb,),
        in_specs=[pl.BlockSpec((bb, t, w), lambda b: (b, 0, 0)),
                  pl.BlockSpec(memory_space=pl.ANY), pl.BlockSpec(memory_space=pl.ANY)],
        out_specs=pl.BlockSpec((bb, t, w), lambda b: (b, 0, 0)),
        out_shape=jax.ShapeDtypeStruct((bsz, t, w), F32),
        scratch_shapes=[pltpu.VMEM((2, HEADS, bb, m, HEAD_DIM), F32),
                        pltpu.VMEM((2, HEADS, bb, m, HEAD_DIM), F32),
                        pltpu.SemaphoreType.DMA((2, 2, HEADS))],
        compiler_params=_cparams("arbitrary"),
        name="mem_attn_cache",
    )(mq, cache_k, cache_v)


def _merge_ffn_kernel(x_ref, ohg_ref, ogd_ref, omem_ref, gpre_ref, wgates_ref,
                      wbh_ref, wbg_ref, wbm_ref, wout_ref, gpm_ref, gpf_ref,
                      wffn_ref, wdown_ref, gpo_ref, y_ref):
    d = x_ref.shape[-1]
    x = x_ref[...]
    xn = _bf(_rms(x, gpre_ref[...]))
    merged = None
    for j, (o_ref, wb_ref) in enumerate(((ohg_ref, wbh_ref), (ogd_ref, wbg_ref), (omem_ref, wbm_ref))):
        gate = jax.nn.sigmoid(_dot(xn, wgates_ref[:, j * d:(j + 1) * d]))
        term = gate * _dot(_bf(o_ref[...]), wb_ref[...])
        merged = term if merged is None else merged + term
    h = x + _rms(_dot(_bf(merged), wout_ref[...]), gpm_ref[...])
    hn = _bf(_rms(h, gpf_ref[...]))
    hidden = wdown_ref.shape[0]
    ff = None
    for c0 in range(0, hidden, FFN_CHUNK):
        a = (jax.nn.silu(_dot(hn, wffn_ref[:, c0:c0 + FFN_CHUNK]))
             * _dot(hn, wffn_ref[:, hidden + c0:hidden + c0 + FFN_CHUNK]))
        p = _dot(_bf(a), wdown_ref[c0:c0 + FFN_CHUNK, :])
        ff = p if ff is None else ff + p
    y_ref[...] = h + _rms(ff, gpo_ref[...])


def _merge_ffn(x2d, ohg, ogd, omem, gpre, wgates, wbh, wbg, wbm, wout, gpm, gpf, wffn, wdown, gpo):
    n, d = x2d.shape
    tm = min(TOKEN_TILE, n)
    hidden = wdown.shape[0]
    assert n % tm == 0 and hidden % FFN_CHUNK == 0 and wffn.shape[1] == 2 * hidden
    assert wgates.shape == (d, N_BRANCH * d)
    tok = lambda width: pl.BlockSpec((tm, width), lambda i: (i, 0))
    row = lambda a: a.reshape(1, d)
    return pl.pallas_call(
        _merge_ffn_kernel,
        grid=(n // tm,),
        in_specs=[tok(d), tok(ohg.shape[1]), tok(ogd.shape[1]), tok(omem.shape[1]),
                  _resident((1, d)), _resident(wgates.shape),
                  _resident(wbh.shape), _resident(wbg.shape), _resident(wbm.shape), _resident(wout.shape),
                  _resident((1, d)), _resident((1, d)),
                  _resident(wffn.shape), _resident(wdown.shape), _resident((1, d))],
        out_specs=tok(d),
        out_shape=jax.ShapeDtypeStruct((n, d), F32),
        compiler_params=_cparams("arbitrary"),
        name="merge_ffn",
    )(x2d, ohg, ogd, omem, row(gpre), wgates, wbh, wbg, wbm, wout, row(gpm), row(gpf), wffn, wdown, row(gpo))


def _pack_w_in(w_in):
    d = w_in.shape[0]
    w = HEADS * HEAD_DIM
    c_ab = 4 * w + 3 * w + w
    c_mq = c_ab + 2 * HEADS
    c_gt = c_mq + w
    ab = jnp.zeros((d, LANES), w_in.dtype).at[:, :2 * HEADS].set(w_in[:, c_ab:c_mq])
    mixers = jnp.concatenate([w_in[:, :c_ab], w_in[:, c_mq:c_gt], ab], axis=1)
    widths = (4 * w, 3 * w, w, w, LANES)
    return _bf(mixers), widths, _bf(w_in[:, c_gt:])


def _of_layer(a, l):
    return a.reshape(a.shape[1:]) if a.shape[0] == 1 else a[l]


def _layer(x, mem_k, mem_v, conv_buf, s_hg, s_gdn, layer, lb_logits, p):
    bsz, t_valid, d = x.shape
    short = t_valid <= SUBLANES
    t = SUBLANES if short else t_valid
    if t != t_valid:
        x = jnp.pad(x, ((0, 0), (0, t - t_valid), (0, 0)))
    x2d = x.reshape(bsz * t, d)
    w = HEADS * HEAD_DIM

    hgp, gqkv, gz, mq, ab = _norm_proj(x2d, p["g_pre_mix"], p["w_in"], p["w_in_widths"], "in_proj")
    as3d = lambda a: a.reshape(bsz, t, a.shape[-1])
    gqkv3 = as3d(gqkv)
    if short:
        o_hg, o_gdn, s_hg_new, s_gdn_new = _mixers_short(
            as3d(hgp), gqkv3, as3d(gz), as3d(ab), conv_buf, lb_logits, p["g_hg_out"], p["w_conv"],
            p["a_log"], p["dt_bias"], p["g_gdn_out"], s_hg, s_gdn, t_valid, layer)
        o_mem = _mem_attn_cache(as3d(mq), mem_k, mem_v, ATTN_DECODE_BLOCK)
    else:
        o_hg, s_hg_new = _hgrn_long(as3d(hgp), lb_logits, p["g_hg_out"], s_hg, layer)
        o_gdn, s_gdn_new = _gdn_long(gqkv3, as3d(gz), as3d(ab), conv_buf, p["w_conv"], p["a_log"],
                                     p["dt_bias"], p["g_gdn_out"], s_gdn)
        o_mem = _mem_attn(as3d(mq), mem_k, mem_v, 1, min(ATTN_TILE, t))
    assert t_valid >= CONV_W - 1
    conv_new = gqkv3[:, t_valid - (CONV_W - 1):t_valid, :]

    flat = lambda a: a.reshape(bsz * t, w)
    y = _merge_ffn(x2d, flat(o_hg), flat(o_gdn), flat(o_mem), p["g_pre_mix"], p["w_gates"],
                   p["w_br_hg"], p["w_br_gdn"], p["w_br_mem"], p["w_out"], p["g_post_mix"],
                   p["g_pre_ffn"], p["w_ffn_in"], p["w_ffn_out"], p["g_post_ffn"])
    y = y.reshape(bsz, t, d)[:, :t_valid]
    return y, conv_new, s_hg_new, s_gdn_new


def kernel(x_prompt, x_sample, mem_prompt, cache_mem_k, cache_mem_v, state_hgrn, state_gdn, state_gdn_conv, hg_lb_logits, g_pre_mix, w_in, w_conv, a_log, dt_bias, g_hg_out, g_gdn_out, g_mem, w_mem_kv, w_br_hg, w_br_gdn, w_br_mem, w_out, g_post_mix, g_pre_ffn, w_ffn_in, w_ffn_out, g_post_ffn):
    depth = w_in.shape[0]
    bp, _, d = x_prompt.shape
    m = mem_prompt.shape[1]
    w = HEADS * HEAD_DIM
    yp, ys = x_prompt, x_sample
    outs = [[] for _ in range(8)]
    for l in range(depth):
        w_in_packed, widths, w_gates = _pack_w_in(w_in[l])
        p = dict(g_pre_mix=g_pre_mix[l], w_in=w_in_packed, w_in_widths=widths, w_gates=w_gates,
                 w_conv=w_conv[l],
                 a_log=a_log[l], dt_bias=dt_bias[l], g_hg_out=g_hg_out[l], g_gdn_out=g_gdn_out[l],
                 w_br_hg=_bf(w_br_hg[l]), w_br_gdn=_bf(w_br_gdn[l]), w_br_mem=_bf(w_br_mem[l]),
                 w_out=_bf(w_out[l]), g_post_mix=g_post_mix[l], g_pre_ffn=g_pre_ffn[l],
                 w_ffn_in=_bf(w_ffn_in[l]), w_ffn_out=_bf(w_ffn_out[l]), g_post_ffn=g_post_ffn[l])
        mk, mv = _norm_proj(mem_prompt.reshape(bp * m, d), g_mem[l], _bf(w_mem_kv[l]), (w, w), "mem_kv")
        mk, mv = mk.reshape(bp, m, w), mv.reshape(bp, m, w)
        zeros_state = jnp.zeros((bp, HEADS, HEAD_DIM, HEAD_DIM), F32)
        yp, cb, sh, sg = _layer(yp, mk, mv, jnp.zeros((bp, CONV_W - 1, 3 * w), F32),
                                zeros_state, zeros_state, l, hg_lb_logits, p)
        bs = x_sample.shape[0]
        ys, cb2, sh2, sg2 = _layer(ys, _of_layer(cache_mem_k, l), _of_layer(cache_mem_v, l),
                                   _of_layer(state_gdn_conv, l), _of_layer(state_hgrn, l),
                                   _of_layer(state_gdn, l), l, hg_lb_logits, p)
        for lst, val in zip(outs, (mk.reshape(bp, m, HEADS, HEAD_DIM), mv.reshape(bp, m, HEADS, HEAD_DIM),
                                   sh, sg, cb, sh2, sg2, cb2)):
            lst.append(val)
    return (yp, ys) + tuple(o[0].reshape((1,) + o[0].shape) if depth == 1 else jnp.stack(o) for o in outs)
```

```python
import functools
import math

import numpy as np
import jax
import jax.numpy as jnp
from jax import lax
from jax.experimental import pallas as pl
from jax.experimental.pallas import tpu as pltpu

F32 = jnp.float32
BF16 = jnp.bfloat16

EPS = 1e-6
HEADS = 4
HEAD_DIM = 128
CONV_W = 4
N_BRANCH = 3

LANES = 128
SUBLANES = 8
VMEM_LIMIT = 56 * 1024 * 1024

SAFE_LOG_RANGE = 70.0
HG_CHUNK = 128
HG_BOUNDED_CHUNK = 64
HG_CHUNKS_PER_STEP = 4
GDN_CHUNK = 64
GDN_CHUNKS_PER_STEP = 8
TOKEN_TILE = 512
ATTN_TILE = 512
FFN_CHUNK = 256
DECODE_BLOCK = 8
DECODE_INTERLEAVE = 2
ATTN_DECODE_BLOCK = 8


def _cparams(*sem):
    return pltpu.CompilerParams(dimension_semantics=sem, vmem_limit_bytes=VMEM_LIMIT)


def _resident(shape):
    nd = len(shape)
    return pl.BlockSpec(shape, lambda *_: (0,) * nd, pipeline_mode=pl.Buffered(1))


def _bf(x):
    return x.astype(BF16)


def _dot(a, b):
    return jnp.dot(a, b, preferred_element_type=F32)


def _dot_nt(a, b):
    return lax.dot_general(a, b, (((1,), (1,)), ((), ())), preferred_element_type=F32)


def _dot_tn(a, b):
    return lax.dot_general(a, b, (((0,), (0,)), ((), ())), preferred_element_type=F32)


def _bmm(a, b):
    return jnp.einsum("nij,njk->nik", a, b, preferred_element_type=F32)


def _bmm_nt(a, b):
    return jnp.einsum("nid,njd->nij", a, b, preferred_element_type=F32)


def _bmm_tn(a, b):
    return jnp.einsum("nci,ncj->nij", a, b, preferred_element_type=F32)


def _split2(x):
    hi = _bf(x)
    lo = _bf(x - hi.astype(F32))
    return hi, lo


def _dot_exact_lhs(w_bf16, x):
    hi, lo = _split2(x)
    return _dot(w_bf16, hi) + _dot(w_bf16, lo)


def _rms(x, g):
    return x * lax.rsqrt(jnp.mean(x * x, axis=-1, keepdims=True) + EPS) * g


def _head(h):
    return slice(h * HEAD_DIM, (h + 1) * HEAD_DIM)


def _block_diag(x, nblk):
    wblk = x.shape[-1] // nblk
    if wblk % LANES == 0:
        zeros = lambda n: [jnp.zeros(x.shape[:-1] + (n * wblk,), x.dtype)] if n else []
        parts = [jnp.concatenate(zeros(j) + [x[..., j * wblk:(j + 1) * wblk]] + zeros(nblk - 1 - j), axis=-1)
                 for j in range(nblk)]
    else:
        lane = lax.broadcasted_iota(jnp.int32, x.shape, x.ndim - 1)
        parts = [jnp.where((lane >= j * wblk) & (lane < (j + 1) * wblk), x, jnp.zeros_like(x))
                 for j in range(nblk)]
    return jnp.concatenate(parts, axis=-2)


def _norm_proj_kernel(x_ref, g_ref, w_ref, *out_refs):
    xn = _bf(_rms(x_ref[...], g_ref[...]))
    off = 0
    for o_ref in out_refs:
        width = o_ref.shape[-1]
        for c0 in range(0, width, 512):
            cw = min(512, width - c0)
            o_ref[:, c0:c0 + cw] = _dot(xn, w_ref[:, off + c0:off + c0 + cw])
        off += width


def _norm_proj(x2d, g, w_bf16, widths, name):
    n, d = x2d.shape
    tm = min(TOKEN_TILE, n)
    assert n % tm == 0 and sum(widths) == w_bf16.shape[1]
    return pl.pallas_call(
        _norm_proj_kernel,
        grid=(n // tm,),
        in_specs=[pl.BlockSpec((tm, d), lambda i: (i, 0)),
                  _resident((1, d)),
                  _resident(w_bf16.shape)],
        out_specs=[pl.BlockSpec((tm, w), lambda i: (i, 0)) for w in widths],
        out_shape=[jax.ShapeDtypeStruct((n, w), F32) for w in widths],
        compiler_params=_cparams("arbitrary"),
        name=name,
    )(x2d, g.reshape(1, d), w_bf16)


def _forget_lower_bound(logits, layer):
    m = jnp.max(logits, axis=0, keepdims=True)
    e = jnp.exp(logits - m)
    return jnp.sum(e[:layer + 1], axis=0, keepdims=True) / jnp.sum(e, axis=0, keepdims=True)


def _col_to_matrix(row):
    n = row.shape[-1]
    return jnp.broadcast_to(row, (n, n)).T


def _gdn_inputs(ext_ref, c, wconv, ab, alog, dtb, t_valid):
    if c > SUBLANES:
        full = ext_ref[...]
        acc = full * wconv[CONV_W - 1:CONV_W, :]
        for j in range(CONV_W - 1):
            acc = acc + pltpu.roll(full, CONV_W - 1 - j, axis=0) * wconv[j:j + 1, :]
        acc = acc[SUBLANES:SUBLANES + c]
    else:
        acc = ext_ref[pl.ds(SUBLANES - 3, c), :] * wconv[0:1, :]
        for j in range(1, CONV_W):
            acc = acc + ext_ref[pl.ds(SUBLANES - 3 + j, c), :] * wconv[j:j + 1, :]
    qkv = jax.nn.silu(acc)
    w = HEADS * HEAD_DIM
    qs, ks = [], []
    for h in range(HEADS):
        qh = qkv[:, h * HEAD_DIM:(h + 1) * HEAD_DIM]
        kh = qkv[:, w + h * HEAD_DIM:w + (h + 1) * HEAD_DIM]
        qs.append(qh * lax.rsqrt(jnp.sum(qh * qh, axis=-1, keepdims=True) + EPS) * (HEAD_DIM ** -0.5))
        ks.append(kh * lax.rsqrt(jnp.sum(kh * kh, axis=-1, keepdims=True) + EPS))
    v = qkv[:, 2 * w:3 * w]
    glog = -jnp.exp(alog) * jax.nn.softplus(ab + dtb)
    beta = jax.nn.sigmoid(ab)
    if t_valid is not None:
        valid = lax.broadcasted_iota(jnp.int32, ab.shape, 0) < t_valid
        glog = jnp.where(valid, glog, 0.0)
        beta = jnp.where(valid, beta, 0.0)
    return qs, ks, v, glog, beta


def _hier_tables(c):
    t = np.arange(c)
    u = np.arange(c)
    tri = (u[None, :] <= t[:, None]).astype(np.float32)
    rows, masks = [], []
    b = c // 2
    while b >= 1:
        ref = (t // (2 * b)) * 2 * b + b - 1
        rows.append(tri - (u[None, :] <= ref[:, None]).astype(np.float32))
        right = (t % (2 * b)) >= b
        left = (t % (2 * b)) < b
        same = (t[:, None] // (2 * b)) == (t[None, :] // (2 * b))
        masks.append((right[:, None] & left[None, :] & same).astype(np.float32))
        b //= 2
    rows.append(tri)
    return np.concatenate(rows, 0), np.stack(masks, 0)


def _hgrn_chunk(q, k, v, logf_hi, logf_lo, s, wall_ref, masks, c):
    nlev = len(masks)
    d = _dot(wall_ref[...], jnp.concatenate([logf_hi, logf_lo], axis=0))
    g = d[nlev * c:(nlev + 1) * c]
    gc = g[c - 1:c, :]
    qg = _bf(q * jnp.exp(g))
    kg = _bf(k * jnp.exp(gc - g))
    egc = jnp.exp(gc)
    vb = _bf(v)
    npair = HEADS // 2
    a = [None] * npair
    for l in range(nlev):
        e = jnp.exp(-jnp.abs(d[l * c:(l + 1) * c]))
        ql, kl = _bf(q * e), _bf(k * e)
        for pr in range(npair):
            pair = slice(2 * pr * HEAD_DIM, (2 * pr + 2) * HEAD_DIM)
            p = _dot_nt(ql[:, pair], _block_diag(kl[:, pair], 2))
            a[pr] = jnp.where(masks[l], p, 0.0) if a[pr] is None else jnp.where(masks[l], p, a[pr])
    qk = q * k
    outs, s_new = [], []
    for h in range(HEADS):
        sl = _head(h)
        a_h = a[h // 2][:, (h % 2) * c:(h % 2 + 1) * c]
        o = _dot(jnp.concatenate([qg[:, sl], _bf(a_h)], axis=1),
                 jnp.concatenate([_bf(s[h]), vb[:, sl]], axis=0))
        outs.append(o + jnp.sum(qk[:, sl], axis=-1, keepdims=True) * v[:, sl])
        s_new.append(_col_to_matrix(egc[:, sl]) * s[h] + _dot_tn(kg[:, sl], vb[:, sl]))
    return outs, s_new


def _hgrn_chunk_bounded(q, k, v, g, s, c):
    gc = g[c - 1:c, :]
    qg = _bf(q * jnp.exp(g))
    kn = _bf(k * jnp.exp(-g))
    kg = _bf(k * jnp.exp(gc - g))
    egc = jnp.exp(gc)
    vb = _bf(v)
    ti = lax.broadcasted_iota(jnp.int32, (c, HEADS * c), 0)
    si = lax.broadcasted_iota(jnp.int32, (c, HEADS * c), 1) & (c - 1)
    a = jnp.where(ti > si, _dot_nt(qg, _block_diag(kn, HEADS)), 0.0)
    v_rows = jnp.concatenate([vb[:, _head(h)] for h in range(HEADS)], axis=0)
    intra = _dot(_block_diag(_bf(a), HEADS), v_rows)
    qk = q * k
    outs, s_new = [], []
    for h in range(HEADS):
        sl = _head(h)
        outs.append(_dot(qg[:, sl], _bf(s[h])) + intra[h * c:(h + 1) * c]
                    + jnp.sum(qk[:, sl], axis=-1, keepdims=True) * v[:, sl])
        s_new.append(_col_to_matrix(egc[:, sl]) * s[h] + _dot_tn(kg[:, sl], vb[:, sl]))
    return outs, s_new


def _hgrn_kernel(q_ref, f_ref, i_ref, gate_ref, lbl_ref, gout_ref, wall_ref, mask_ref, tri_ref, s0_ref,
                 o_ref, sn_ref, s_scr, *, c, nch, layer):
    n = pl.program_id(1)

    @pl.when(n == 0)
    def _():
        s_scr[...] = s0_ref[0]

    lb = _forget_lower_bound(lbl_ref[...], layer)
    gout = gout_ref[...]
    f = lb + (1.0 - lb) * jax.nn.sigmoid(f_ref[0])
    hi, lo = _split2(jnp.log(f))
    k = 1.0 - f
    nlev = mask_ref.shape[0]

    def emit(r, outs):
        for h in range(HEADS):
            sl = _head(h)
            o_ref[0, r, sl] = _rms(outs[h], gout[:, sl]) * jax.nn.silu(gate_ref[0, r, sl])

    def run(chunk_len, chunk_fn):
        s = [s_scr[h] for h in range(HEADS)]
        for ch in range(nch * c // chunk_len):
            r = slice(ch * chunk_len, (ch + 1) * chunk_len)
            outs, s = chunk_fn(ch, r, s)
            emit(r, outs)
        for h in range(HEADS):
            s_scr[h] = s[h]

    cb = tri_ref.shape[0]
    half = lambda ch: slice(ch * cb, (ch + 1) * cb)
    gs = [_dot(tri_ref[...], jnp.concatenate([hi[half(ch)], lo[half(ch)]], axis=0))
          for ch in range(nch * c // cb)]
    gmin = functools.reduce(jnp.minimum, [jnp.min(g) for g in gs])
    bounded = gmin >= -SAFE_LOG_RANGE

    @pl.when(bounded)
    def _():
        run(cb, lambda ch, r, s: _hgrn_chunk_bounded(q_ref[0, r, :], k[r], i_ref[0, r, :], gs[ch], s, cb))

    @pl.when(jnp.logical_not(bounded))
    def _():
        masks = [mask_ref[l] != 0.0 for l in range(nlev)]
        run(c, lambda ch, r, s: _hgrn_chunk(q_ref[0, r, :], k[r], i_ref[0, r, :], hi[r], lo[r], s,
                                            wall_ref, masks, c))

    @pl.when(n == pl.num_programs(1) - 1)
    def _():
        sn_ref[0] = s_scr[...]


def _hgrn_long(hgp, lb_logits, g_out, s0, layer):
    bsz, t, _ = hgp.shape
    c = HG_CHUNK
    nch = HG_CHUNKS_PER_STEP
    tb = c * nch
    assert t % tb == 0 and c == HEAD_DIM and HEADS % 2 == 0
    w = HEADS * HEAD_DIM
    wall, masks = _hier_tables(c)
    wall = np.concatenate([wall, wall], axis=1)
    masks = np.concatenate([masks, masks], axis=2)
    cb = HG_BOUNDED_CHUNK
    tri = np.tril(np.ones((cb, cb), np.float32))
    tri = np.concatenate([tri, tri], axis=1)
    col = lambda j: pl.BlockSpec((1, tb, w), lambda b, n, j=j: (b, n, j))
    st = pl.BlockSpec((1, HEADS, HEAD_DIM, HEAD_DIM), lambda b, n: (b, 0, 0, 0))
    return pl.pallas_call(
        functools.partial(_hgrn_kernel, c=c, nch=nch, layer=layer),
        grid=(bsz, t // tb),
        in_specs=[col(0), col(1), col(2), col(3),
                  _resident(lb_logits.shape), _resident((1, w)),
                  _resident(wall.shape), _resident(masks.shape), _resident(tri.shape), st],
        out_specs=[pl.BlockSpec((1, tb, w), lambda b, n: (b, n, 0)), st],
        out_shape=[jax.ShapeDtypeStruct((bsz, t, w), F32),
                   jax.ShapeDtypeStruct(s0.shape, F32)],
        scratch_shapes=[pltpu.VMEM((HEADS, HEAD_DIM, HEAD_DIM), F32)],
        compiler_params=_cparams("arbitrary", "arbitrary"),
        name="hgrn_long",
    )(hgp, hgp, hgp, hgp, lb_logits, g_out.reshape(1, w),
      jnp.asarray(wall, BF16), jnp.asarray(masks, F32), jnp.asarray(tri, BF16), s0)


def _gdn_prepare(qs, ks, v, glog, beta, tri_bf, c, nch):
    nb = nch * HEADS
    wc = HEADS * c
    rows = lambda x, ch: x[ch * c:(ch + 1) * c]
    per = lambda f: jnp.stack([f(ch, h) for ch in range(nch) for h in range(HEADS)], axis=0)
    q = per(lambda ch, h: rows(qs[h], ch))
    k = per(lambda ch, h: rows(ks[h], ch))
    vv = per(lambda ch, h: rows(v, ch)[:, _head(h)])
    beta_b = per(lambda ch, h: jnp.broadcast_to(rows(beta, ch)[:, HEADS + h:HEADS + h + 1], (c, LANES)))
    lane = lax.broadcasted_iota(jnp.int32, (c, LANES), 1)
    g_small = None
    for ch in range(nch):
        part = jnp.where(lane < HEADS, rows(glog, ch), 0.0)
        part = part if ch == 0 else pltpu.roll(part, ch * HEADS, axis=1)
        g_small = part if g_small is None else g_small + part
    gcum = _dot_exact_lhs(tri_bf, g_small)
    gcum_t = jnp.concatenate([gcum, jnp.zeros((LANES - c, LANES), F32)], axis=0).T
    g = jnp.stack([jnp.broadcast_to(gcum[:, n:n + 1], (c, LANES)) for n in range(nb)], axis=0)
    packed = lambda f: jnp.stack([jnp.concatenate([f(ch * HEADS + h) for h in range(HEADS)], axis=1)
                                  for ch in range(nch)], axis=0)
    g_col = packed(lambda n: jnp.broadcast_to(gcum[:, n:n + 1], (c, c)))
    g_row = packed(lambda n: jnp.broadcast_to(gcum_t[n:n + 1, :c], (c, c)))
    ti = lax.broadcasted_iota(jnp.int32, (nch, c, wc), 1)
    si = lax.broadcasted_iota(jnp.int32, (nch, c, wc), 2) & (c - 1)
    decay = jnp.exp(jnp.minimum(g_col - g_row, 0.0))
    kb = k * beta_b
    heads_on_lanes = lambda x: jnp.stack(
        [jnp.concatenate([x[ch * HEADS + h] for h in range(HEADS)], axis=1) for ch in range(nch)], axis=0)
    kbq = _bf(jnp.concatenate([heads_on_lanes(kb), heads_on_lanes(q)], axis=1))
    kq = _bmm_nt(kbq, _block_diag(_bf(heads_on_lanes(k)), HEADS))
    m = jnp.where(ti > si, kq[:, :c] * decay, 0.0)
    aqk = jnp.where(ti > si, kq[:, c:] * decay, 0.0)
    qk_diag = jnp.sum(q * k, axis=-1, keepdims=True)
    eg = jnp.exp(g)
    rhs = jnp.concatenate([vv * beta_b, kb * eg], axis=-1)
    npow = -m
    poff = npow
    npow_bd = _block_diag(_bf(npow), HEADS)
    for _ in range(int(math.log2(c)) - 1):
        npow = _bmm(_bf(npow), npow_bd)
        npow_bd = _block_diag(_bf(npow), HEADS)
        poff = poff + npow + _bmm(_bf(poff), npow_bd)
    xoff = _bmm(_block_diag(_bf(poff), HEADS), _bf(rhs.reshape(nch, wc, 2 * HEAD_DIM)))
    x = rhs + xoff.reshape(nb, c, 2 * HEAD_DIM)
    u, w = x[:, :, :HEAD_DIM], x[:, :, HEAD_DIM:]
    gc = g[:, c - 1:c, :]
    return (u, _bf(jnp.concatenate([w, q * eg], axis=1)), _block_diag(_bf(aqk), HEADS), qk_diag,
            _bf(k * jnp.exp(gc - g)), jnp.exp(gc))


def _gdn_kernel(qkv_ref, z_ref, ab_ref, buf_ref, wconv_ref, alog_ref, dtb_ref, gout_ref, tri_ref,
                s0_ref, o_ref, sn_ref, s_scr, ext_scr, *, c, nch):
    n = pl.program_id(1)
    tb = c * nch

    @pl.when(n == 0)
    def _():
        s_scr[...] = s0_ref[0]
        ext_scr[0:SUBLANES - 3, :] = jnp.zeros((SUBLANES - 3, ext_scr.shape[1]), F32)
        ext_scr[SUBLANES - 3:SUBLANES, :] = buf_ref[0]

    @pl.when(n > 0)
    def _():
        ext_scr[0:SUBLANES, :] = ext_scr[tb:tb + SUBLANES, :]

    ext_scr[SUBLANES:SUBLANES + tb, :] = qkv_ref[0]
    qs, ks, v, glog, beta = _gdn_inputs(ext_scr, tb, wconv_ref[...], ab_ref[0], alog_ref[...],
                                        dtb_ref[...], None)
    u, wq, aqk_bd, qk_diag, kg, egc = _gdn_prepare(qs, ks, v, glog, beta, tri_ref[...], c, nch)
    gout = gout_ref[...]
    s = s_scr[...]
    for ch in range(nch):
        sel = slice(ch * HEADS, (ch + 1) * HEADS)
        ws_qs = _bmm(wq[sel], _bf(s))
        v_new = u[sel] - ws_qs[:, :c]
        intra = _dot(aqk_bd[ch], _bf(v_new.reshape(HEADS * c, HEAD_DIM)))
        o = ws_qs[:, c:] + intra.reshape(HEADS, c, HEAD_DIM) + qk_diag[sel] * v_new
        s = egc[sel] * s + _bmm_tn(kg[sel], _bf(v_new))
        for h in range(HEADS):
            sl = _head(h)
            o_ref[0, ch * c:(ch + 1) * c, sl] = _rms(o[h], gout) * jax.nn.silu(z_ref[0, ch * c:(ch + 1) * c, sl])
    s_scr[...] = s

    @pl.when(n == pl.num_programs(1) - 1)
    def _():
        sn_ref[0] = s


def _pad_lanes(x):
    return jnp.zeros((1, LANES), F32).at[0, :x.shape[0]].set(x.astype(F32))


def _gdn_long(gqkv, gz, ab, conv_buf, w_conv, a_log, dt_bias, g_out, s0):
    bsz, t, wq = gqkv.shape
    c = GDN_CHUNK
    nch = GDN_CHUNKS_PER_STEP
    tb = c * nch
    assert t % tb == 0
    w = HEADS * HEAD_DIM
    tri = np.tril(np.ones((c, c), np.float32))
    st = pl.BlockSpec((1, HEADS, HEAD_DIM, HEAD_DIM), lambda b, n: (b, 0, 0, 0))
    return pl.pallas_call(
        functools.partial(_gdn_kernel, c=c, nch=nch),
        grid=(bsz, t // tb),
        in_specs=[pl.BlockSpec((1, tb, wq), lambda b, n: (b, n, 0)),
                  pl.BlockSpec((1, tb, w), lambda b, n: (b, n, 0)),
                  pl.BlockSpec((1, tb, LANES), lambda b, n: (b, n, 0)),
                  pl.BlockSpec((1, CONV_W - 1, wq), lambda b, n: (b, 0, 0)),
                  _resident((CONV_W, wq)), _resident((1, LANES)), _resident((1, LANES)),
                  _resident((1, HEAD_DIM)), _resident((c, c)), st],
        out_specs=[pl.BlockSpec((1, tb, w), lambda b, n: (b, n, 0)), st],
        out_shape=[jax.ShapeDtypeStruct((bsz, t, w), F32),
                   jax.ShapeDtypeStruct((bsz, HEADS, HEAD_DIM, HEAD_DIM), F32)],
        scratch_shapes=[pltpu.VMEM((HEADS, HEAD_DIM, HEAD_DIM), F32),
                        pltpu.VMEM((tb + SUBLANES, wq), F32)],
        compiler_params=_cparams("arbitrary", "arbitrary"),
        name="gdn_long",
    )(gqkv, gz, ab, conv_buf, w_conv,
      _pad_lanes(a_log), _pad_lanes(dt_bias),
      g_out.reshape(1, HEAD_DIM), jnp.asarray(tri, BF16), s0)


def _shift_rows(x, d):
    return x if d == 0 else pltpu.roll(x, d, axis=0)


def _cumsum_rows8(x, rows):
    for d in (1, 2, 4):
        x = x + jnp.where(rows >= d, pltpu.roll(x, d, axis=0), 0.0)
    return x


def _hgrn_short(q, hf, v, lb, s, rows, t_valid):
    f = lb + (1.0 - lb) * jax.nn.sigmoid(hf)
    valid = rows < t_valid
    logf = jnp.where(valid, jnp.log(f), 0.0)
    k = jnp.where(valid, 1.0 - f, 0.0)
    g = _cumsum_rows8(logf, rows)
    gc = g[SUBLANES - 1:SUBLANES, :]
    o = _dot(_bf(q * jnp.exp(g)), _bf(s))
    for d in range(t_valid):
        ok = rows >= d
        dec = jnp.exp(jnp.where(ok, g - _shift_rows(g, d), 0.0))
        a = jnp.sum(jnp.where(ok, q * _shift_rows(k, d) * dec, 0.0), axis=-1, keepdims=True)
        o = o + a * _shift_rows(v, d)
    s_new = _col_to_matrix(jnp.exp(gc)) * s + _dot_tn(_bf(k * jnp.exp(gc - g)), _bf(v))
    return o, s_new


def _gdn_short(q, k, v, g_b, beta_b, s, rows, t_valid):
    g = _cumsum_rows8(g_b, rows)
    gc = g[SUBLANES - 1:SUBLANES, :]
    eg = jnp.exp(g)
    kb = k * beta_b
    dec = [None] + [jnp.exp(jnp.where(rows >= d, g - _shift_rows(g, d), 0.0)) for d in range(1, t_valid)]
    mcol = [None] + [jnp.where(rows >= d, jnp.sum(kb * _shift_rows(k, d), axis=-1, keepdims=True) * dec[d], 0.0)
                     for d in range(1, t_valid)]
    ru, rw = v * beta_b, kb * eg
    xu, xw = ru, rw
    for _ in range(t_valid - 1):
        nu, nw = ru, rw
        for d in range(1, t_valid):
            nu = nu - mcol[d] * _shift_rows(xu, d)
            nw = nw - mcol[d] * _shift_rows(xw, d)
        xu, xw = nu, nw
    ws_qs = _dot(_bf(jnp.concatenate([xw, q * eg], axis=0)), _bf(s))
    v_new = xu - ws_qs[:SUBLANES]
    o = ws_qs[SUBLANES:] + jnp.sum(q * k, axis=-1, keepdims=True) * v_new
    for d in range(1, t_valid):
        a = jnp.where(rows >= d, jnp.sum(q * _shift_rows(k, d), axis=-1, keepdims=True) * dec[d], 0.0)
        o = o + a * _shift_rows(v_new, d)
    s_new = jnp.exp(gc) * s + _dot_tn(_bf(k * jnp.exp(gc - g)), _bf(v_new))
    return o, s_new


def _short_kernel(hq_ref, hf_ref, hi_ref, hgate_ref, qkv_ref, z_ref, ab_ref, buf_ref,
                  lbl_ref, hgout_ref, wconv_ref, alog_ref, dtb_ref, ggout_ref, shg0_ref, sgd0_ref,
                  ohg_ref, ogd_ref, shg_ref, sgd_ref, ext_scr, *, t_valid, layer):
    bb = hq_ref.shape[0]
    rows = lax.broadcasted_iota(jnp.int32, (SUBLANES, LANES), 0)
    lb = _forget_lower_bound(lbl_ref[...], layer)
    hgout = hgout_ref[...]
    ggout = ggout_ref[...]
    wconv = wconv_ref[...]
    alog = alog_ref[...]
    dtb = dtb_ref[...]

    def per_seq(i, ext):
        ext[SUBLANES - 3:SUBLANES, :] = buf_ref[i]
        ext[SUBLANES:2 * SUBLANES, :] = qkv_ref[i]
        qs, ks, v, glog, beta = _gdn_inputs(ext, SUBLANES, wconv, ab_ref[i], alog, dtb, t_valid)
        for h in range(HEADS):
            sl = _head(h)
            o, s_new = _hgrn_short(hq_ref[i, :, sl], hf_ref[i, :, sl], hi_ref[i, :, sl], lb[:, sl],
                                   shg0_ref[i, h], rows, t_valid)
            shg_ref[i, h] = s_new
            ohg_ref[i, :, sl] = _rms(o, hgout[:, sl]) * jax.nn.silu(hgate_ref[i, :, sl])
            g_b = jnp.broadcast_to(glog[:, h:h + 1], (SUBLANES, LANES))
            beta_b = jnp.broadcast_to(beta[:, HEADS + h:HEADS + h + 1], (SUBLANES, LANES))
            o, s_new = _gdn_short(qs[h], ks[h], v[:, sl], g_b, beta_b, sgd0_ref[i, h], rows, t_valid)
            sgd_ref[i, h] = s_new
            ogd_ref[i, :, sl] = _rms(o, ggout) * jax.nn.silu(z_ref[i, :, sl])

    def per_group(j, carry):
        for u in range(DECODE_INTERLEAVE):
            per_seq(j * DECODE_INTERLEAVE + u, ext_scr.at[u])
        return carry

    lax.fori_loop(0, bb // DECODE_INTERLEAVE, per_group, 0)


def _mixers_short(hgp, gqkv, gz, ab, conv_buf, lb_logits, hg_out, w_conv, a_log, dt_bias, gdn_out,
                  s_hg0, s_gdn0, t_valid, layer):
    bsz, t, wq = gqkv.shape
    assert t == SUBLANES and CONV_W - 1 <= t_valid <= SUBLANES
    bb = DECODE_BLOCK
    assert bsz % bb == 0 and bb % DECODE_INTERLEAVE == 0
    w = HEADS * HEAD_DIM
    col = lambda j: pl.BlockSpec((bb, t, w), lambda b, j=j: (b, 0, j))
    tok = lambda width: pl.BlockSpec((bb, t, width), lambda b: (b, 0, 0))
    st = pl.BlockSpec((bb, HEADS, HEAD_DIM, HEAD_DIM), lambda b: (b, 0, 0, 0))
    return pl.pallas_call(
        functools.partial(_short_kernel, t_valid=t_valid, layer=layer),
        grid=(bsz // bb,),
        in_specs=[col(0), col(1), col(2), col(3), tok(wq), tok(w), tok(LANES),
                  pl.BlockSpec((bb, CONV_W - 1, wq), lambda b: (b, 0, 0)),
                  _resident(lb_logits.shape), _resident((1, w)), _resident((CONV_W, wq)),
                  _resident((1, LANES)), _resident((1, LANES)), _resident((1, HEAD_DIM)), st, st],
        out_specs=[tok(w), tok(w), st, st],
        out_shape=[jax.ShapeDtypeStruct((bsz, t, w), F32), jax.ShapeDtypeStruct((bsz, t, w), F32),
                   jax.ShapeDtypeStruct(s_hg0.shape, F32), jax.ShapeDtypeStruct(s_gdn0.shape, F32)],
        scratch_shapes=[pltpu.VMEM((DECODE_INTERLEAVE, 2 * SUBLANES, wq), F32)],
        compiler_params=_cparams("arbitrary"),
        name="mixers_short",
    )(hgp, hgp, hgp, hgp, gqkv, gz, ab, conv_buf, lb_logits, hg_out.reshape(1, w), w_conv,
      _pad_lanes(a_log), _pad_lanes(dt_bias), gdn_out.reshape(1, HEAD_DIM), s_hg0, s_gdn0)


def _mem_attn_kernel(q_ref, k_ref, v_ref, o_ref):
    for h in range(HEADS):
        sl = _head(h)
        s = _bmm_nt(_bf(q_ref[:, :, sl]), _bf(k_ref[:, :, sl])) * (HEAD_DIM ** -0.5)
        e = jnp.exp(s - jnp.max(s, axis=-1, keepdims=True))
        o_ref[:, :, sl] = _bmm(_bf(e), _bf(v_ref[:, :, sl])) / jnp.sum(e, axis=-1, keepdims=True)


def _mem_attn(mq, mem_k, mem_v, bb, tq):
    bsz, t, w = mq.shape
    m = mem_k.shape[1]
    assert bsz % bb == 0 and t % tq == 0
    kv = pl.BlockSpec((bb, m, w), lambda b, n: (b, 0, 0))
    return pl.pallas_call(
        _mem_attn_kernel,
        grid=(bsz // bb, t // tq),
        in_specs=[pl.BlockSpec((bb, tq, w), lambda b, n: (b, n, 0)), kv, kv],
        out_specs=pl.BlockSpec((bb, tq, w), lambda b, n: (b, n, 0)),
        out_shape=jax.ShapeDtypeStruct((bsz, t, w), F32),
        compiler_params=_cparams("arbitrary", "arbitrary"),
        name="mem_attn",
    )(mq, mem_k, mem_v)


def _mem_attn_cache_kernel(q_ref, k_hbm, v_hbm, o_ref, kbuf, vbuf, sem):
    i = pl.program_id(0)
    bb = q_ref.shape[0]

    def copies(step, slot):
        out = []
        for h in range(HEADS):
            for j, (src, dst) in enumerate(((k_hbm, kbuf), (v_hbm, vbuf))):
                out.append(pltpu.make_async_copy(src.at[pl.ds(step * bb, bb), :, h, :],
                                                 dst.at[slot, h], sem.at[j, slot, h]))
        return out

    slot = i % 2

    @pl.when(i == 0)
    def _():
        for cp in copies(0, 0):
            cp.start()

    @pl.when(i + 1 < pl.num_programs(0))
    def _():
        for cp in copies(i + 1, 1 - slot):
            cp.start()

    for cp in copies(i, slot):
        cp.wait()
    for h in range(HEADS):
        sl = _head(h)
        s = _bmm_nt(_bf(q_ref[:, :, sl]), _bf(kbuf[slot, h])) * (HEAD_DIM ** -0.5)
        e = jnp.exp(s - jnp.max(s, axis=-1, keepdims=True))
        o_ref[:, :, sl] = _bmm(_bf(e), _bf(vbuf[slot, h])) / jnp.sum(e, axis=-1, keepdims=True)


def _mem_attn_cache(mq, cache_k, cache_v, bb):
    bsz, t, w = mq.shape
    m = cache_k.shape[1]
    assert bsz % bb == 0 and cache_k.shape == (bsz, m, HEADS, HEAD_DIM)
    return pl.pallas_call(
        _mem_attn_cache_kernel,
        grid=(bsz // bb,),
        in_specs=[pl.BlockSpec((bb, t, w), lambda b: (b, 0, 0)),
                  pl.BlockSpec(memory_space=pl.ANY), pl.BlockSpec(memory_space=pl.ANY)],
        out_specs=pl.BlockSpec((bb, t, w), lambda b: (b, 0, 0)),
        out_shape=jax.ShapeDtypeStruct((bsz, t, w), F32),
        scratch_shapes=[pltpu.VMEM((2, HEADS, bb, m, HEAD_DIM), F32),
                        pltpu.VMEM((2, HEADS, bb, m, HEAD_DIM), F32),
                        pltpu.SemaphoreType.DMA((2, 2, HEADS))],
        compiler_params=_cparams("arbitrary"),
        name="mem_attn_cache",
    )(mq, cache_k, cache_v)


def _merge_ffn_kernel(x_ref, ohg_ref, ogd_ref, omem_ref, gpre_ref, wgates_ref,
                      wbh_ref, wbg_ref, wbm_ref, wout_ref, gpm_ref, gpf_ref,
                      wffn_ref, wdown_ref, gpo_ref, y_ref):
    d = x_ref.shape[-1]
    x = x_ref[...]
    xn = _bf(_rms(x, gpre_ref[...]))
    merged = None
    for j, (o_ref, wb_ref) in enumerate(((ohg_ref, wbh_ref), (ogd_ref, wbg_ref), (omem_ref, wbm_ref))):
        gate = jax.nn.sigmoid(_dot(xn, wgates_ref[:, j * d:(j + 1) * d]))
        term = gate * _dot(_bf(o_ref[...]), wb_ref[...])
        merged = term if merged is None else merged + term
    h = x + _rms(_dot(_bf(merged), wout_ref[...]), gpm_ref[...])
    hn = _bf(_rms(h, gpf_ref[...]))
    hidden = wdown_ref.shape[0]
    ff = None
    for c0 in range(0, hidden, FFN_CHUNK):
        a = (jax.nn.silu(_dot(hn, wffn_ref[:, c0:c0 + FFN_CHUNK]))
             * _dot(hn, wffn_ref[:, hidden + c0:hidden + c0 + FFN_CHUNK]))
        p = _dot(_bf(a), wdown_ref[c0:c0 + FFN_CHUNK, :])
        ff = p if ff is None else ff + p
    y_ref[...] = h + _rms(ff, gpo_ref[...])


def _merge_ffn(x2d, ohg, ogd, omem, gpre, wgates, wbh, wbg, wbm, wout, gpm, gpf, wffn, wdown, gpo):
    n, d = x2d.shape
    tm = min(TOKEN_TILE, n)
    hidden = wdown.shape[0]
    assert n % tm == 0 and hidden % FFN_CHUNK == 0 and wffn.shape[1] == 2 * hidden
    assert wgates.shape == (d, N_BRANCH * d)
    tok = lambda width: pl.BlockSpec((tm, width), lambda i: (i, 0))
    row = lambda a: a.reshape(1, d)
    return pl.pallas_call(
        _merge_ffn_kernel,
        grid=(n // tm,),
        in_specs=[tok(d), tok(ohg.shape[1]), tok(ogd.shape[1]), tok(omem.shape[1]),
                  _resident((1, d)), _resident(wgates.shape),
                  _resident(wbh.shape), _resident(wbg.shape), _resident(wbm.shape), _resident(wout.shape),
                  _resident((1, d)), _resident((1, d)),
                  _resident(wffn.shape), _resident(wdown.shape), _resident((1, d))],
        out_specs=tok(d),
        out_shape=jax.ShapeDtypeStruct((n, d), F32),
        compiler_params=_cparams("arbitrary"),
        name="merge_ffn",
    )(x2d, ohg, ogd, omem, row(gpre), wgates, wbh, wbg, wbm, wout, row(gpm), row(gpf), wffn, wdown, row(gpo))


def _pack_w_in(w_in):
    d = w_in.shape[0]
    w = HEADS * HEAD_DIM
    c_ab = 4 * w + 3 * w + w
    c_mq = c_ab + 2 * HEADS
    c_gt = c_mq + w
    ab = jnp.zeros((d, LANES), w_in.dtype).at[:, :2 * HEADS].set(w_in[:, c_ab:c_mq])
    mixers = jnp.concatenate([w_in[:, :c_ab], w_in[:, c_mq:c_gt], ab], axis=1)
    widths = (4 * w, 3 * w, w, w, LANES)
    return _bf(mixers), widths, _bf(w_in[:, c_gt:])


def _of_layer(a, l):
    return a.reshape(a.shape[1:]) if a.shape[0] == 1 else a[l]


def _layer(x, mem_k, mem_v, conv_buf, s_hg, s_gdn, layer, lb_logits, p):
    bsz, t_valid, d = x.shape
    short = t_valid <= SUBLANES
    t = SUBLANES if short else t_valid
    if t != t_valid:
        x = jnp.pad(x, ((0, 0), (0, t - t_valid), (0, 0)))
    x2d = x.reshape(bsz * t, d)
    w = HEADS * HEAD_DIM

    hgp, gqkv, gz, mq, ab = _norm_proj(x2d, p["g_pre_mix"], p["w_in"], p["w_in_widths"], "in_proj")
    as3d = lambda a: a.reshape(bsz, t, a.shape[-1])
    gqkv3 = as3d(gqkv)
    if short:
        o_hg, o_gdn, s_hg_new, s_gdn_new = _mixers_short(
            as3d(hgp), gqkv3, as3d(gz), as3d(ab), conv_buf, lb_logits, p["g_hg_out"], p["w_conv"],
            p["a_log"], p["dt_bias"], p["g_gdn_out"], s_hg, s_gdn, t_valid, layer)
        o_mem = _mem_attn_cache(as3d(mq), mem_k, mem_v, ATTN_DECODE_BLOCK)
    else:
        o_hg, s_hg_new = _hgrn_long(as3d(hgp), lb_logits, p["g_hg_out"], s_hg, layer)
        o_gdn, s_gdn_new = _gdn_long(gqkv3, as3d(gz), as3d(ab), conv_buf, p["w_conv"], p["a_log"],
                                     p["dt_bias"], p["g_gdn_out"], s_gdn)
        o_mem = _mem_attn(as3d(mq), mem_k, mem_v, 1, min(ATTN_TILE, t))
    assert t_valid >= CONV_W - 1
    conv_new = gqkv3[:, t_valid - (CONV_W - 1):t_valid, :]

    flat = lambda a: a.reshape(bsz * t, w)
    y = _merge_ffn(x2d, flat(o_hg), flat(o_gdn), flat(o_mem), p["g_pre_mix"], p["w_gates"],
                   p["w_br_hg"], p["w_br_gdn"], p["w_br_mem"], p["w_out"], p["g_post_mix"],
                   p["g_pre_ffn"], p["w_ffn_in"], p["w_ffn_out"], p["g_post_ffn"])
    y = y.reshape(bsz, t, d)[:, :t_valid]
    return y, conv_new, s_hg_new, s_gdn_new


def kernel(x_prompt, x_sample, mem_prompt, cache_mem_k, cache_mem_v, state_hgrn, state_gdn, state_gdn_conv, hg_lb_logits, g_pre_mix, w_in, w_conv, a_log, dt_bias, g_hg_out, g_gdn_out, g_mem, w_mem_kv, w_br_hg, w_br_gdn, w_br_mem, w_out, g_post_mix, g_pre_ffn, w_ffn_in, w_ffn_out, g_post_ffn):
    depth = w_in.shape[0]
    bp, _, d = x_prompt.shape
    m = mem_prompt.shape[1]
    w = HEADS * HEAD_DIM
    yp, ys = x_prompt, x_sample
    outs = [[] for _ in range(8)]
    for l in range(depth):
        w_in_packed, widths, w_gates = _pack_w_in(w_in[l])
        p = dict(g_pre_mix=g_pre_mix[l], w_in=w_in_packed, w_in_widths=widths, w_gates=w_gates,
                 w_conv=w_conv[l],
                 a_log=a_log[l], dt_bias=dt_bias[l], g_hg_out=g_hg_out[l], g_gdn_out=g_gdn_out[l],
                 w_br_hg=_bf(w_br_hg[l]), w_br_gdn=_bf(w_br_gdn[l]), w_br_mem=_bf(w_br_mem[l]),
                 w_out=_bf(w_out[l]), g_post_mix=g_post_mix[l], g_pre_ffn=g_pre_ffn[l],
                 w_ffn_in=_bf(w_ffn_in[l]), w_ffn_out=_bf(w_ffn_out[l]), g_post_ffn=g_post_ffn[l])
        mk, mv = _norm_proj(mem_prompt.reshape(bp * m, d), g_mem[l], _bf(w_mem_kv[l]), (w, w), "mem_kv")
        mk, mv = mk.reshape(bp, m, w), mv.reshape(bp, m, w)
        zeros_state = jnp.zeros((bp, HEADS, HEAD_DIM, HEAD_DIM), F32)
        yp, cb, sh, sg = _layer(yp, mk, mv, jnp.zeros((bp, CONV_W - 1, 3 * w), F32),
                                zeros_state, zeros_state, l, hg_lb_logits, p)
        bs = x_sample.shape[0]
        ys, cb2, sh2, sg2 = _layer(ys, _of_layer(cache_mem_k, l), _of_layer(cache_mem_v, l),
                                   _of_layer(state_gdn_conv, l), _of_layer(state_hgrn, l),
                                   _of_layer(state_gdn, l), l, hg_lb_logits, p)
        for lst, val in zip(outs, (mk.reshape(bp, m, HEADS, HEAD_DIM), mv.reshape(bp, m, HEADS, HEAD_DIM),
                                   sh, sg, cb, sh2, sg2, cb2)):
            lst.append(val)
    return (yp, ys) + tuple(o[0].reshape((1,) + o[0].shape) if depth == 1 else jnp.stack(o) for o in outs)
```

```python
import functools
import math

import numpy as np
import jax
import jax.numpy as jnp
from jax import lax
from jax.experimental import pallas as pl
from jax.experimental.pallas import tpu as pltpu

F32 = jnp.float32
BF16 = jnp.bfloat16

EPS = 1e-6
HEADS = 4
HEAD_DIM = 128
CONV_W = 4
N_BRANCH = 3

LANES = 128
SUBLANES = 8
VMEM_LIMIT = 56 * 1024 * 1024

SAFE_LOG_RANGE = 70.0
HG_CHUNK = 128
HG_BOUNDED_CHUNK = 64
HG_CHUNKS_PER_STEP = 4
GDN_CHUNK = 64
GDN_CHUNKS_PER_STEP = 8
TOKEN_TILE = 512
ATTN_TILE = 512
FFN_CHUNK = 256
DECODE_BLOCK = 8
DECODE_INTERLEAVE = 2
ATTN_DECODE_BLOCK = 8


def _cparams(*sem):
    return pltpu.CompilerParams(dimension_semantics=sem, vmem_limit_bytes=VMEM_LIMIT)


def _resident(shape):
    nd = len(shape)
    return pl.BlockSpec(shape, lambda *_: (0,) * nd, pipeline_mode=pl.Buffered(1))


def _bf(x):
    return x.astype(BF16)


def _dot(a, b):
    return jnp.dot(a, b, preferred_element_type=F32)


def _dot_nt(a, b):
    return lax.dot_general(a, b, (((1,), (1,)), ((), ())), preferred_element_type=F32)


def _dot_tn(a, b):
    return lax.dot_general(a, b, (((0,), (0,)), ((), ())), preferred_element_type=F32)


def _bmm(a, b):
    return jnp.einsum("nij,njk->nik", a, b, preferred_element_type=F32)


def _bmm_nt(a, b):
    return jnp.einsum("nid,njd->nij", a, b, preferred_element_type=F32)


def _bmm_tn(a, b):
    return jnp.einsum("nci,ncj->nij", a, b, preferred_element_type=F32)


def _split2(x):
    hi = _bf(x)
    lo = _bf(x - hi.astype(F32))
    return hi, lo


def _dot_exact_lhs(w_bf16, x):
    hi, lo = _split2(x)
    return _dot(w_bf16, hi) + _dot(w_bf16, lo)


def _rms(x, g):
    return x * lax.rsqrt(jnp.mean(x * x, axis=-1, keepdims=True) + EPS) * g


def _head(h):
    return slice(h * HEAD_DIM, (h + 1) * HEAD_DIM)


def _block_diag(x, nblk):
    wblk = x.shape[-1] // nblk
    if wblk % LANES == 0:
        zeros = lambda n: [jnp.zeros(x.shape[:-1] + (n * wblk,), x.dtype)] if n else []
        parts = [jnp.concatenate(zeros(j) + [x[..., j * wblk:(j + 1) * wblk]] + zeros(nblk - 1 - j), axis=-1)
                 for j in range(nblk)]
    else:
        lane = lax.broadcasted_iota(jnp.int32, x.shape, x.ndim - 1)
        parts = [jnp.where((lane >= j * wblk) & (lane < (j + 1) * wblk), x, jnp.zeros_like(x))
                 for j in range(nblk)]
    return jnp.concatenate(parts, axis=-2)


def _norm_proj_kernel(x_ref, g_ref, w_ref, *out_refs):
    xn = _bf(_rms(x_ref[...], g_ref[...]))
    off = 0
    for o_ref in out_refs:
        width = o_ref.shape[-1]
        for c0 in range(0, width, 512):
            cw = min(512, width - c0)
            o_ref[:, c0:c0 + cw] = _dot(xn, w_ref[:, off + c0:off + c0 + cw])
        off += width


def _norm_proj(x2d, g, w_bf16, widths, name):
    n, d = x2d.shape
    tm = min(TOKEN_TILE, n)
    assert n % tm == 0 and sum(widths) == w_bf16.shape[1]
    return pl.pallas_call(
        _norm_proj_kernel,
        grid=(n // tm,),
        in_specs=[pl.BlockSpec((tm, d), lambda i: (i, 0)),
                  _resident((1, d)),
                  _resident(w_bf16.shape)],
        out_specs=[pl.BlockSpec((tm, w), lambda i: (i, 0)) for w in widths],
        out_shape=[jax.ShapeDtypeStruct((n, w), F32) for w in widths],
        compiler_params=_cparams("arbitrary"),
        name=name,
    )(x2d, g.reshape(1, d), w_bf16)


def _forget_lower_bound(logits, layer):
    m = jnp.max(logits, axis=0, keepdims=True)
    e = jnp.exp(logits - m)
    return jnp.sum(e[:layer + 1], axis=0, keepdims=True) / jnp.sum(e, axis=0, keepdims=True)


def _col_to_matrix(row):
    n = row.shape[-1]
    return jnp.broadcast_to(row, (n, n)).T


def _col_to_matrix_mxu(row):
    n = row.shape[-1]
    tile = jnp.concatenate([row, jnp.zeros((SUBLANES - 1, n), row.dtype)], axis=0)
    hi = _bf(tile)
    mid = _bf(tile - hi.astype(F32))
    lo = _bf(tile - hi.astype(F32) - mid.astype(F32))
    ones = jnp.ones((SUBLANES, n), BF16)
    return _dot_tn(hi, ones) + _dot_tn(mid, ones) + _dot_tn(lo, ones)


def _gdn_inputs(ext_ref, c, wconv, ab, alog, dtb, t_valid):
    if c > SUBLANES:
        full = ext_ref[...]
        acc = full * wconv[CONV_W - 1:CONV_W, :]
        for j in range(CONV_W - 1):
            acc = acc + pltpu.roll(full, CONV_W - 1 - j, axis=0) * wconv[j:j + 1, :]
        acc = acc[SUBLANES:SUBLANES + c]
    else:
        acc = ext_ref[pl.ds(SUBLANES - 3, c), :] * wconv[0:1, :]
        for j in range(1, CONV_W):
            acc = acc + ext_ref[pl.ds(SUBLANES - 3 + j, c), :] * wconv[j:j + 1, :]
    qkv = jax.nn.silu(acc)
    w = HEADS * HEAD_DIM
    qs, ks = [], []
    for h in range(HEADS):
        qh = qkv[:, h * HEAD_DIM:(h + 1) * HEAD_DIM]
        kh = qkv[:, w + h * HEAD_DIM:w + (h + 1) * HEAD_DIM]
        qs.append(qh * lax.rsqrt(jnp.sum(qh * qh, axis=-1, keepdims=True) + EPS) * (HEAD_DIM ** -0.5))
        ks.append(kh * lax.rsqrt(jnp.sum(kh * kh, axis=-1, keepdims=True) + EPS))
    v = qkv[:, 2 * w:3 * w]
    glog = -jnp.exp(alog) * jax.nn.softplus(ab + dtb)
    beta = jax.nn.sigmoid(ab)
    if t_valid is not None:
        valid = lax.broadcasted_iota(jnp.int32, ab.shape, 0) < t_valid
        glog = jnp.where(valid, glog, 0.0)
        beta = jnp.where(valid, beta, 0.0)
    return qs, ks, v, glog, beta


def _hier_tables(c):
    t = np.arange(c)
    u = np.arange(c)
    tri = (u[None, :] <= t[:, None]).astype(np.float32)
    rows, masks = [], []
    b = c // 2
    while b >= 1:
        ref = (t // (2 * b)) * 2 * b + b - 1
        rows.append(tri - (u[None, :] <= ref[:, None]).astype(np.float32))
        right = (t % (2 * b)) >= b
        left = (t % (2 * b)) < b
        same = (t[:, None] // (2 * b)) == (t[None, :] // (2 * b))
        masks.append((right[:, None] & left[None, :] & same).astype(np.float32))
        b //= 2
    rows.append(tri)
    return np.concatenate(rows, 0), np.stack(masks, 0)


def _hgrn_chunk(q, k, v, logf_hi, logf_lo, s, wall_ref, masks, c):
    nlev = len(masks)
    d = _dot(wall_ref[...], jnp.concatenate([logf_hi, logf_lo], axis=0))
    g = d[nlev * c:(nlev + 1) * c]
    gc = g[c - 1:c, :]
    qg = _bf(q * jnp.exp(g))
    kg = _bf(k * jnp.exp(gc - g))
    egc = jnp.exp(gc)
    vb = _bf(v)
    npair = HEADS // 2
    a = [None] * npair
    for l in range(nlev):
        e = jnp.exp(-jnp.abs(d[l * c:(l + 1) * c]))
        ql, kl = _bf(q * e), _bf(k * e)
        for pr in range(npair):
            pair = slice(2 * pr * HEAD_DIM, (2 * pr + 2) * HEAD_DIM)
            p = _dot_nt(ql[:, pair], _block_diag(kl[:, pair], 2))
            a[pr] = jnp.where(masks[l], p, 0.0) if a[pr] is None else jnp.where(masks[l], p, a[pr])
    qk = q * k
    outs, s_new = [], []
    for h in range(HEADS):
        sl = _head(h)
        a_h = a[h // 2][:, (h % 2) * c:(h % 2 + 1) * c]
        o = _dot(jnp.concatenate([qg[:, sl], _bf(a_h)], axis=1),
                 jnp.concatenate([_bf(s[h]), vb[:, sl]], axis=0))
        outs.append(o + jnp.sum(qk[:, sl], axis=-1, keepdims=True) * v[:, sl])
        s_new.append(_col_to_matrix(egc[:, sl]) * s[h] + _dot_tn(kg[:, sl], vb[:, sl]))
    return outs, s_new


def _hgrn_chunk_bounded(q, k, v, g, s, c):
    gc = g[c - 1:c, :]
    qg = _bf(q * jnp.exp(g))
    kn = _bf(k * jnp.exp(-g))
    kg = _bf(k * jnp.exp(gc - g))
    egc = jnp.exp(gc)
    vb = _bf(v)
    ti = lax.broadcasted_iota(jnp.int32, (c, HEADS * c), 0)
    si = lax.broadcasted_iota(jnp.int32, (c, HEADS * c), 1) & (c - 1)
    a = jnp.where(ti > si, _dot_nt(qg, _block_diag(kn, HEADS)), 0.0)
    v_rows = jnp.concatenate([vb[:, _head(h)] for h in range(HEADS)], axis=0)
    intra = _dot(_block_diag(_bf(a), HEADS), v_rows)
    qk = q * k
    outs, s_new = [], []
    for h in range(HEADS):
        sl = _head(h)
        outs.append(_dot(qg[:, sl], _bf(s[h])) + intra[h * c:(h + 1) * c]
                    + jnp.sum(qk[:, sl], axis=-1, keepdims=True) * v[:, sl])
        s_new.append(_col_to_matrix(egc[:, sl]) * s[h] + _dot_tn(kg[:, sl], vb[:, sl]))
    return outs, s_new


def _hgrn_kernel(q_ref, f_ref, i_ref, gate_ref, lbl_ref, gout_ref, wall_ref, mask_ref, tri_ref, s0_ref,
                 o_ref, sn_ref, s_scr, *, c, nch, layer):
    n = pl.program_id(1)

    @pl.when(n == 0)
    def _():
        s_scr[...] = s0_ref[0]

    lb = _forget_lower_bound(lbl_ref[...], layer)
    gout = gout_ref[...]
    f = lb + (1.0 - lb) * jax.nn.sigmoid(f_ref[0])
    hi, lo = _split2(jnp.log(f))
    k = 1.0 - f
    nlev = mask_ref.shape[0]

    def emit(r, outs):
        for h in range(HEADS):
            sl = _head(h)
            o_ref[0, r, sl] = _rms(outs[h], gout[:, sl]) * jax.nn.silu(gate_ref[0, r, sl])

    def run(chunk_len, chunk_fn):
        s = [s_scr[h] for h in range(HEADS)]
        for ch in range(nch * c // chunk_len):
            r = slice(ch * chunk_len, (ch + 1) * chunk_len)
            outs, s = chunk_fn(ch, r, s)
            emit(r, outs)
        for h in range(HEADS):
            s_scr[h] = s[h]

    cb = tri_ref.shape[0]
    half = lambda ch: slice(ch * cb, (ch + 1) * cb)
    gs = [_dot(tri_ref[...], jnp.concatenate([hi[half(ch)], lo[half(ch)]], axis=0))
          for ch in range(nch * c // cb)]
    gmin = functools.reduce(jnp.minimum, [jnp.min(g) for g in gs])
    bounded = gmin >= -SAFE_LOG_RANGE

    @pl.when(bounded)
    def _():
        run(cb, lambda ch, r, s: _hgrn_chunk_bounded(q_ref[0, r, :], k[r], i_ref[0, r, :], gs[ch], s, cb))

    @pl.when(jnp.logical_not(bounded))
    def _():
        masks = [mask_ref[l] != 0.0 for l in range(nlev)]
        run(c, lambda ch, r, s: _hgrn_chunk(q_ref[0, r, :], k[r], i_ref[0, r, :], hi[r], lo[r], s,
                                            wall_ref, masks, c))

    @pl.when(n == pl.num_programs(1) - 1)
    def _():
        sn_ref[0] = s_scr[...]


def _hgrn_long(hgp, lb_logits, g_out, s0, layer):
    bsz, t, _ = hgp.shape
    c = HG_CHUNK
    nch = HG_CHUNKS_PER_STEP
    tb = c * nch
    assert t % tb == 0 and c == HEAD_DIM and HEADS % 2 == 0
    w = HEADS * HEAD_DIM
    wall, masks = _hier_tables(c)
    wall = np.concatenate([wall, wall], axis=1)
    masks = np.concatenate([masks, masks], axis=2)
    cb = HG_BOUNDED_CHUNK
    tri = np.tril(np.ones((cb, cb), np.float32))
    tri = np.concatenate([tri, tri], axis=1)
    col = lambda j: pl.BlockSpec((1, tb, w), lambda b, n, j=j: (b, n, j))
    st = pl.BlockSpec((1, HEADS, HEAD_DIM, HEAD_DIM), lambda b, n: (b, 0, 0, 0))
    return pl.pallas_call(
        functools.partial(_hgrn_kernel, c=c, nch=nch, layer=layer),
        grid=(bsz, t // tb),
        in_specs=[col(0), col(1), col(2), col(3),
                  _resident(lb_logits.shape), _resident((1, w)),
                  _resident(wall.shape), _resident(masks.shape), _resident(tri.shape), st],
        out_specs=[pl.BlockSpec((1, tb, w), lambda b, n: (b, n, 0)), st],
        out_shape=[jax.ShapeDtypeStruct((bsz, t, w), F32),
                   jax.ShapeDtypeStruct(s0.shape, F32)],
        scratch_shapes=[pltpu.VMEM((HEADS, HEAD_DIM, HEAD_DIM), F32)],
        compiler_params=_cparams("arbitrary", "arbitrary"),
        name="hgrn_long",
    )(hgp, hgp, hgp, hgp, lb_logits, g_out.reshape(1, w),
      jnp.asarray(wall, BF16), jnp.asarray(masks, F32), jnp.asarray(tri, BF16), s0)


def _gdn_prepare(qs, ks, v, glog, beta, tri_bf, c, nch):
    nb = nch * HEADS
    wc = HEADS * c
    rows = lambda x, ch: x[ch * c:(ch + 1) * c]
    per = lambda f: jnp.stack([f(ch, h) for ch in range(nch) for h in range(HEADS)], axis=0)
    q = per(lambda ch, h: rows(qs[h], ch))
    k = per(lambda ch, h: rows(ks[h], ch))
    vv = per(lambda ch, h: rows(v, ch)[:, _head(h)])
    beta_b = per(lambda ch, h: jnp.broadcast_to(rows(beta, ch)[:, HEADS + h:HEADS + h + 1], (c, LANES)))
    lane = lax.broadcasted_iota(jnp.int32, (c, LANES), 1)
    g_small = None
    for ch in range(nch):
        part = jnp.where(lane < HEADS, rows(glog, ch), 0.0)
        part = part if ch == 0 else pltpu.roll(part, ch * HEADS, axis=1)
        g_small = part if g_small is None else g_small + part
    gcum = _dot_exact_lhs(tri_bf, g_small)
    gcum_t = jnp.concatenate([gcum, jnp.zeros((LANES - c, LANES), F32)], axis=0).T
    g = jnp.stack([jnp.broadcast_to(gcum[:, n:n + 1], (c, LANES)) for n in range(nb)], axis=0)
    packed = lambda f: jnp.stack([jnp.concatenate([f(ch * HEADS + h) for h in range(HEADS)], axis=1)
                                  for ch in range(nch)], axis=0)
    g_col = packed(lambda n: jnp.broadcast_to(gcum[:, n:n + 1], (c, c)))
    g_row = packed(lambda n: jnp.broadcast_to(gcum_t[n:n + 1, :c], (c, c)))
    ti = lax.broadcasted_iota(jnp.int32, (nch, c, wc), 1)
    si = lax.broadcasted_iota(jnp.int32, (nch, c, wc), 2) & (c - 1)
    decay = jnp.exp(jnp.minimum(g_col - g_row, 0.0))
    kb = k * beta_b
    heads_on_lanes = lambda x: jnp.stack(
        [jnp.concatenate([x[ch * HEADS + h] for h in range(HEADS)], axis=1) for ch in range(nch)], axis=0)
    kbq = _bf(jnp.concatenate([heads_on_lanes(kb), heads_on_lanes(q)], axis=1))
    kq = _bmm_nt(kbq, _block_diag(_bf(heads_on_lanes(k)), HEADS))
    m = jnp.where(ti > si, kq[:, :c] * decay, 0.0)
    aqk = jnp.where(ti > si, kq[:, c:] * decay, 0.0)
    qk_diag = jnp.sum(q * k, axis=-1, keepdims=True)
    eg = jnp.exp(g)
    rhs = jnp.concatenate([vv * beta_b, kb * eg], axis=-1)
    npow = -m
    poff = npow
    npow_bd = _block_diag(_bf(npow), HEADS)
    for _ in range(int(math.log2(c)) - 1):
        npow = _bmm(_bf(npow), npow_bd)
        npow_bd = _block_diag(_bf(npow), HEADS)
        poff = poff + npow + _bmm(_bf(poff), npow_bd)
    xoff = _bmm(_block_diag(_bf(poff), HEADS), _bf(rhs.reshape(nch, wc, 2 * HEAD_DIM)))
    x = rhs + xoff.reshape(nb, c, 2 * HEAD_DIM)
    u, w = x[:, :, :HEAD_DIM], x[:, :, HEAD_DIM:]
    gc = g[:, c - 1:c, :]
    return (u, _bf(jnp.concatenate([w, q * eg], axis=1)), _block_diag(_bf(aqk), HEADS), qk_diag,
            _bf(k * jnp.exp(gc - g)), jnp.exp(gc))


def _gdn_kernel(qkv_ref, z_ref, ab_ref, buf_ref, wconv_ref, alog_ref, dtb_ref, gout_ref, tri_ref,
                s0_ref, o_ref, sn_ref, s_scr, ext_scr, *, c, nch):
    n = pl.program_id(1)
    tb = c * nch

    @pl.when(n == 0)
    def _():
        s_scr[...] = s0_ref[0]
        ext_scr[0:SUBLANES - 3, :] = jnp.zeros((SUBLANES - 3, ext_scr.shape[1]), F32)
        ext_scr[SUBLANES - 3:SUBLANES, :] = buf_ref[0]

    @pl.when(n > 0)
    def _():
        ext_scr[0:SUBLANES, :] = ext_scr[tb:tb + SUBLANES, :]

    ext_scr[SUBLANES:SUBLANES + tb, :] = qkv_ref[0]
    qs, ks, v, glog, beta = _gdn_inputs(ext_scr, tb, wconv_ref[...], ab_ref[0], alog_ref[...],
                                        dtb_ref[...], None)
    u, wq, aqk_bd, qk_diag, kg, egc = _gdn_prepare(qs, ks, v, glog, beta, tri_ref[...], c, nch)
    gout = gout_ref[...]
    s = s_scr[...]
    for ch in range(nch):
        sel = slice(ch * HEADS, (ch + 1) * HEADS)
        ws_qs = _bmm(wq[sel], _bf(s))
        v_new = u[sel] - ws_qs[:, :c]
        intra = _dot(aqk_bd[ch], _bf(v_new.reshape(HEADS * c, HEAD_DIM)))
        o = ws_qs[:, c:] + intra.reshape(HEADS, c, HEAD_DIM) + qk_diag[sel] * v_new
        s = egc[sel] * s + _bmm_tn(kg[sel], _bf(v_new))
        for h in range(HEADS):
            sl = _head(h)
            o_ref[0, ch * c:(ch + 1) * c, sl] = _rms(o[h], gout) * jax.nn.silu(z_ref[0, ch * c:(ch + 1) * c, sl])
    s_scr[...] = s

    @pl.when(n == pl.num_programs(1) - 1)
    def _():
        sn_ref[0] = s


def _pad_lanes(x):
    return jnp.zeros((1, LANES), F32).at[0, :x.shape[0]].set(x.astype(F32))


def _gdn_long(gqkv, gz, ab, conv_buf, w_conv, a_log, dt_bias, g_out, s0):
    bsz, t, wq = gqkv.shape
    c = GDN_CHUNK
    nch = GDN_CHUNKS_PER_STEP
    tb = c * nch
    assert t % tb == 0
    w = HEADS * HEAD_DIM
    tri = np.tril(np.ones((c, c), np.float32))
    st = pl.BlockSpec((1, HEADS, HEAD_DIM, HEAD_DIM), lambda b, n: (b, 0, 0, 0))
    return pl.pallas_call(
        functools.partial(_gdn_kernel, c=c, nch=nch),
        grid=(bsz, t // tb),
        in_specs=[pl.BlockSpec((1, tb, wq), lambda b, n: (b, n, 0)),
                  pl.BlockSpec((1, tb, w), lambda b, n: (b, n, 0)),
                  pl.BlockSpec((1, tb, LANES), lambda b, n: (b, n, 0)),
                  pl.BlockSpec((1, CONV_W - 1, wq), lambda b, n: (b, 0, 0)),
                  _resident((CONV_W, wq)), _resident((1, LANES)), _resident((1, LANES)),
                  _resident((1, HEAD_DIM)), _resident((c, c)), st],
        out_specs=[pl.BlockSpec((1, tb, w), lambda b, n: (b, n, 0)), st],
        out_shape=[jax.ShapeDtypeStruct((bsz, t, w), F32),
                   jax.ShapeDtypeStruct((bsz, HEADS, HEAD_DIM, HEAD_DIM), F32)],
        scratch_shapes=[pltpu.VMEM((HEADS, HEAD_DIM, HEAD_DIM), F32),
                        pltpu.VMEM((tb + SUBLANES, wq), F32)],
        compiler_params=_cparams("arbitrary", "arbitrary"),
        name="gdn_long",
    )(gqkv, gz, ab, conv_buf, w_conv,
      _pad_lanes(a_log), _pad_lanes(dt_bias),
      g_out.reshape(1, HEAD_DIM), jnp.asarray(tri, BF16), s0)


def _shift_rows(x, d):
    return x if d == 0 else pltpu.roll(x, d, axis=0)


def _cumsum_rows8(x, rows):
    for d in (1, 2, 4):
        x = x + jnp.where(rows >= d, pltpu.roll(x, d, axis=0), 0.0)
    return x


def _hgrn_short(q, hf, v, lb, s, rows, t_valid):
    f = lb + (1.0 - lb) * jax.nn.sigmoid(hf)
    valid = rows < t_valid
    logf = jnp.where(valid, jnp.log(f), 0.0)
    k = jnp.where(valid, 1.0 - f, 0.0)
    g = _cumsum_rows8(logf, rows)
    gc = g[SUBLANES - 1:SUBLANES, :]
    o = _dot(_bf(q * jnp.exp(g)), _bf(s))
    for d in range(t_valid):
        ok = rows >= d
        dec = jnp.exp(jnp.where(ok, g - _shift_rows(g, d), 0.0))
        a = jnp.sum(jnp.where(ok, q * _shift_rows(k, d) * dec, 0.0), axis=-1, keepdims=True)
        o = o + a * _shift_rows(v, d)
    s_new = _col_to_matrix_mxu(jnp.exp(gc)) * s + _dot_tn(_bf(k * jnp.exp(gc - g)), _bf(v))
    return o, s_new


def _gdn_short(q, k, v, g_b, beta_b, s, rows, t_valid):
    g = _cumsum_rows8(g_b, rows)
    gc = g[SUBLANES - 1:SUBLANES, :]
    eg = jnp.exp(g)
    kb = k * beta_b
    dec = [None] + [jnp.exp(jnp.where(rows >= d, g - _shift_rows(g, d), 0.0)) for d in range(1, t_valid)]
    mcol = [None] + [jnp.where(rows >= d, jnp.sum(kb * _shift_rows(k, d), axis=-1, keepdims=True) * dec[d], 0.0)
                     for d in range(1, t_valid)]
    ru, rw = v * beta_b, kb * eg
    xu, xw = ru, rw
    for _ in range(t_valid - 1):
        nu, nw = ru, rw
        for d in range(1, t_valid):
            nu = nu - mcol[d] * _shift_rows(xu, d)
            nw = nw - mcol[d] * _shift_rows(xw, d)
        xu, xw = nu, nw
    ws_qs = _dot(_bf(jnp.concatenate([xw, q * eg], axis=0)), _bf(s))
    v_new = xu - ws_qs[:SUBLANES]
    o = ws_qs[SUBLANES:] + jnp.sum(q * k, axis=-1, keepdims=True) * v_new
    for d in range(1, t_valid):
        a = jnp.where(rows >= d, jnp.sum(q * _shift_rows(k, d), axis=-1, keepdims=True) * dec[d], 0.0)
        o = o + a * _shift_rows(v_new, d)
    s_new = jnp.exp(gc) * s + _dot_tn(_bf(k * jnp.exp(gc - g)), _bf(v_new))
    return o, s_new


def _short_kernel(hq_ref, hf_ref, hi_ref, hgate_ref, qkv_ref, z_ref, ab_ref, buf_ref,
                  lbl_ref, hgout_ref, wconv_ref, alog_ref, dtb_ref, ggout_ref, shg0_ref, sgd0_ref,
                  ohg_ref, ogd_ref, shg_ref, sgd_ref, ext_scr, *, t_valid, layer):
    bb = hq_ref.shape[0]
    rows = lax.broadcasted_iota(jnp.int32, (SUBLANES, LANES), 0)
    lb = _forget_lower_bound(lbl_ref[...], layer)
    hgout = hgout_ref[...]
    ggout = ggout_ref[...]
    wconv = wconv_ref[...]
    alog = alog_ref[...]
    dtb = dtb_ref[...]

    def per_seq(i, ext):
        ext[SUBLANES - 3:SUBLANES, :] = buf_ref[i]
        ext[SUBLANES:2 * SUBLANES, :] = qkv_ref[i]
        qs, ks, v, glog, beta = _gdn_inputs(ext, SUBLANES, wconv, ab_ref[i], alog, dtb, t_valid)
        for h in range(HEADS):
            sl = _head(h)
            o, s_new = _hgrn_short(hq_ref[i, :, sl], hf_ref[i, :, sl], hi_ref[i, :, sl], lb[:, sl],
                                   shg0_ref[i, h], rows, t_valid)
            shg_ref[i, h] = s_new
            ohg_ref[i, :, sl] = _rms(o, hgout[:, sl]) * jax.nn.silu(hgate_ref[i, :, sl])
            g_b = jnp.broadcast_to(glog[:, h:h + 1], (SUBLANES, LANES))
            beta_b = jnp.broadcast_to(beta[:, HEADS + h:HEADS + h + 1], (SUBLANES, LANES))
            o, s_new = _gdn_short(qs[h], ks[h], v[:, sl], g_b, beta_b, sgd0_ref[i, h], rows, t_valid)
            sgd_ref[i, h] = s_new
            ogd_ref[i, :, sl] = _rms(o, ggout) * jax.nn.silu(z_ref[i, :, sl])

    def per_group(j, carry):
        for u in range(DECODE_INTERLEAVE):
            per_seq(j * DECODE_INTERLEAVE + u, ext_scr.at[u])
        return carry

    lax.fori_loop(0, bb // DECODE_INTERLEAVE, per_group, 0)


def _mixers_short(hgp, gqkv, gz, ab, conv_buf, lb_logits, hg_out, w_conv, a_log, dt_bias, gdn_out,
                  s_hg0, s_gdn0, t_valid, layer):
    bsz, t, wq = gqkv.shape
    assert t == SUBLANES and CONV_W - 1 <= t_valid <= SUBLANES
    bb = DECODE_BLOCK
    assert bsz % bb == 0 and bb % DECODE_INTERLEAVE == 0
    w = HEADS * HEAD_DIM
    col = lambda j: pl.BlockSpec((bb, t, w), lambda b, j=j: (b, 0, j))
    tok = lambda width: pl.BlockSpec((bb, t, width), lambda b: (b, 0, 0))
    st = pl.BlockSpec((bb, HEADS, HEAD_DIM, HEAD_DIM), lambda b: (b, 0, 0, 0))
    return pl.pallas_call(
        functools.partial(_short_kernel, t_valid=t_valid, layer=layer),
        grid=(bsz // bb,),
        in_specs=[col(0), col(1), col(2), col(3), tok(wq), tok(w), tok(LANES),
                  pl.BlockSpec((bb, CONV_W - 1, wq), lambda b: (b, 0, 0)),
                  _resident(lb_logits.shape), _resident((1, w)), _resident((CONV_W, wq)),
                  _resident((1, LANES)), _resident((1, LANES)), _resident((1, HEAD_DIM)), st, st],
        out_specs=[tok(w), tok(w), st, st],
        out_shape=[jax.ShapeDtypeStruct((bsz, t, w), F32), jax.ShapeDtypeStruct((bsz, t, w), F32),
                   jax.ShapeDtypeStruct(s_hg0.shape, F32), jax.ShapeDtypeStruct(s_gdn0.shape, F32)],
        scratch_shapes=[pltpu.VMEM((DECODE_INTERLEAVE, 2 * SUBLANES, wq), F32)],
        compiler_params=_cparams("arbitrary"),
        name="mixers_short",
    )(hgp, hgp, hgp, hgp, gqkv, gz, ab, conv_buf, lb_logits, hg_out.reshape(1, w), w_conv,
      _pad_lanes(a_log), _pad_lanes(dt_bias), gdn_out.reshape(1, HEAD_DIM), s_hg0, s_gdn0)


def _mem_attn_kernel(q_ref, k_ref, v_ref, o_ref):
    for h in range(HEADS):
        sl = _head(h)
        s = _bmm_nt(_bf(q_ref[:, :, sl]), _bf(k_ref[:, :, sl])) * (HEAD_DIM ** -0.5)
        e = jnp.exp(s - jnp.max(s, axis=-1, keepdims=True))
        o_ref[:, :, sl] = _bmm(_bf(e), _bf(v_ref[:, :, sl])) / jnp.sum(e, axis=-1, keepdims=True)


def _mem_attn(mq, mem_k, mem_v, bb, tq):
    bsz, t, w = mq.shape
    m = mem_k.shape[1]
    assert bsz % bb == 0 and t % tq == 0
    kv = pl.BlockSpec((bb, m, w), lambda b, n: (b, 0, 0))
    return pl.pallas_call(
        _mem_attn_kernel,
        grid=(bsz // bb, t // tq),
        in_specs=[pl.BlockSpec((bb, tq, w), lambda b, n: (b, n, 0)), kv, kv],
        out_specs=pl.BlockSpec((bb, tq, w), lambda b, n: (b, n, 0)),
        out_shape=jax.ShapeDtypeStruct((bsz, t, w), F32),
        compiler_params=_cparams("arbitrary", "arbitrary"),
        name="mem_attn",
    )(mq, mem_k, mem_v)


def _mem_attn_cache_kernel(q_ref, k_hbm, v_hbm, o_ref, kbuf, vbuf, sem):
    i = pl.program_id(0)
    bb = q_ref.shape[0]

    def copies(step, slot):
        out = []
        for h in range(HEADS):
            for j, (src, dst) in enumerate(((k_hbm, kbuf), (v_hbm, vbuf))):
                out.append(pltpu.make_async_copy(src.at[pl.ds(step * bb, bb), :, h, :],
                                                 dst.at[slot, h], sem.at[j, slot, h]))
        return out

    slot = i % 2

    @pl.when(i == 0)
    def _():
        for cp in copies(0, 0):
            cp.start()

    @pl.when(i + 1 < pl.num_programs(0))
    def _():
        for cp in copies(i + 1, 1 - slot):
            cp.start()

    for cp in copies(i, slot):
        cp.wait()
    for h in range(HEADS):
        sl = _head(h)
        s = _bmm_nt(_bf(q_ref[:, :, sl]), _bf(kbuf[slot, h])) * (HEAD_DIM ** -0.5)
        e = jnp.exp(s - jnp.max(s, axis=-1, keepdims=True))
        o_ref[:, :, sl] = _bmm(_bf(e), _bf(vbuf[slot, h])) / jnp.sum(e, axis=-1, keepdims=True)


def _mem_attn_cache(mq, cache_k, cache_v, bb):
    bsz, t, w = mq.shape
    m = cache_k.shape[1]
    assert bsz % bb == 0 and cache_k.shape == (bsz, m, HEADS, HEAD_DIM)
    return pl.pallas_call(
        _mem_attn_cache_kernel,
        grid=(bsz // bb,),
        in_specs=[pl.BlockSpec((bb, t, w), lambda b: (b, 0, 0)),
                  pl.BlockSpec(memory_space=pl.ANY), pl.BlockSpec(memory_space=pl.ANY)],
        out_specs=pl.BlockSpec((bb, t, w), lambda b: (b, 0, 0)),
        out_shape=jax.ShapeDtypeStruct((bsz, t, w), F32),
        scratch_shapes=[pltpu.VMEM((2, HEADS, bb, m, HEAD_DIM), F32),
                        pltpu.VMEM((2, HEADS, bb, m, HEAD_DIM), F32),
                        pltpu.SemaphoreType.DMA((2, 2, HEADS))],
        compiler_params=_cparams("arbitrary"),
        name="mem_attn_cache",
    )(mq, cache_k, cache_v)


def _merge_ffn_kernel(x_ref, ohg_ref, ogd_ref, omem_ref, gpre_ref, wgates_ref,
                      wbh_ref, wbg_ref, wbm_ref, wout_ref, gpm_ref, gpf_ref,
                      wffn_ref, wdown_ref, gpo_ref, y_ref):
    d = x_ref.shape[-1]
    x = x_ref[...]
    xn = _bf(_rms(x, gpre_ref[...]))
    merged = None
    for j, (o_ref, wb_ref) in enumerate(((ohg_ref, wbh_ref), (ogd_ref, wbg_ref), (omem_ref, wbm_ref))):
        gate = jax.nn.sigmoid(_dot(xn, wgates_ref[:, j * d:(j + 1) * d]))
        term = gate * _dot(_bf(o_ref[...]), wb_ref[...])
        merged = term if merged is None else merged + term
    h = x + _rms(_dot(_bf(merged), wout_ref[...]), gpm_ref[...])
    hn = _bf(_rms(h, gpf_ref[...]))
    hidden = wdown_ref.shape[0]
    ff = None
    for c0 in range(0, hidden, FFN_CHUNK):
        a = (jax.nn.silu(_dot(hn, wffn_ref[:, c0:c0 + FFN_CHUNK]))
             * _dot(hn, wffn_ref[:, hidden + c0:hidden + c0 + FFN_CHUNK]))
        p = _dot(_bf(a), wdown_ref[c0:c0 + FFN_CHUNK, :])
        ff = p if ff is None else ff + p
    y_ref[...] = h + _rms(ff, gpo_ref[...])


def _merge_ffn(x2d, ohg, ogd, omem, gpre, wgates, wbh, wbg, wbm, wout, gpm, gpf, wffn, wdown, gpo):
    n, d = x2d.shape
    tm = min(TOKEN_TILE, n)
    hidden = wdown.shape[0]
    assert n % tm == 0 and hidden % FFN_CHUNK == 0 and wffn.shape[1] == 2 * hidden
    assert wgates.shape == (d, N_BRANCH * d)
    tok = lambda width: pl.BlockSpec((tm, width), lambda i: (i, 0))
    row = lambda a: a.reshape(1, d)
    return pl.pallas_call(
        _merge_ffn_kernel,
        grid=(n // tm,),
        in_specs=[tok(d), tok(ohg.shape[1]), tok(ogd.shape[1]), tok(omem.shape[1]),
                  _resident((1, d)), _resident(wgates.shape),
                  _resident(wbh.shape), _resident(wbg.shape), _resident(wbm.shape), _resident(wout.shape),
                  _resident((1, d)), _resident((1, d)),
                  _resident(wffn.shape), _resident(wdown.shape), _resident((1, d))],
        out_specs=tok(d),
        out_shape=jax.ShapeDtypeStruct((n, d), F32),
        compiler_params=_cparams("arbitrary"),
        name="merge_ffn",
    )(x2d, ohg, ogd, omem, row(gpre), wgates, wbh, wbg, wbm, wout, row(gpm), row(gpf), wffn, wdown, row(gpo))


def _pack_w_in_kernel(w_ref, mix_ref, gates_ref):
    w = HEADS * HEAD_DIM
    c_ab = 8 * w
    c_mq = c_ab + 2 * HEADS
    c_gt = c_mq + w
    x = w_ref[...]
    mix_ref[:, 0:c_ab] = _bf(x[:, 0:c_ab])
    mix_ref[:, c_ab:c_ab + w] = _bf(x[:, c_mq:c_gt])
    pad = jnp.zeros((x.shape[0], LANES - 2 * HEADS), x.dtype)
    mix_ref[:, c_ab + w:c_ab + w + LANES] = _bf(jnp.concatenate([x[:, c_ab:c_mq], pad], axis=1))
    gates_ref[...] = _bf(x[:, c_gt:])


def _pack_w_in(w_in):
    d, total = w_in.shape
    w = HEADS * HEAD_DIM
    widths = (4 * w, 3 * w, w, w, LANES)
    n_gates = total - (9 * w + 2 * HEADS)
    rb = LANES
    assert d % rb == 0 and n_gates % LANES == 0
    mixers, gates = pl.pallas_call(
        _pack_w_in_kernel,
        grid=(d // rb,),
        in_specs=[pl.BlockSpec((rb, total), lambda i: (i, 0))],
        out_specs=[pl.BlockSpec((rb, sum(widths)), lambda i: (i, 0)),
                   pl.BlockSpec((rb, n_gates), lambda i: (i, 0))],
        out_shape=[jax.ShapeDtypeStruct((d, sum(widths)), BF16), jax.ShapeDtypeStruct((d, n_gates), BF16)],
        compiler_params=_cparams("arbitrary"),
        name="pack_w_in",
    )(w_in)
    return mixers, widths, gates


def _of_layer(a, l):
    return a.reshape(a.shape[1:]) if a.shape[0] == 1 else a[l]


def _layer(x, mem_k, mem_v, conv_buf, s_hg, s_gdn, layer, lb_logits, p):
    bsz, t_valid, d = x.shape
    short = t_valid <= SUBLANES
    t = SUBLANES if short else t_valid
    if t != t_valid:
        x = jnp.pad(x, ((0, 0), (0, t - t_valid), (0, 0)))
    x2d = x.reshape(bsz * t, d)
    w = HEADS * HEAD_DIM

    hgp, gqkv, gz, mq, ab = _norm_proj(x2d, p["g_pre_mix"], p["w_in"], p["w_in_widths"], "in_proj")
    as3d = lambda a: a.reshape(bsz, t, a.shape[-1])
    gqkv3 = as3d(gqkv)
    if short:
        o_hg, o_gdn, s_hg_new, s_gdn_new = _mixers_short(
            as3d(hgp), gqkv3, as3d(gz), as3d(ab), conv_buf, lb_logits, p["g_hg_out"], p["w_conv"],
            p["a_log"], p["dt_bias"], p["g_gdn_out"], s_hg, s_gdn, t_valid, layer)
        o_mem = _mem_attn_cache(as3d(mq), mem_k, mem_v, ATTN_DECODE_BLOCK)
    else:
        o_hg, s_hg_new = _hgrn_long(as3d(hgp), lb_logits, p["g_hg_out"], s_hg, layer)
        o_gdn, s_gdn_new = _gdn_long(gqkv3, as3d(gz), as3d(ab), conv_buf, p["w_conv"], p["a_log"],
                                     p["dt_bias"], p["g_gdn_out"], s_gdn)
        o_mem = _mem_attn(as3d(mq), mem_k, mem_v, 1, min(ATTN_TILE, t))
    assert t_valid >= CONV_W - 1
    conv_new = gqkv3[:, t_valid - (CONV_W - 1):t_valid, :]

    flat = lambda a: a.reshape(bsz * t, w)
    y = _merge_ffn(x2d, flat(o_hg), flat(o_gdn), flat(o_mem), p["g_pre_mix"], p["w_gates"],
                   p["w_br_hg"], p["w_br_gdn"], p["w_br_mem"], p["w_out"], p["g_post_mix"],
                   p["g_pre_ffn"], p["w_ffn_in"], p["w_ffn_out"], p["g_post_ffn"])
    y = y.reshape(bsz, t, d)[:, :t_valid]
    return y, conv_new, s_hg_new, s_gdn_new


def kernel(x_prompt, x_sample, mem_prompt, cache_mem_k, cache_mem_v, state_hgrn, state_gdn, state_gdn_conv, hg_lb_logits, g_pre_mix, w_in, w_conv, a_log, dt_bias, g_hg_out, g_gdn_out, g_mem, w_mem_kv, w_br_hg, w_br_gdn, w_br_mem, w_out, g_post_mix, g_pre_ffn, w_ffn_in, w_ffn_out, g_post_ffn):
    depth = w_in.shape[0]
    bp, _, d = x_prompt.shape
    m = mem_prompt.shape[1]
    w = HEADS * HEAD_DIM
    yp, ys = x_prompt, x_sample
    outs = [[] for _ in range(8)]
    for l in range(depth):
        w_in_packed, widths, w_gates = _pack_w_in(_of_layer(w_in, l))
        p = dict(g_pre_mix=g_pre_mix[l], w_in=w_in_packed, w_in_widths=widths, w_gates=w_gates,
                 w_conv=w_conv[l],
                 a_log=a_log[l], dt_bias=dt_bias[l], g_hg_out=g_hg_out[l], g_gdn_out=g_gdn_out[l],
                 w_br_hg=_bf(w_br_hg[l]), w_br_gdn=_bf(w_br_gdn[l]), w_br_mem=_bf(w_br_mem[l]),
                 w_out=_bf(w_out[l]), g_post_mix=g_post_mix[l], g_pre_ffn=g_pre_ffn[l],
                 w_ffn_in=_bf(w_ffn_in[l]), w_ffn_out=_bf(w_ffn_out[l]), g_post_ffn=g_post_ffn[l])
        mk, mv = _norm_proj(mem_prompt.reshape(bp * m, d), g_mem[l], _bf(w_mem_kv[l]), (w, w), "mem_kv")
        mk, mv = mk.reshape(bp, m, w), mv.reshape(bp, m, w)
        zeros_state = jnp.zeros((bp, HEADS, HEAD_DIM, HEAD_DIM), F32)
        yp, cb, sh, sg = _layer(yp, mk, mv, jnp.zeros((bp, CONV_W - 1, 3 * w), F32),
                                zeros_state, zeros_state, l, hg_lb_logits, p)
        bs = x_sample.shape[0]
        ys, cb2, sh2, sg2 = _layer(ys, _of_layer(cache_mem_k, l), _of_layer(cache_mem_v, l),
                                   _of_layer(state_gdn_conv, l), _of_layer(state_hgrn, l),
                                   _of_layer(state_gdn, l), l, hg_lb_logits, p)
        for lst, val in zip(outs, (mk.reshape(bp, m, HEADS, HEAD_DIM), mv.reshape(bp, m, HEADS, HEAD_DIM),
                                   sh, sg, cb, sh2, sg2, cb2)):
            lst.append(val)
    return (yp, ys) + tuple(o[0].reshape((1,) + o[0].shape) if depth == 1 else jnp.stack(o) for o in outs)
```

```python
import functools
import math

import numpy as np
import jax
import jax.numpy as jnp
from jax import lax
from jax.experimental import pallas as pl
from jax.experimental.pallas import tpu as pltpu

F32 = jnp.float32
BF16 = jnp.bfloat16

EPS = 1e-6
HEADS = 4
HEAD_DIM = 128
CONV_W = 4
N_BRANCH = 3

LANES = 128
SUBLANES = 8
VMEM_LIMIT = 56 * 1024 * 1024

SAFE_LOG_RANGE = 70.0
HG_CHUNK = 128
HG_BOUNDED_CHUNK = 64
HG_CHUNKS_PER_STEP = 4
GDN_CHUNK = 64
GDN_CHUNKS_PER_STEP = 8
TOKEN_TILE = 512
ATTN_TILE = 512
FFN_CHUNK = 256
DECODE_BLOCK = 8
DECODE_INTERLEAVE = 2
ATTN_DECODE_BLOCK = 8


def _cparams(*sem):
    return pltpu.CompilerParams(dimension_semantics=sem, vmem_limit_bytes=VMEM_LIMIT)


def _resident(shape):
    nd = len(shape)
    return pl.BlockSpec(shape, lambda *_: (0,) * nd, pipeline_mode=pl.Buffered(1))


def _bf(x):
    return x.astype(BF16)


def _dot(a, b):
    return jnp.dot(a, b, preferred_element_type=F32)


def _dot_nt(a, b):
    return lax.dot_general(a, b, (((1,), (1,)), ((), ())), preferred_element_type=F32)


def _dot_tn(a, b):
    return lax.dot_general(a, b, (((0,), (0,)), ((), ())), preferred_element_type=F32)


def _bmm(a, b):
    return jnp.einsum("nij,njk->nik", a, b, preferred_element_type=F32)


def _bmm_nt(a, b):
    return jnp.einsum("nid,njd->nij", a, b, preferred_element_type=F32)


def _bmm_tn(a, b):
    return jnp.einsum("nci,ncj->nij", a, b, preferred_element_type=F32)


def _split2(x):
    hi = _bf(x)
    lo = _bf(x - hi.astype(F32))
    return hi, lo


def _dot_exact_lhs(w_bf16, x):
    hi, lo = _split2(x)
    return _dot(w_bf16, hi) + _dot(w_bf16, lo)


def _rms(x, g):
    return x * lax.rsqrt(jnp.mean(x * x, axis=-1, keepdims=True) + EPS) * g


def _head(h):
    return slice(h * HEAD_DIM, (h + 1) * HEAD_DIM)


def _block_diag(x, nblk):
    wblk = x.shape[-1] // nblk
    if wblk % LANES == 0:
        zeros = lambda n: [jnp.zeros(x.shape[:-1] + (n * wblk,), x.dtype)] if n else []
        parts = [jnp.concatenate(zeros(j) + [x[..., j * wblk:(j + 1) * wblk]] + zeros(nblk - 1 - j), axis=-1)
                 for j in range(nblk)]
    else:
        lane = lax.broadcasted_iota(jnp.int32, x.shape, x.ndim - 1)
        parts = [jnp.where((lane >= j * wblk) & (lane < (j + 1) * wblk), x, jnp.zeros_like(x))
                 for j in range(nblk)]
    return jnp.concatenate(parts, axis=-2)


def _norm_proj_kernel(x_ref, g_ref, w_ref, *out_refs):
    xn = _bf(_rms(x_ref[...], g_ref[...]))
    off = 0
    for o_ref in out_refs:
        width = o_ref.shape[-1]
        for c0 in range(0, width, 512):
            cw = min(512, width - c0)
            o_ref[:, c0:c0 + cw] = _dot(xn, w_ref[:, off + c0:off + c0 + cw])
        off += width


def _norm_proj(x2d, g, w_bf16, widths, name):
    n, d = x2d.shape
    tm = min(TOKEN_TILE, n)
    assert n % tm == 0 and sum(widths) == w_bf16.shape[1]
    return pl.pallas_call(
        _norm_proj_kernel,
        grid=(n // tm,),
        in_specs=[pl.BlockSpec((tm, d), lambda i: (i, 0)),
                  _resident((1, d)),
                  _resident(w_bf16.shape)],
        out_specs=[pl.BlockSpec((tm, w), lambda i: (i, 0)) for w in widths],
        out_shape=[jax.ShapeDtypeStruct((n, w), F32) for w in widths],
        compiler_params=_cparams("arbitrary"),
        name=name,
    )(x2d, g.reshape(1, d), w_bf16)


def _forget_lower_bound(logits, layer):
    m = jnp.max(logits, axis=0, keepdims=True)
    e = jnp.exp(logits - m)
    return jnp.sum(e[:layer + 1], axis=0, keepdims=True) / jnp.sum(e, axis=0, keepdims=True)


def _col_to_matrix(row):
    n = row.shape[-1]
    return jnp.broadcast_to(row, (n, n)).T


def _col_to_matrix_mxu(row):
    n = row.shape[-1]
    tile = jnp.concatenate([row, jnp.zeros((SUBLANES - 1, n), row.dtype)], axis=0)
    hi = _bf(tile)
    mid = _bf(tile - hi.astype(F32))
    lo = _bf(tile - hi.astype(F32) - mid.astype(F32))
    ones = jnp.ones((SUBLANES, n), BF16)
    return _dot_tn(hi, ones) + _dot_tn(mid, ones) + _dot_tn(lo, ones)


def _gdn_inputs(ext_ref, c, wconv, ab, alog, dtb, t_valid):
    if c > SUBLANES:
        full = ext_ref[...]
        acc = full * wconv[CONV_W - 1:CONV_W, :]
        for j in range(CONV_W - 1):
            acc = acc + pltpu.roll(full, CONV_W - 1 - j, axis=0) * wconv[j:j + 1, :]
        acc = acc[SUBLANES:SUBLANES + c]
    else:
        acc = ext_ref[pl.ds(SUBLANES - 3, c), :] * wconv[0:1, :]
        for j in range(1, CONV_W):
            acc = acc + ext_ref[pl.ds(SUBLANES - 3 + j, c), :] * wconv[j:j + 1, :]
    qkv = jax.nn.silu(acc)
    w = HEADS * HEAD_DIM
    qs, ks = [], []
    for h in range(HEADS):
        qh = qkv[:, h * HEAD_DIM:(h + 1) * HEAD_DIM]
        kh = qkv[:, w + h * HEAD_DIM:w + (h + 1) * HEAD_DIM]
        qs.append(qh * lax.rsqrt(jnp.sum(qh * qh, axis=-1, keepdims=True) + EPS) * (HEAD_DIM ** -0.5))
        ks.append(kh * lax.rsqrt(jnp.sum(kh * kh, axis=-1, keepdims=True) + EPS))
    v = qkv[:, 2 * w:3 * w]
    pre = ab + dtb
    softplus = jnp.maximum(pre, 0.0) + jnp.log(1.0 + jnp.exp(-jnp.abs(pre)))
    glog = -jnp.exp(alog) * softplus
    beta = jax.nn.sigmoid(ab)
    if t_valid is not None:
        valid = lax.broadcasted_iota(jnp.int32, ab.shape, 0) < t_valid
        glog = jnp.where(valid, glog, 0.0)
        beta = jnp.where(valid, beta, 0.0)
    return qs, ks, v, glog, beta


def _hier_tables(c):
    t = np.arange(c)
    u = np.arange(c)
    tri = (u[None, :] <= t[:, None]).astype(np.float32)
    rows, masks = [], []
    b = c // 2
    while b >= 1:
        ref = (t // (2 * b)) * 2 * b + b - 1
        rows.append(tri - (u[None, :] <= ref[:, None]).astype(np.float32))
        right = (t % (2 * b)) >= b
        left = (t % (2 * b)) < b
        same = (t[:, None] // (2 * b)) == (t[None, :] // (2 * b))
        masks.append((right[:, None] & left[None, :] & same).astype(np.float32))
        b //= 2
    rows.append(tri)
    return np.concatenate(rows, 0), np.stack(masks, 0)


def _hgrn_chunk(q, k, v, logf_hi, logf_lo, s, wall_ref, masks, c):
    nlev = len(masks)
    d = _dot(wall_ref[...], jnp.concatenate([logf_hi, logf_lo], axis=0))
    g = d[nlev * c:(nlev + 1) * c]
    gc = g[c - 1:c, :]
    qg = _bf(q * jnp.exp(g))
    kg = _bf(k * jnp.exp(gc - g))
    egc = jnp.exp(gc)
    vb = _bf(v)
    npair = HEADS // 2
    a = [None] * npair
    for l in range(nlev):
        e = jnp.exp(-jnp.abs(d[l * c:(l + 1) * c]))
        ql, kl = _bf(q * e), _bf(k * e)
        for pr in range(npair):
            pair = slice(2 * pr * HEAD_DIM, (2 * pr + 2) * HEAD_DIM)
            p = _dot_nt(ql[:, pair], _block_diag(kl[:, pair], 2))
            a[pr] = jnp.where(masks[l], p, 0.0) if a[pr] is None else jnp.where(masks[l], p, a[pr])
    qk = q * k
    outs, s_new = [], []
    for h in range(HEADS):
        sl = _head(h)
        a_h = a[h // 2][:, (h % 2) * c:(h % 2 + 1) * c]
        o = _dot(jnp.concatenate([qg[:, sl], _bf(a_h)], axis=1),
                 jnp.concatenate([_bf(s[h]), vb[:, sl]], axis=0))
        outs.append(o + jnp.sum(qk[:, sl], axis=-1, keepdims=True) * v[:, sl])
        s_new.append(_col_to_matrix(egc[:, sl]) * s[h] + _dot_tn(kg[:, sl], vb[:, sl]))
    return outs, s_new


def _hgrn_chunk_bounded(q, k, v, g, s, c):
    gc = g[c - 1:c, :]
    qg = _bf(q * jnp.exp(g))
    kn = _bf(k * jnp.exp(-g))
    kg = _bf(k * jnp.exp(gc - g))
    egc = jnp.exp(gc)
    vb = _bf(v)
    ti = lax.broadcasted_iota(jnp.int32, (c, HEADS * c), 0)
    si = lax.broadcasted_iota(jnp.int32, (c, HEADS * c), 1) & (c - 1)
    a = jnp.where(ti > si, _dot_nt(qg, _block_diag(kn, HEADS)), 0.0)
    v_rows = jnp.concatenate([vb[:, _head(h)] for h in range(HEADS)], axis=0)
    intra = _dot(_block_diag(_bf(a), HEADS), v_rows)
    qk = q * k
    outs, s_new = [], []
    for h in range(HEADS):
        sl = _head(h)
        outs.append(_dot(qg[:, sl], _bf(s[h])) + intra[h * c:(h + 1) * c]
                    + jnp.sum(qk[:, sl], axis=-1, keepdims=True) * v[:, sl])
        s_new.append(_col_to_matrix(egc[:, sl]) * s[h] + _dot_tn(kg[:, sl], vb[:, sl]))
    return outs, s_new


def _hgrn_kernel(q_ref, f_ref, i_ref, gate_ref, lbl_ref, gout_ref, wall_ref, mask_ref, tri_ref, s0_ref,
                 o_ref, sn_ref, s_scr, *, c, nch, layer):
    n = pl.program_id(1)

    @pl.when(n == 0)
    def _():
        s_scr[...] = s0_ref[0]

    lb = _forget_lower_bound(lbl_ref[...], layer)
    gout = gout_ref[...]
    f = lb + (1.0 - lb) * jax.nn.sigmoid(f_ref[0])
    hi, lo = _split2(jnp.log(f))
    k = 1.0 - f
    nlev = mask_ref.shape[0]

    def emit(r, outs):
        for h in range(HEADS):
            sl = _head(h)
            o_ref[0, r, sl] = _rms(outs[h], gout[:, sl]) * jax.nn.silu(gate_ref[0, r, sl])

    def run(chunk_len, chunk_fn):
        s = [s_scr[h] for h in range(HEADS)]
        for ch in range(nch * c // chunk_len):
            r = slice(ch * chunk_len, (ch + 1) * chunk_len)
            outs, s = chunk_fn(ch, r, s)
            emit(r, outs)
        for h in range(HEADS):
            s_scr[h] = s[h]

    cb = tri_ref.shape[0]
    half = lambda ch: slice(ch * cb, (ch + 1) * cb)
    gs = [_dot(tri_ref[...], jnp.concatenate([hi[half(ch)], lo[half(ch)]], axis=0))
          for ch in range(nch * c // cb)]
    gmin = functools.reduce(jnp.minimum, [jnp.min(g) for g in gs])
    bounded = gmin >= -SAFE_LOG_RANGE

    @pl.when(bounded)
    def _():
        run(cb, lambda ch, r, s: _hgrn_chunk_bounded(q_ref[0, r, :], k[r], i_ref[0, r, :], gs[ch], s, cb))

    @pl.when(jnp.logical_not(bounded))
    def _():
        masks = [mask_ref[l] != 0.0 for l in range(nlev)]
        run(c, lambda ch, r, s: _hgrn_chunk(q_ref[0, r, :], k[r], i_ref[0, r, :], hi[r], lo[r], s,
                                            wall_ref, masks, c))

    @pl.when(n == pl.num_programs(1) - 1)
    def _():
        sn_ref[0] = s_scr[...]


def _hgrn_long(hgp, lb_logits, g_out, s0, layer):
    bsz, t, _ = hgp.shape
    c = HG_CHUNK
    nch = HG_CHUNKS_PER_STEP
    tb = c * nch
    assert t % tb == 0 and c == HEAD_DIM and HEADS % 2 == 0
    w = HEADS * HEAD_DIM
    wall, masks = _hier_tables(c)
    wall = np.concatenate([wall, wall], axis=1)
    masks = np.concatenate([masks, masks], axis=2)
    cb = HG_BOUNDED_CHUNK
    tri = np.tril(np.ones((cb, cb), np.float32))
    tri = np.concatenate([tri, tri], axis=1)
    col = lambda j: pl.BlockSpec((1, tb, w), lambda b, n, j=j: (b, n, j))
    st = pl.BlockSpec((1, HEADS, HEAD_DIM, HEAD_DIM), lambda b, n: (b, 0, 0, 0))
    return pl.pallas_call(
        functools.partial(_hgrn_kernel, c=c, nch=nch, layer=layer),
        grid=(bsz, t // tb),
        in_specs=[col(0), col(1), col(2), col(3),
                  _resident(lb_logits.shape), _resident((1, w)),
                  _resident(wall.shape), _resident(masks.shape), _resident(tri.shape), st],
        out_specs=[pl.BlockSpec((1, tb, w), lambda b, n: (b, n, 0)), st],
        out_shape=[jax.ShapeDtypeStruct((bsz, t, w), F32),
                   jax.ShapeDtypeStruct(s0.shape, F32)],
        scratch_shapes=[pltpu.VMEM((HEADS, HEAD_DIM, HEAD_DIM), F32)],
        compiler_params=_cparams("arbitrary", "arbitrary"),
        name="hgrn_long",
    )(hgp, hgp, hgp, hgp, lb_logits, g_out.reshape(1, w),
      jnp.asarray(wall, BF16), jnp.asarray(masks, F32), jnp.asarray(tri, BF16), s0)


def _gdn_prepare(qs, ks, v, glog, beta, tri_bf, c, nch):
    nb = nch * HEADS
    wc = HEADS * c
    rows = lambda x, ch: x[ch * c:(ch + 1) * c]
    per = lambda f: jnp.stack([f(ch, h) for ch in range(nch) for h in range(HEADS)], axis=0)
    q = per(lambda ch, h: rows(qs[h], ch))
    k = per(lambda ch, h: rows(ks[h], ch))
    vv = per(lambda ch, h: rows(v, ch)[:, _head(h)])
    beta_b = per(lambda ch, h: jnp.broadcast_to(rows(beta, ch)[:, HEADS + h:HEADS + h + 1], (c, LANES)))
    lane = lax.broadcasted_iota(jnp.int32, (c, LANES), 1)
    g_small = None
    for ch in range(nch):
        part = jnp.where(lane < HEADS, rows(glog, ch), 0.0)
        part = part if ch == 0 else pltpu.roll(part, ch * HEADS, axis=1)
        g_small = part if g_small is None else g_small + part
    gcum = _dot_exact_lhs(tri_bf, g_small)
    gcum_t = jnp.concatenate([gcum, jnp.zeros((LANES - c, LANES), F32)], axis=0).T
    g = jnp.stack([jnp.broadcast_to(gcum[:, n:n + 1], (c, LANES)) for n in range(nb)], axis=0)
    packed = lambda f: jnp.stack([jnp.concatenate([f(ch * HEADS + h) for h in range(HEADS)], axis=1)
                                  for ch in range(nch)], axis=0)
    g_col = packed(lambda n: jnp.broadcast_to(gcum[:, n:n + 1], (c, c)))
    g_row = packed(lambda n: jnp.broadcast_to(gcum_t[n:n + 1, :c], (c, c)))
    ti = lax.broadcasted_iota(jnp.int32, (nch, c, wc), 1)
    si = lax.broadcasted_iota(jnp.int32, (nch, c, wc), 2) & (c - 1)
    decay = jnp.exp(jnp.minimum(g_col - g_row, 0.0))
    kb = k * beta_b
    heads_on_lanes = lambda x: jnp.stack(
        [jnp.concatenate([x[ch * HEADS + h] for h in range(HEADS)], axis=1) for ch in range(nch)], axis=0)
    kbq = _bf(jnp.concatenate([heads_on_lanes(kb), heads_on_lanes(q)], axis=1))
    kq = _bmm_nt(kbq, _block_diag(_bf(heads_on_lanes(k)), HEADS))
    m = jnp.where(ti > si, kq[:, :c] * decay, 0.0)
    aqk = jnp.where(ti > si, kq[:, c:] * decay, 0.0)
    qk_diag = jnp.sum(q * k, axis=-1, keepdims=True)
    eg = jnp.exp(g)
    rhs = jnp.concatenate([vv * beta_b, kb * eg], axis=-1)
    npow = -m
    poff = npow
    npow_bd = _block_diag(_bf(npow), HEADS)
    for _ in range(int(math.log2(c)) - 1):
        npow = _bmm(_bf(npow), npow_bd)
        npow_bd = _block_diag(_bf(npow), HEADS)
        poff = poff + npow + _bmm(_bf(poff), npow_bd)
    xoff = _bmm(_block_diag(_bf(poff), HEADS), _bf(rhs.reshape(nch, wc, 2 * HEAD_DIM)))
    x = rhs + xoff.reshape(nb, c, 2 * HEAD_DIM)
    u, w = x[:, :, :HEAD_DIM], x[:, :, HEAD_DIM:]
    gc = g[:, c - 1:c, :]
    return (u, _bf(jnp.concatenate([w, q * eg], axis=1)), _block_diag(_bf(aqk), HEADS), qk_diag,
            _bf(k * jnp.exp(gc - g)), jnp.exp(gc))


def _gdn_kernel(qkv_ref, z_ref, ab_ref, buf_ref, wconv_ref, alog_ref, dtb_ref, gout_ref, tri_ref,
                s0_ref, o_ref, sn_ref, s_scr, ext_scr, *, c, nch):
    n = pl.program_id(1)
    tb = c * nch

    @pl.when(n == 0)
    def _():
        s_scr[...] = s0_ref[0]
        ext_scr[0:SUBLANES - 3, :] = jnp.zeros((SUBLANES - 3, ext_scr.shape[1]), F32)
        ext_scr[SUBLANES - 3:SUBLANES, :] = buf_ref[0]

    @pl.when(n > 0)
    def _():
        ext_scr[0:SUBLANES, :] = ext_scr[tb:tb + SUBLANES, :]

    ext_scr[SUBLANES:SUBLANES + tb, :] = qkv_ref[0]
    qs, ks, v, glog, beta = _gdn_inputs(ext_scr, tb, wconv_ref[...], ab_ref[0], alog_ref[...],
                                        dtb_ref[...], None)
    u, wq, aqk_bd, qk_diag, kg, egc = _gdn_prepare(qs, ks, v, glog, beta, tri_ref[...], c, nch)
    gout = gout_ref[...]
    s = s_scr[...]
    for ch in range(nch):
        sel = slice(ch * HEADS, (ch + 1) * HEADS)
        ws_qs = _bmm(wq[sel], _bf(s))
        v_new = u[sel] - ws_qs[:, :c]
        intra = _dot(aqk_bd[ch], _bf(v_new.reshape(HEADS * c, HEAD_DIM)))
        o = ws_qs[:, c:] + intra.reshape(HEADS, c, HEAD_DIM) + qk_diag[sel] * v_new
        s = egc[sel] * s + _bmm_tn(kg[sel], _bf(v_new))
        for h in range(HEADS):
            sl = _head(h)
            o_ref[0, ch * c:(ch + 1) * c, sl] = _rms(o[h], gout) * jax.nn.silu(z_ref[0, ch * c:(ch + 1) * c, sl])
    s_scr[...] = s

    @pl.when(n == pl.num_programs(1) - 1)
    def _():
        sn_ref[0] = s


def _pad_lanes(x):
    return jnp.zeros((1, LANES), F32).at[0, :x.shape[0]].set(x.astype(F32))


def _gdn_long(gqkv, gz, ab, conv_buf, w_conv, a_log, dt_bias, g_out, s0):
    bsz, t, wq = gqkv.shape
    c = GDN_CHUNK
    nch = GDN_CHUNKS_PER_STEP
    tb = c * nch
    assert t % tb == 0
    w = HEADS * HEAD_DIM
    tri = np.tril(np.ones((c, c), np.float32))
    st = pl.BlockSpec((1, HEADS, HEAD_DIM, HEAD_DIM), lambda b, n: (b, 0, 0, 0))
    return pl.pallas_call(
        functools.partial(_gdn_kernel, c=c, nch=nch),
        grid=(bsz, t // tb),
        in_specs=[pl.BlockSpec((1, tb, wq), lambda b, n: (b, n, 0)),
                  pl.BlockSpec((1, tb, w), lambda b, n: (b, n, 0)),
                  pl.BlockSpec((1, tb, LANES), lambda b, n: (b, n, 0)),
                  pl.BlockSpec((1, CONV_W - 1, wq), lambda b, n: (b, 0, 0)),
                  _resident((CONV_W, wq)), _resident((1, LANES)), _resident((1, LANES)),
                  _resident((1, HEAD_DIM)), _resident((c, c)), st],
        out_specs=[pl.BlockSpec((1, tb, w), lambda b, n: (b, n, 0)), st],
        out_shape=[jax.ShapeDtypeStruct((bsz, t, w), F32),
                   jax.ShapeDtypeStruct((bsz, HEADS, HEAD_DIM, HEAD_DIM), F32)],
        scratch_shapes=[pltpu.VMEM((HEADS, HEAD_DIM, HEAD_DIM), F32),
                        pltpu.VMEM((tb + SUBLANES, wq), F32)],
        compiler_params=_cparams("arbitrary", "arbitrary"),
        name="gdn_long",
    )(gqkv, gz, ab, conv_buf, w_conv,
      _pad_lanes(a_log), _pad_lanes(dt_bias),
      g_out.reshape(1, HEAD_DIM), jnp.asarray(tri, BF16), s0)


def _shift_rows(x, d):
    return x if d == 0 else pltpu.roll(x, d, axis=0)


def _cumsum_rows8(x, rows):
    for d in (1, 2, 4):
        x = x + jnp.where(rows >= d, pltpu.roll(x, d, axis=0), 0.0)
    return x


def _hgrn_short(q, hf, v, lb, s, rows, t_valid):
    f = lb + (1.0 - lb) * jax.nn.sigmoid(hf)
    valid = rows < t_valid
    logf = jnp.where(valid, jnp.log(f), 0.0)
    k = jnp.where(valid, 1.0 - f, 0.0)
    g = _cumsum_rows8(logf, rows)
    gc = g[SUBLANES - 1:SUBLANES, :]
    o = _dot(_bf(q * jnp.exp(g)), _bf(s))
    for d in range(t_valid):
        ok = rows >= d
        dec = jnp.exp(jnp.where(ok, g - _shift_rows(g, d), 0.0))
        a = jnp.sum(jnp.where(ok, q * _shift_rows(k, d) * dec, 0.0), axis=-1, keepdims=True)
        o = o + a * _shift_rows(v, d)
    s_new = _col_to_matrix_mxu(jnp.exp(gc)) * s + _dot_tn(_bf(k * jnp.exp(gc - g)), _bf(v))
    return o, s_new


def _gdn_short(q, k, v, g_b, beta_b, s, rows, t_valid):
    g = _cumsum_rows8(g_b, rows)
    gc = g[SUBLANES - 1:SUBLANES, :]
    eg = jnp.exp(g)
    kb = k * beta_b
    dec = [None] + [jnp.exp(jnp.where(rows >= d, g - _shift_rows(g, d), 0.0)) for d in range(1, t_valid)]
    mcol = [None] + [jnp.where(rows >= d, jnp.sum(kb * _shift_rows(k, d), axis=-1, keepdims=True) * dec[d], 0.0)
                     for d in range(1, t_valid)]
    ru, rw = v * beta_b, kb * eg
    xu, xw = ru, rw
    for _ in range(t_valid - 1):
        nu, nw = ru, rw
        for d in range(1, t_valid):
            nu = nu - mcol[d] * _shift_rows(xu, d)
            nw = nw - mcol[d] * _shift_rows(xw, d)
        xu, xw = nu, nw
    ws_qs = _dot(_bf(jnp.concatenate([xw, q * eg], axis=0)), _bf(s))
    v_new = xu - ws_qs[:SUBLANES]
    o = ws_qs[SUBLANES:] + jnp.sum(q * k, axis=-1, keepdims=True) * v_new
    for d in range(1, t_valid):
        a = jnp.where(rows >= d, jnp.sum(q * _shift_rows(k, d), axis=-1, keepdims=True) * dec[d], 0.0)
        o = o + a * _shift_rows(v_new, d)
    s_new = jnp.exp(gc) * s + _dot_tn(_bf(k * jnp.exp(gc - g)), _bf(v_new))
    return o, s_new


def _short_kernel(hq_ref, hf_ref, hi_ref, hgate_ref, qkv_ref, z_ref, ab_ref, buf_ref,
                  lbl_ref, hgout_ref, wconv_ref, alog_ref, dtb_ref, ggout_ref, shg0_ref, sgd0_ref,
                  ohg_ref, ogd_ref, shg_ref, sgd_ref, ext_scr, *, t_valid, layer):
    bb = hq_ref.shape[0]
    rows = lax.broadcasted_iota(jnp.int32, (SUBLANES, LANES), 0)
    lb = _forget_lower_bound(lbl_ref[...], layer)
    hgout = hgout_ref[...]
    ggout = ggout_ref[...]
    wconv = wconv_ref[...]
    alog = alog_ref[...]
    dtb = dtb_ref[...]

    def tile8(x):
        if t_valid == SUBLANES:
            return x
        return jnp.concatenate([x, jnp.zeros((SUBLANES - t_valid, x.shape[1]), x.dtype)], axis=0)

    def per_seq(i, ext):
        ext[SUBLANES - 3:SUBLANES, :] = buf_ref[i]
        ext[SUBLANES:2 * SUBLANES, :] = tile8(qkv_ref[i])
        qs, ks, v, glog, beta = _gdn_inputs(ext, SUBLANES, wconv, tile8(ab_ref[i]), alog, dtb, t_valid)
        for h in range(HEADS):
            sl = _head(h)
            o, s_new = _hgrn_short(tile8(hq_ref[i, :, sl]), tile8(hf_ref[i, :, sl]), tile8(hi_ref[i, :, sl]),
                                   lb[:, sl], shg0_ref[i, h], rows, t_valid)
            shg_ref[i, h] = s_new
            ohg_ref[i, :, sl] = (_rms(o, hgout[:, sl])[:t_valid]) * jax.nn.silu(hgate_ref[i, :, sl])
            g_b = jnp.broadcast_to(glog[:, h:h + 1], (SUBLANES, LANES))
            beta_b = jnp.broadcast_to(beta[:, HEADS + h:HEADS + h + 1], (SUBLANES, LANES))
            o, s_new = _gdn_short(qs[h], ks[h], v[:, sl], g_b, beta_b, sgd0_ref[i, h], rows, t_valid)
            sgd_ref[i, h] = s_new
            ogd_ref[i, :, sl] = (_rms(o, ggout)[:t_valid]) * jax.nn.silu(z_ref[i, :, sl])

    def per_group(j, carry):
        for u in range(DECODE_INTERLEAVE):
            per_seq(j * DECODE_INTERLEAVE + u, ext_scr.at[u])
        return carry

    lax.fori_loop(0, bb // DECODE_INTERLEAVE, per_group, 0)


def _mixers_short(hgp, gqkv, gz, ab, conv_buf, lb_logits, hg_out, w_conv, a_log, dt_bias, gdn_out,
                  s_hg0, s_gdn0, t_valid, layer):
    bsz, t, wq = gqkv.shape
    assert t == t_valid and CONV_W - 1 <= t_valid <= SUBLANES
    bb = DECODE_BLOCK
    assert bsz % bb == 0 and bb % DECODE_INTERLEAVE == 0
    w = HEADS * HEAD_DIM
    col = lambda j: pl.BlockSpec((bb, t, w), lambda b, j=j: (b, 0, j))
    tok = lambda width: pl.BlockSpec((bb, t, width), lambda b: (b, 0, 0))
    st = pl.BlockSpec((bb, HEADS, HEAD_DIM, HEAD_DIM), lambda b: (b, 0, 0, 0))
    return pl.pallas_call(
        functools.partial(_short_kernel, t_valid=t_valid, layer=layer),
        grid=(bsz // bb,),
        in_specs=[col(0), col(1), col(2), col(3), tok(wq), tok(w), tok(LANES),
                  pl.BlockSpec((bb, CONV_W - 1, wq), lambda b: (b, 0, 0)),
                  _resident(lb_logits.shape), _resident((1, w)), _resident((CONV_W, wq)),
                  _resident((1, LANES)), _resident((1, LANES)), _resident((1, HEAD_DIM)), st, st],
        out_specs=[tok(w), tok(w), st, st],
        out_shape=[jax.ShapeDtypeStruct((bsz, t, w), F32), jax.ShapeDtypeStruct((bsz, t, w), F32),
                   jax.ShapeDtypeStruct(s_hg0.shape, F32), jax.ShapeDtypeStruct(s_gdn0.shape, F32)],
        scratch_shapes=[pltpu.VMEM((DECODE_INTERLEAVE, 2 * SUBLANES, wq), F32)],
        compiler_params=_cparams("arbitrary"),
        name="mixers_short",
    )(hgp, hgp, hgp, hgp, gqkv, gz, ab, conv_buf, lb_logits, hg_out.reshape(1, w), w_conv,
      _pad_lanes(a_log), _pad_lanes(dt_bias), gdn_out.reshape(1, HEAD_DIM), s_hg0, s_gdn0)


def _mem_attn_kernel(q_ref, k_ref, v_ref, o_ref):
    for h in range(HEADS):
        sl = _head(h)
        s = _bmm_nt(_bf(q_ref[:, :, sl]), _bf(k_ref[:, :, sl])) * (HEAD_DIM ** -0.5)
        e = jnp.exp(s - jnp.max(s, axis=-1, keepdims=True))
        o_ref[:, :, sl] = _bmm(_bf(e), _bf(v_ref[:, :, sl])) / jnp.sum(e, axis=-1, keepdims=True)


def _mem_attn(mq, mem_k, mem_v, bb, tq):
    bsz, t, w = mq.shape
    m = mem_k.shape[1]
    assert bsz % bb == 0 and t % tq == 0
    kv = pl.BlockSpec((bb, m, w), lambda b, n: (b, 0, 0))
    return pl.pallas_call(
        _mem_attn_kernel,
        grid=(bsz // bb, t // tq),
        in_specs=[pl.BlockSpec((bb, tq, w), lambda b, n: (b, n, 0)), kv, kv],
        out_specs=pl.BlockSpec((bb, tq, w), lambda b, n: (b, n, 0)),
        out_shape=jax.ShapeDtypeStruct((bsz, t, w), F32),
        compiler_params=_cparams("arbitrary", "arbitrary"),
        name="mem_attn",
    )(mq, mem_k, mem_v)


def _mem_attn_cache_kernel(q_ref, k_hbm, v_hbm, o_ref, kbuf, vbuf, sem):
    i = pl.program_id(0)
    bb = q_ref.shape[0]

    def copies(step, slot):
        out = []
        for h in range(HEADS):
            for j, (src, dst) in enumerate(((k_hbm, kbuf), (v_hbm, vbuf))):
                out.append(pltpu.make_async_copy(src.at[pl.ds(step * bb, bb), :, h, :],
                                                 dst.at[slot, h], sem.at[j, slot, h]))
        return out

    slot = i % 2

    @pl.when(i == 0)
    def _():
        for cp in copies(0, 0):
            cp.start()

    @pl.when(i + 1 < pl.num_programs(0))
    def _():
        for cp in copies(i + 1, 1 - slot):
            cp.start()

    for cp in copies(i, slot):
        cp.wait()
    t = q_ref.shape[1]
    for h in range(HEADS):
        sl = _head(h)
        q = q_ref[:, :, sl]
        if t < SUBLANES:
            q = jnp.concatenate([q, jnp.zeros((bb, SUBLANES - t, HEAD_DIM), q.dtype)], axis=1)
        s = _bmm_nt(_bf(q), _bf(kbuf[slot, h])) * (HEAD_DIM ** -0.5)
        e = jnp.exp(s - jnp.max(s, axis=-1, keepdims=True))
        o = _bmm(_bf(e), _bf(vbuf[slot, h])) / jnp.sum(e, axis=-1, keepdims=True)
        o_ref[:, :, sl] = o[:, :t]


def _mem_attn_cache(mq, cache_k, cache_v, bb):
    bsz, t, w = mq.shape
    m = cache_k.shape[1]
    assert bsz % bb == 0 and cache_k.shape == (bsz, m, HEADS, HEAD_DIM)
    return pl.pallas_call(
        _mem_attn_cache_kernel,
        grid=(bsz // bb,),
        in_specs=[pl.BlockSpec((bb, t, w), lambda b: (b, 0, 0)),
                  pl.BlockSpec(memory_space=pl.ANY), pl.BlockSpec(memory_space=pl.ANY)],
        out_specs=pl.BlockSpec((bb, t, w), lambda b: (b, 0, 0)),
        out_shape=jax.ShapeDtypeStruct((bsz, t, w), F32),
        scratch_shapes=[pltpu.VMEM((2, HEADS, bb, m, HEAD_DIM), F32),
                        pltpu.VMEM((2, HEADS, bb, m, HEAD_DIM), F32),
                        pltpu.SemaphoreType.DMA((2, 2, HEADS))],
        compiler_params=_cparams("arbitrary"),
        name="mem_attn_cache",
    )(mq, cache_k, cache_v)


def _merge_ffn_kernel(x_ref, ohg_ref, ogd_ref, omem_ref, gpre_ref, wgates_ref,
                      wbh_ref, wbg_ref, wbm_ref, wout_ref, gpm_ref, gpf_ref,
                      wffn_ref, wdown_ref, gpo_ref, y_ref):
    d = x_ref.shape[-1]
    x = x_ref[...]
    xn = _bf(_rms(x, gpre_ref[...]))
    merged = None
    for j, (o_ref, wb_ref) in enumerate(((ohg_ref, wbh_ref), (ogd_ref, wbg_ref), (omem_ref, wbm_ref))):
        gate = jax.nn.sigmoid(_dot(xn, wgates_ref[:, j * d:(j + 1) * d]))
        term = gate * _dot(_bf(o_ref[...]), wb_ref[...])
        merged = term if merged is None else merged + term
    h = x + _rms(_dot(_bf(merged), wout_ref[...]), gpm_ref[...])
    hn = _bf(_rms(h, gpf_ref[...]))
    hidden = wdown_ref.shape[0]
    ff = None
    for c0 in range(0, hidden, FFN_CHUNK):
        a = (jax.nn.silu(_dot(hn, wffn_ref[:, c0:c0 + FFN_CHUNK]))
             * _dot(hn, wffn_ref[:, hidden + c0:hidden + c0 + FFN_CHUNK]))
        p = _dot(_bf(a), wdown_ref[c0:c0 + FFN_CHUNK, :])
        ff = p if ff is None else ff + p
    y_ref[...] = h + _rms(ff, gpo_ref[...])


def _merge_ffn(x2d, ohg, ogd, omem, gpre, wgates, wbh, wbg, wbm, wout, gpm, gpf, wffn, wdown, gpo):
    n, d = x2d.shape
    tm = min(TOKEN_TILE, n)
    hidden = wdown.shape[0]
    assert n % tm == 0 and hidden % FFN_CHUNK == 0 and wffn.shape[1] == 2 * hidden
    assert wgates.shape == (d, N_BRANCH * d)
    tok = lambda width: pl.BlockSpec((tm, width), lambda i: (i, 0))
    row = lambda a: a.reshape(1, d)
    return pl.pallas_call(
        _merge_ffn_kernel,
        grid=(n // tm,),
        in_specs=[tok(d), tok(ohg.shape[1]), tok(ogd.shape[1]), tok(omem.shape[1]),
                  _resident((1, d)), _resident(wgates.shape),
                  _resident(wbh.shape), _resident(wbg.shape), _resident(wbm.shape), _resident(wout.shape),
                  _resident((1, d)), _resident((1, d)),
                  _resident(wffn.shape), _resident(wdown.shape), _resident((1, d))],
        out_specs=tok(d),
        out_shape=jax.ShapeDtypeStruct((n, d), F32),
        compiler_params=_cparams("arbitrary"),
        name="merge_ffn",
    )(x2d, ohg, ogd, omem, row(gpre), wgates, wbh, wbg, wbm, wout, row(gpm), row(gpf), wffn, wdown, row(gpo))


def _pack_w_in_kernel(w_ref, mix_ref, gates_ref):
    w = HEADS * HEAD_DIM
    c_ab = 8 * w
    c_mq = c_ab + 2 * HEADS
    c_gt = c_mq + w
    x = w_ref[0]
    mix_ref[:, 0:c_ab] = _bf(x[:, 0:c_ab])
    mix_ref[:, c_ab:c_ab + w] = _bf(x[:, c_mq:c_gt])
    pad = jnp.zeros((x.shape[0], LANES - 2 * HEADS), x.dtype)
    mix_ref[:, c_ab + w:c_ab + w + LANES] = _bf(jnp.concatenate([x[:, c_ab:c_mq], pad], axis=1))
    gates_ref[...] = _bf(x[:, c_gt:])


def _pack_w_in(w_in, layer):
    _, d, total = w_in.shape
    w = HEADS * HEAD_DIM
    widths = (4 * w, 3 * w, w, w, LANES)
    n_gates = total - (9 * w + 2 * HEADS)
    rb = LANES
    assert d % rb == 0 and n_gates % LANES == 0
    mixers, gates = pl.pallas_call(
        _pack_w_in_kernel,
        grid=(d // rb,),
        in_specs=[pl.BlockSpec((1, rb, total), lambda i: (layer, i, 0))],
        out_specs=[pl.BlockSpec((rb, sum(widths)), lambda i: (i, 0)),
                   pl.BlockSpec((rb, n_gates), lambda i: (i, 0))],
        out_shape=[jax.ShapeDtypeStruct((d, sum(widths)), BF16), jax.ShapeDtypeStruct((d, n_gates), BF16)],
        compiler_params=_cparams("arbitrary"),
        name="pack_w_in",
    )(w_in)
    return mixers, widths, gates


def _of_layer(a, l):
    return a.reshape(a.shape[1:]) if a.shape[0] == 1 else a[l]


def _layer(x, mem_k, mem_v, conv_buf, s_hg, s_gdn, layer, lb_logits, p):
    bsz, t, d = x.shape
    t_valid = t
    short = t <= SUBLANES
    x2d = x.reshape(bsz * t, d)
    w = HEADS * HEAD_DIM

    hgp, gqkv, gz, mq, ab = _norm_proj(x2d, p["g_pre_mix"], p["w_in"], p["w_in_widths"], "in_proj")
    as3d = lambda a: a.reshape(bsz, t, a.shape[-1])
    gqkv3 = as3d(gqkv)
    if short:
        o_hg, o_gdn, s_hg_new, s_gdn_new = _mixers_short(
            as3d(hgp), gqkv3, as3d(gz), as3d(ab), conv_buf, lb_logits, p["g_hg_out"], p["w_conv"],
            p["a_log"], p["dt_bias"], p["g_gdn_out"], s_hg, s_gdn, t_valid, layer)
        o_mem = _mem_attn_cache(as3d(mq), mem_k, mem_v, ATTN_DECODE_BLOCK)
    else:
        o_hg, s_hg_new = _hgrn_long(as3d(hgp), lb_logits, p["g_hg_out"], s_hg, layer)
        o_gdn, s_gdn_new = _gdn_long(gqkv3, as3d(gz), as3d(ab), conv_buf, p["w_conv"], p["a_log"],
                                     p["dt_bias"], p["g_gdn_out"], s_gdn)
        o_mem = _mem_attn(as3d(mq), mem_k, mem_v, 1, min(ATTN_TILE, t))
    assert t_valid >= CONV_W - 1
    conv_new = gqkv3[:, t_valid - (CONV_W - 1):t_valid, :]

    flat = lambda a: a.reshape(bsz * t, w)
    y = _merge_ffn(x2d, flat(o_hg), flat(o_gdn), flat(o_mem), p["g_pre_mix"], p["w_gates"],
                   p["w_br_hg"], p["w_br_gdn"], p["w_br_mem"], p["w_out"], p["g_post_mix"],
                   p["g_pre_ffn"], p["w_ffn_in"], p["w_ffn_out"], p["g_post_ffn"])
    return y.reshape(bsz, t, d), conv_new, s_hg_new, s_gdn_new


def kernel(x_prompt, x_sample, mem_prompt, cache_mem_k, cache_mem_v, state_hgrn, state_gdn, state_gdn_conv, hg_lb_logits, g_pre_mix, w_in, w_conv, a_log, dt_bias, g_hg_out, g_gdn_out, g_mem, w_mem_kv, w_br_hg, w_br_gdn, w_br_mem, w_out, g_post_mix, g_pre_ffn, w_ffn_in, w_ffn_out, g_post_ffn):
    depth = w_in.shape[0]
    bp, _, d = x_prompt.shape
    m = mem_prompt.shape[1]
    w = HEADS * HEAD_DIM
    yp, ys = x_prompt, x_sample
    outs = [[] for _ in range(8)]
    for l in range(depth):
        w_in_packed, widths, w_gates = _pack_w_in(w_in, l)
        p = dict(g_pre_mix=g_pre_mix[l], w_in=w_in_packed, w_in_widths=widths, w_gates=w_gates,
                 w_conv=w_conv[l],
                 a_log=a_log[l], dt_bias=dt_bias[l], g_hg_out=g_hg_out[l], g_gdn_out=g_gdn_out[l],
                 w_br_hg=_bf(w_br_hg[l]), w_br_gdn=_bf(w_br_gdn[l]), w_br_mem=_bf(w_br_mem[l]),
                 w_out=_bf(w_out[l]), g_post_mix=g_post_mix[l], g_pre_ffn=g_pre_ffn[l],
                 w_ffn_in=_bf(w_ffn_in[l]), w_ffn_out=_bf(w_ffn_out[l]), g_post_ffn=g_post_ffn[l])
        mk, mv = _norm_proj(mem_prompt.reshape(bp * m, d), g_mem[l], _bf(w_mem_kv[l]), (w, w), "mem_kv")
        mk, mv = mk.reshape(bp, m, w), mv.reshape(bp, m, w)
        zeros_state = jnp.zeros((bp, HEADS, HEAD_DIM, HEAD_DIM), F32)
        yp, cb, sh, sg = _layer(yp, mk, mv, jnp.zeros((bp, CONV_W - 1, 3 * w), F32),
                                zeros_state, zeros_state, l, hg_lb_logits, p)
        bs = x_sample.shape[0]
        ys, cb2, sh2, sg2 = _layer(ys, _of_layer(cache_mem_k, l), _of_layer(cache_mem_v, l),
                                   _of_layer(state_gdn_conv, l), _of_layer(state_hgrn, l),
                                   _of_layer(state_gdn, l), l, hg_lb_logits, p)
        for lst, val in zip(outs, (mk.reshape(bp, m, HEADS, HEAD_DIM), mv.reshape(bp, m, HEADS, HEAD_DIM),
                                   sh, sg, cb, sh2, sg2, cb2)):
            lst.append(val)
    return (yp, ys) + tuple(o[0].reshape((1,) + o[0].shape) if depth == 1 else jnp.stack(o) for o in outs)
```

```python
import functools
import math

import numpy as np
import jax
import jax.numpy as jnp
from jax import lax
from jax.experimental import pallas as pl
from jax.experimental.pallas import tpu as pltpu

F32 = jnp.float32
BF16 = jnp.bfloat16

EPS = 1e-6
HEADS = 4
HEAD_DIM = 128
CONV_W = 4
N_BRANCH = 3

LANES = 128
SUBLANES = 8
VMEM_LIMIT = 56 * 1024 * 1024

SAFE_LOG_RANGE = 70.0
HG_CHUNK = 128
HG_BOUNDED_CHUNK = 64
HG_CHUNKS_PER_STEP = 8
GDN_CHUNK = 64
GDN_CHUNKS_PER_STEP = 16
TOKEN_TILE = 512
ATTN_TILE = 512
FFN_CHUNK = 256
DECODE_BLOCK = 8
DECODE_INTERLEAVE = 2
ATTN_DECODE_BLOCK = 8


def _cparams(*sem):
    return pltpu.CompilerParams(dimension_semantics=sem, vmem_limit_bytes=VMEM_LIMIT)


def _resident(shape):
    nd = len(shape)
    return pl.BlockSpec(shape, lambda *_: (0,) * nd, pipeline_mode=pl.Buffered(1))


def _bf(x):
    return x.astype(BF16)


def _dot(a, b):
    return jnp.dot(a, b, preferred_element_type=F32)


def _dot_nt(a, b):
    return lax.dot_general(a, b, (((1,), (1,)), ((), ())), preferred_element_type=F32)


def _dot_tn(a, b):
    return lax.dot_general(a, b, (((0,), (0,)), ((), ())), preferred_element_type=F32)


def _bmm(a, b):
    return jnp.einsum("nij,njk->nik", a, b, preferred_element_type=F32)


def _bmm_nt(a, b):
    return jnp.einsum("nid,njd->nij", a, b, preferred_element_type=F32)


def _bmm_tn(a, b):
    return jnp.einsum("nci,ncj->nij", a, b, preferred_element_type=F32)


def _split2(x):
    hi = _bf(x)
    lo = _bf(x - hi.astype(F32))
    return hi, lo


def _dot_exact_lhs(w_bf16, x):
    hi, lo = _split2(x)
    return _dot(w_bf16, hi) + _dot(w_bf16, lo)


def _rms(x, g):
    return x * lax.rsqrt(jnp.mean(x * x, axis=-1, keepdims=True) + EPS) * g


def _head(h):
    return slice(h * HEAD_DIM, (h + 1) * HEAD_DIM)


def _block_diag(x, nblk):
    wblk = x.shape[-1] // nblk
    if wblk % LANES == 0:
        zeros = lambda n: [jnp.zeros(x.shape[:-1] + (n * wblk,), x.dtype)] if n else []
        parts = [jnp.concatenate(zeros(j) + [x[..., j * wblk:(j + 1) * wblk]] + zeros(nblk - 1 - j), axis=-1)
                 for j in range(nblk)]
    else:
        lane = lax.broadcasted_iota(jnp.int32, x.shape, x.ndim - 1)
        parts = [jnp.where((lane >= j * wblk) & (lane < (j + 1) * wblk), x, jnp.zeros_like(x))
                 for j in range(nblk)]
    return jnp.concatenate(parts, axis=-2)


def _norm_proj_kernel(x_ref, g_ref, w_ref, *out_refs):
    xn = _bf(_rms(x_ref[...], g_ref[...]))
    off = 0
    for o_ref in out_refs:
        width = o_ref.shape[-1]
        for c0 in range(0, width, 512):
            cw = min(512, width - c0)
            o_ref[:, c0:c0 + cw] = _dot(xn, w_ref[:, off + c0:off + c0 + cw])
        off += width


def _norm_proj(x2d, g, w_bf16, widths, name):
    n, d = x2d.shape
    tm = min(TOKEN_TILE, n)
    assert n % tm == 0 and sum(widths) == w_bf16.shape[1]
    return pl.pallas_call(
        _norm_proj_kernel,
        grid=(n // tm,),
        in_specs=[pl.BlockSpec((tm, d), lambda i: (i, 0)),
                  _resident((1, d)),
                  _resident(w_bf16.shape)],
        out_specs=[pl.BlockSpec((tm, w), lambda i: (i, 0)) for w in widths],
        out_shape=[jax.ShapeDtypeStruct((n, w), F32) for w in widths],
        compiler_params=_cparams("arbitrary"),
        name=name,
    )(x2d, g.reshape(1, d), w_bf16)


def _forget_lower_bound(logits, layer):
    m = jnp.max(logits, axis=0, keepdims=True)
    e = jnp.exp(logits - m)
    return jnp.sum(e[:layer + 1], axis=0, keepdims=True) / jnp.sum(e, axis=0, keepdims=True)


def _col_to_matrix_mxu(row):
    n = row.shape[-1]
    tile = jnp.concatenate([row, jnp.zeros((SUBLANES - 1, n), row.dtype)], axis=0)
    hi = _bf(tile)
    mid = _bf(tile - hi.astype(F32))
    lo = _bf(tile - hi.astype(F32) - mid.astype(F32))
    ones = jnp.ones((SUBLANES, n), BF16)
    return _dot_tn(hi, ones) + _dot_tn(mid, ones) + _dot_tn(lo, ones)


def _gdn_inputs(ext_ref, c, wconv, ab, alog, dtb, t_valid):
    if c > SUBLANES:
        full = ext_ref[...]
        acc = full * wconv[CONV_W - 1:CONV_W, :]
        for j in range(CONV_W - 1):
            acc = acc + pltpu.roll(full, CONV_W - 1 - j, axis=0) * wconv[j:j + 1, :]
        acc = acc[SUBLANES:SUBLANES + c]
    else:
        acc = ext_ref[pl.ds(SUBLANES - 3, c), :] * wconv[0:1, :]
        for j in range(1, CONV_W):
            acc = acc + ext_ref[pl.ds(SUBLANES - 3 + j, c), :] * wconv[j:j + 1, :]
    qkv = jax.nn.silu(acc)
    w = HEADS * HEAD_DIM
    qs, ks = [], []
    for h in range(HEADS):
        qh = qkv[:, h * HEAD_DIM:(h + 1) * HEAD_DIM]
        kh = qkv[:, w + h * HEAD_DIM:w + (h + 1) * HEAD_DIM]
        qs.append(qh * lax.rsqrt(jnp.sum(qh * qh, axis=-1, keepdims=True) + EPS) * (HEAD_DIM ** -0.5))
        ks.append(kh * lax.rsqrt(jnp.sum(kh * kh, axis=-1, keepdims=True) + EPS))
    v = qkv[:, 2 * w:3 * w]
    pre = ab + dtb
    softplus = jnp.maximum(pre, 0.0) + jnp.log(1.0 + jnp.exp(-jnp.abs(pre)))
    glog = -jnp.exp(alog) * softplus
    beta = jax.nn.sigmoid(ab)
    if t_valid is not None:
        valid = lax.broadcasted_iota(jnp.int32, ab.shape, 0) < t_valid
        glog = jnp.where(valid, glog, 0.0)
        beta = jnp.where(valid, beta, 0.0)
    return qs, ks, v, glog, beta


def _hier_tables(c):
    t = np.arange(c)
    u = np.arange(c)
    tri = (u[None, :] <= t[:, None]).astype(np.float32)
    rows, masks = [], []
    b = c // 2
    while b >= 1:
        ref = (t // (2 * b)) * 2 * b + b - 1
        rows.append(tri - (u[None, :] <= ref[:, None]).astype(np.float32))
        right = (t % (2 * b)) >= b
        left = (t % (2 * b)) < b
        same = (t[:, None] // (2 * b)) == (t[None, :] // (2 * b))
        masks.append((right[:, None] & left[None, :] & same).astype(np.float32))
        b //= 2
    rows.append(tri)
    return np.concatenate(rows, 0), np.stack(masks, 0)


def _hgrn_chunk(q, k, v, logf_hi, logf_lo, s, wall_ref, masks, c):
    nlev = len(masks)
    d = _dot(wall_ref[...], jnp.concatenate([logf_hi, logf_lo], axis=0))
    g = d[nlev * c:(nlev + 1) * c]
    gc = g[c - 1:c, :]
    qg = _bf(q * jnp.exp(g))
    kg = _bf(k * jnp.exp(gc - g))
    egc = jnp.exp(gc)
    vb = _bf(v)
    npair = HEADS // 2
    a = [None] * npair
    for l in range(nlev):
        e = jnp.exp(-jnp.abs(d[l * c:(l + 1) * c]))
        ql, kl = _bf(q * e), _bf(k * e)
        for pr in range(npair):
            pair = slice(2 * pr * HEAD_DIM, (2 * pr + 2) * HEAD_DIM)
            p = _dot_nt(ql[:, pair], _block_diag(kl[:, pair], 2))
            a[pr] = jnp.where(masks[l], p, 0.0) if a[pr] is None else jnp.where(masks[l], p, a[pr])
    qk = q * k
    outs, s_new = [], []
    for h in range(HEADS):
        sl = _head(h)
        a_h = a[h // 2][:, (h % 2) * c:(h % 2 + 1) * c]
        o = _dot_nt(qg[:, sl], _bf(s[h])) + _dot(_bf(a_h), vb[:, sl])
        outs.append(o + jnp.sum(qk[:, sl], axis=-1, keepdims=True) * v[:, sl])
        s_new.append(egc[:, sl] * s[h] + _dot_tn(vb[:, sl], kg[:, sl]))
    return outs, s_new


def _hgrn_chunk_bounded(q, k, v, g, s, c):
    gc = g[c - 1:c, :]
    qg = _bf(q * jnp.exp(g))
    kn = _bf(k * jnp.exp(-g))
    kg = _bf(k * jnp.exp(gc - g))
    egc = jnp.exp(gc)
    vb = _bf(v)
    ti = lax.broadcasted_iota(jnp.int32, (c, HEADS * c), 0)
    si = lax.broadcasted_iota(jnp.int32, (c, HEADS * c), 1) & (c - 1)
    a = jnp.where(ti > si, _dot_nt(qg, _block_diag(kn, HEADS)), 0.0)
    v_rows = jnp.concatenate([vb[:, _head(h)] for h in range(HEADS)], axis=0)
    intra = _dot(_block_diag(_bf(a), HEADS), v_rows)
    qk = q * k
    outs, s_new = [], []
    for h in range(HEADS):
        sl = _head(h)
        outs.append(_dot_nt(qg[:, sl], _bf(s[h])) + intra[h * c:(h + 1) * c]
                    + jnp.sum(qk[:, sl], axis=-1, keepdims=True) * v[:, sl])
        s_new.append(egc[:, sl] * s[h] + _dot_tn(vb[:, sl], kg[:, sl]))
    return outs, s_new


def _hgrn_kernel(q_ref, f_ref, i_ref, gate_ref, lbl_ref, gout_ref, wall_ref, mask_ref, tri_ref, s0_ref,
                 o_ref, sn_ref, s_scr, *, c, nch, layer):
    n = pl.program_id(1)

    @pl.when(n == 0)
    def _():
        for h in range(HEADS):
            s_scr[h] = s0_ref[0, h].T

    lb = _forget_lower_bound(lbl_ref[...], layer)
    gout = gout_ref[...]
    f = lb + (1.0 - lb) * jax.nn.sigmoid(f_ref[0])
    hi, lo = _split2(jnp.log(f))
    k = 1.0 - f
    nlev = mask_ref.shape[0]

    def emit(r, outs):
        for h in range(HEADS):
            sl = _head(h)
            o_ref[0, r, sl] = _rms(outs[h], gout[:, sl]) * jax.nn.silu(gate_ref[0, r, sl])

    def run(chunk_len, chunk_fn):
        s = [s_scr[h] for h in range(HEADS)]
        for ch in range(nch * c // chunk_len):
            r = slice(ch * chunk_len, (ch + 1) * chunk_len)
            outs, s = chunk_fn(ch, r, s)
            emit(r, outs)
        for h in range(HEADS):
            s_scr[h] = s[h]

    cb = tri_ref.shape[0]
    half = lambda ch: slice(ch * cb, (ch + 1) * cb)
    gs = [_dot(tri_ref[...], jnp.concatenate([hi[half(ch)], lo[half(ch)]], axis=0))
          for ch in range(nch * c // cb)]
    gmin = functools.reduce(jnp.minimum, [jnp.min(g) for g in gs])
    bounded = gmin >= -SAFE_LOG_RANGE

    @pl.when(bounded)
    def _():
        run(cb, lambda ch, r, s: _hgrn_chunk_bounded(q_ref[0, r, :], k[r], i_ref[0, r, :], gs[ch], s, cb))

    @pl.when(jnp.logical_not(bounded))
    def _():
        masks = [mask_ref[l] != 0.0 for l in range(nlev)]
        run(c, lambda ch, r, s: _hgrn_chunk(q_ref[0, r, :], k[r], i_ref[0, r, :], hi[r], lo[r], s,
                                            wall_ref, masks, c))

    @pl.when(n == pl.num_programs(1) - 1)
    def _():
        for h in range(HEADS):
            sn_ref[0, h] = s_scr[h].T


def _hgrn_long(hgp, lb_logits, g_out, s0, layer):
    bsz, t, _ = hgp.shape
    c = HG_CHUNK
    nch = HG_CHUNKS_PER_STEP
    tb = c * nch
    assert t % tb == 0 and c == HEAD_DIM and HEADS % 2 == 0
    w = HEADS * HEAD_DIM
    wall, masks = _hier_tables(c)
    wall = np.concatenate([wall, wall], axis=1)
    masks = np.concatenate([masks, masks], axis=2)
    cb = HG_BOUNDED_CHUNK
    tri = np.tril(np.ones((cb, cb), np.float32))
    tri = np.concatenate([tri, tri], axis=1)
    col = lambda j: pl.BlockSpec((1, tb, w), lambda b, n, j=j: (b, n, j))
    st = pl.BlockSpec((1, HEADS, HEAD_DIM, HEAD_DIM), lambda b, n: (b, 0, 0, 0))
    return pl.pallas_call(
        functools.partial(_hgrn_kernel, c=c, nch=nch, layer=layer),
        grid=(bsz, t // tb),
        in_specs=[col(0), col(1), col(2), col(3),
                  _resident(lb_logits.shape), _resident((1, w)),
                  _resident(wall.shape), _resident(masks.shape), _resident(tri.shape), st],
        out_specs=[pl.BlockSpec((1, tb, w), lambda b, n: (b, n, 0)), st],
        out_shape=[jax.ShapeDtypeStruct((bsz, t, w), F32),
                   jax.ShapeDtypeStruct(s0.shape, F32)],
        scratch_shapes=[pltpu.VMEM((HEADS, HEAD_DIM, HEAD_DIM), F32)],
        compiler_params=_cparams("arbitrary", "arbitrary"),
        name="hgrn_long",
    )(hgp, hgp, hgp, hgp, lb_logits, g_out.reshape(1, w),
      jnp.asarray(wall, BF16), jnp.asarray(masks, F32), jnp.asarray(tri, BF16), s0)


def _gdn_prepare(qs, ks, v, glog, beta, tri_bf, c, nch):
    nb = nch * HEADS
    wc = HEADS * c
    rows = lambda x, ch: x[ch * c:(ch + 1) * c]
    per = lambda f: jnp.stack([f(ch, h) for ch in range(nch) for h in range(HEADS)], axis=0)
    q = per(lambda ch, h: rows(qs[h], ch))
    k = per(lambda ch, h: rows(ks[h], ch))
    vv = per(lambda ch, h: rows(v, ch)[:, _head(h)])
    beta_b = per(lambda ch, h: jnp.broadcast_to(rows(beta, ch)[:, HEADS + h:HEADS + h + 1], (c, LANES)))
    lane = lax.broadcasted_iota(jnp.int32, (c, LANES), 1)
    g_small = None
    for ch in range(nch):
        part = jnp.where(lane < HEADS, rows(glog, ch), 0.0)
        part = part if ch == 0 else pltpu.roll(part, ch * HEADS, axis=1)
        g_small = part if g_small is None else g_small + part
    gcum = _dot_exact_lhs(tri_bf, g_small)
    gcum_t = jnp.concatenate([gcum, jnp.zeros((LANES - c, LANES), F32)], axis=0).T
    g = jnp.stack([jnp.broadcast_to(gcum[:, n:n + 1], (c, LANES)) for n in range(nb)], axis=0)
    packed = lambda f: jnp.stack([jnp.concatenate([f(ch * HEADS + h) for h in range(HEADS)], axis=1)
                                  for ch in range(nch)], axis=0)
    g_col = packed(lambda n: jnp.broadcast_to(gcum[:, n:n + 1], (c, c)))
    g_row = packed(lambda n: jnp.broadcast_to(gcum_t[n:n + 1, :c], (c, c)))
    ti = lax.broadcasted_iota(jnp.int32, (nch, c, wc), 1)
    si = lax.broadcasted_iota(jnp.int32, (nch, c, wc), 2) & (c - 1)
    decay = jnp.exp(jnp.minimum(g_col - g_row, 0.0))
    kb = k * beta_b
    heads_on_lanes = lambda x: jnp.stack(
        [jnp.concatenate([x[ch * HEADS + h] for h in range(HEADS)], axis=1) for ch in range(nch)], axis=0)
    kbq = _bf(jnp.concatenate([heads_on_lanes(kb), heads_on_lanes(q)], axis=1))
    kq = _bmm_nt(kbq, _block_diag(_bf(heads_on_lanes(k)), HEADS))
    m = jnp.where(ti > si, kq[:, :c] * decay, 0.0)
    aqk = jnp.where(ti > si, kq[:, c:] * decay, 0.0)
    qk_diag = jnp.sum(q * k, axis=-1, keepdims=True)
    eg = jnp.exp(g)
    rhs = jnp.concatenate([vv * beta_b, kb * eg], axis=-1)
    npow = -m
    poff = npow
    npow_bd = _block_diag(_bf(npow), HEADS)
    for _ in range(int(math.log2(c)) - 1):
        npow = _bmm(_bf(npow), npow_bd)
        npow_bd = _block_diag(_bf(npow), HEADS)
        poff = poff + npow + _bmm(_bf(poff), npow_bd)
    xoff = _bmm(_block_diag(_bf(poff), HEADS), _bf(rhs.reshape(nch, wc, 2 * HEAD_DIM)))
    x = rhs + xoff.reshape(nb, c, 2 * HEAD_DIM)
    u, w = x[:, :, :HEAD_DIM], x[:, :, HEAD_DIM:]
    gc = g[:, c - 1:c, :]
    return (u, _bf(jnp.concatenate([w, q * eg], axis=1)), _block_diag(_bf(aqk), HEADS), qk_diag,
            _bf(k * jnp.exp(gc - g)), jnp.exp(gc))


def _gdn_kernel(qkv_ref, z_ref, ab_ref, buf_ref, wconv_ref, alog_ref, dtb_ref, gout_ref, tri_ref,
                s0_ref, o_ref, sn_ref, s_scr, ext_scr, *, c, nch):
    n = pl.program_id(1)
    tb = c * nch

    @pl.when(n == 0)
    def _():
        s_scr[...] = s0_ref[0]
        ext_scr[0:SUBLANES - 3, :] = jnp.zeros((SUBLANES - 3, ext_scr.shape[1]), F32)
        ext_scr[SUBLANES - 3:SUBLANES, :] = buf_ref[0]

    @pl.when(n > 0)
    def _():
        ext_scr[0:SUBLANES, :] = ext_scr[tb:tb + SUBLANES, :]

    ext_scr[SUBLANES:SUBLANES + tb, :] = qkv_ref[0]
    qs, ks, v, glog, beta = _gdn_inputs(ext_scr, tb, wconv_ref[...], ab_ref[0], alog_ref[...],
                                        dtb_ref[...], None)
    u, wq, aqk_bd, qk_diag, kg, egc = _gdn_prepare(qs, ks, v, glog, beta, tri_ref[...], c, nch)
    gout = gout_ref[...]
    s = s_scr[...]
    for ch in range(nch):
        sel = slice(ch * HEADS, (ch + 1) * HEADS)
        ws_qs = _bmm(wq[sel], _bf(s))
        v_new = u[sel] - ws_qs[:, :c]
        intra = _dot(aqk_bd[ch], _bf(v_new.reshape(HEADS * c, HEAD_DIM)))
        o = ws_qs[:, c:] + intra.reshape(HEADS, c, HEAD_DIM) + qk_diag[sel] * v_new
        s = egc[sel] * s + _bmm_tn(kg[sel], _bf(v_new))
        for h in range(HEADS):
            sl = _head(h)
            o_ref[0, ch * c:(ch + 1) * c, sl] = _rms(o[h], gout) * jax.nn.silu(z_ref[0, ch * c:(ch + 1) * c, sl])
    s_scr[...] = s

    @pl.when(n == pl.num_programs(1) - 1)
    def _():
        sn_ref[0] = s


def _pad_lanes(x):
    return jnp.zeros((1, LANES), F32).at[0, :x.shape[0]].set(x.astype(F32))


def _gdn_long(gqkv, gz, ab, conv_buf, w_conv, a_log, dt_bias, g_out, s0):
    bsz, t, wq = gqkv.shape
    c = GDN_CHUNK
    nch = GDN_CHUNKS_PER_STEP
    tb = c * nch
    assert t % tb == 0
    w = HEADS * HEAD_DIM
    tri = np.tril(np.ones((c, c), np.float32))
    st = pl.BlockSpec((1, HEADS, HEAD_DIM, HEAD_DIM), lambda b, n: (b, 0, 0, 0))
    return pl.pallas_call(
        functools.partial(_gdn_kernel, c=c, nch=nch),
        grid=(bsz, t // tb),
        in_specs=[pl.BlockSpec((1, tb, wq), lambda b, n: (b, n, 0)),
                  pl.BlockSpec((1, tb, w), lambda b, n: (b, n, 0)),
                  pl.BlockSpec((1, tb, LANES), lambda b, n: (b, n, 0)),
                  pl.BlockSpec((1, CONV_W - 1, wq), lambda b, n: (b, 0, 0)),
                  _resident((CONV_W, wq)), _resident((1, LANES)), _resident((1, LANES)),
                  _resident((1, HEAD_DIM)), _resident((c, c)), st],
        out_specs=[pl.BlockSpec((1, tb, w), lambda b, n: (b, n, 0)), st],
        out_shape=[jax.ShapeDtypeStruct((bsz, t, w), F32),
                   jax.ShapeDtypeStruct((bsz, HEADS, HEAD_DIM, HEAD_DIM), F32)],
        scratch_shapes=[pltpu.VMEM((HEADS, HEAD_DIM, HEAD_DIM), F32),
                        pltpu.VMEM((tb + SUBLANES, wq), F32)],
        compiler_params=_cparams("arbitrary", "arbitrary"),
        name="gdn_long",
    )(gqkv, gz, ab, conv_buf, w_conv,
      _pad_lanes(a_log), _pad_lanes(dt_bias),
      g_out.reshape(1, HEAD_DIM), jnp.asarray(tri, BF16), s0)


def _shift_rows(x, d):
    return x if d == 0 else pltpu.roll(x, d, axis=0)


def _cumsum_rows8(x, rows):
    for d in (1, 2, 4):
        x = x + jnp.where(rows >= d, pltpu.roll(x, d, axis=0), 0.0)
    return x


def _hgrn_short(q, hf, v, lb, s, rows, t_valid):
    f = lb + (1.0 - lb) * jax.nn.sigmoid(hf)
    valid = rows < t_valid
    logf = jnp.where(valid, jnp.log(f), 0.0)
    k = jnp.where(valid, 1.0 - f, 0.0)
    g = _cumsum_rows8(logf, rows)
    gc = g[SUBLANES - 1:SUBLANES, :]
    o = _dot(_bf(q * jnp.exp(g)), _bf(s))
    for d in range(t_valid):
        ok = rows >= d
        dec = jnp.exp(jnp.where(ok, g - _shift_rows(g, d), 0.0))
        a = jnp.sum(jnp.where(ok, q * _shift_rows(k, d) * dec, 0.0), axis=-1, keepdims=True)
        o = o + a * _shift_rows(v, d)
    s_new = _col_to_matrix_mxu(jnp.exp(gc)) * s + _dot_tn(_bf(k * jnp.exp(gc - g)), _bf(v))
    return o, s_new


def _gdn_short(q, k, v, g_b, beta_b, s, rows, t_valid):
    g = _cumsum_rows8(g_b, rows)
    gc = g[SUBLANES - 1:SUBLANES, :]
    eg = jnp.exp(g)
    kb = k * beta_b
    dec = [None] + [jnp.exp(jnp.where(rows >= d, g - _shift_rows(g, d), 0.0)) for d in range(1, t_valid)]
    mcol = [None] + [jnp.where(rows >= d, jnp.sum(kb * _shift_rows(k, d), axis=-1, keepdims=True) * dec[d], 0.0)
                     for d in range(1, t_valid)]
    ru, rw = v * beta_b, kb * eg
    xu, xw = ru, rw
    for _ in range(t_valid - 1):
        nu, nw = ru, rw
        for d in range(1, t_valid):
            nu = nu - mcol[d] * _shift_rows(xu, d)
            nw = nw - mcol[d] * _shift_rows(xw, d)
        xu, xw = nu, nw
    ws_qs = _dot(_bf(jnp.concatenate([xw, q * eg], axis=0)), _bf(s))
    v_new = xu - ws_qs[:SUBLANES]
    o = ws_qs[SUBLANES:] + jnp.sum(q * k, axis=-1, keepdims=True) * v_new
    for d in range(1, t_valid):
        a = jnp.where(rows >= d, jnp.sum(q * _shift_rows(k, d), axis=-1, keepdims=True) * dec[d], 0.0)
        o = o + a * _shift_rows(v_new, d)
    s_new = jnp.exp(gc) * s + _dot_tn(_bf(k * jnp.exp(gc - g)), _bf(v_new))
    return o, s_new


def _short_kernel(hq_ref, hf_ref, hi_ref, hgate_ref, qkv_ref, z_ref, ab_ref, buf_ref,
                  lbl_ref, hgout_ref, wconv_ref, alog_ref, dtb_ref, ggout_ref, shg0_ref, sgd0_ref,
                  ohg_ref, ogd_ref, shg_ref, sgd_ref, ext_scr, *, t_valid, layer):
    bb = hq_ref.shape[0]
    rows = lax.broadcasted_iota(jnp.int32, (SUBLANES, LANES), 0)
    lb = _forget_lower_bound(lbl_ref[...], layer)
    hgout = hgout_ref[...]
    ggout = ggout_ref[...]
    wconv = wconv_ref[...]
    alog = alog_ref[...]
    dtb = dtb_ref[...]

    def tile8(x):
        if t_valid == SUBLANES:
            return x
        return jnp.concatenate([x, jnp.zeros((SUBLANES - t_valid, x.shape[1]), x.dtype)], axis=0)

    def per_seq(i, ext):
        ext[SUBLANES - 3:SUBLANES, :] = buf_ref[i]
        ext[SUBLANES:2 * SUBLANES, :] = tile8(qkv_ref[i])
        qs, ks, v, glog, beta = _gdn_inputs(ext, SUBLANES, wconv, tile8(ab_ref[i]), alog, dtb, t_valid)
        for h in range(HEADS):
            sl = _head(h)
            o, s_new = _hgrn_short(tile8(hq_ref[i, :, sl]), tile8(hf_ref[i, :, sl]), tile8(hi_ref[i, :, sl]),
                                   lb[:, sl], shg0_ref[i, h], rows, t_valid)
            shg_ref[i, h] = s_new
            ohg_ref[i, :, sl] = (_rms(o, hgout[:, sl])[:t_valid]) * jax.nn.silu(hgate_ref[i, :, sl])
            g_b = jnp.broadcast_to(glog[:, h:h + 1], (SUBLANES, LANES))
            beta_b = jnp.broadcast_to(beta[:, HEADS + h:HEADS + h + 1], (SUBLANES, LANES))
            o, s_new = _gdn_short(qs[h], ks[h], v[:, sl], g_b, beta_b, sgd0_ref[i, h], rows, t_valid)
            sgd_ref[i, h] = s_new
            ogd_ref[i, :, sl] = (_rms(o, ggout)[:t_valid]) * jax.nn.silu(z_ref[i, :, sl])

    def per_group(j, carry):
        for u in range(DECODE_INTERLEAVE):
            per_seq(j * DECODE_INTERLEAVE + u, ext_scr.at[u])
        return carry

    lax.fori_loop(0, bb // DECODE_INTERLEAVE, per_group, 0)


def _mixers_short(hgp, gqkv, gz, ab, conv_buf, lb_logits, hg_out, w_conv, a_log, dt_bias, gdn_out,
                  s_hg0, s_gdn0, t_valid, layer):
    bsz, t, wq = gqkv.shape
    assert t == t_valid and CONV_W - 1 <= t_valid <= SUBLANES
    bb = DECODE_BLOCK
    assert bsz % bb == 0 and bb % DECODE_INTERLEAVE == 0
    w = HEADS * HEAD_DIM
    col = lambda j: pl.BlockSpec((bb, t, w), lambda b, j=j: (b, 0, j))
    tok = lambda width: pl.BlockSpec((bb, t, width), lambda b: (b, 0, 0))
    st = pl.BlockSpec((bb, HEADS, HEAD_DIM, HEAD_DIM), lambda b: (b, 0, 0, 0))
    return pl.pallas_call(
        functools.partial(_short_kernel, t_valid=t_valid, layer=layer),
        grid=(bsz // bb,),
        in_specs=[col(0), col(1), col(2), col(3), tok(wq), tok(w), tok(LANES),
                  pl.BlockSpec((bb, CONV_W - 1, wq), lambda b: (b, 0, 0)),
                  _resident(lb_logits.shape), _resident((1, w)), _resident((CONV_W, wq)),
                  _resident((1, LANES)), _resident((1, LANES)), _resident((1, HEAD_DIM)), st, st],
        out_specs=[tok(w), tok(w), st, st],
        out_shape=[jax.ShapeDtypeStruct((bsz, t, w), F32), jax.ShapeDtypeStruct((bsz, t, w), F32),
                   jax.ShapeDtypeStruct(s_hg0.shape, F32), jax.ShapeDtypeStruct(s_gdn0.shape, F32)],
        scratch_shapes=[pltpu.VMEM((DECODE_INTERLEAVE, 2 * SUBLANES, wq), F32)],
        compiler_params=_cparams("arbitrary"),
        name="mixers_short",
    )(hgp, hgp, hgp, hgp, gqkv, gz, ab, conv_buf, lb_logits, hg_out.reshape(1, w), w_conv,
      _pad_lanes(a_log), _pad_lanes(dt_bias), gdn_out.reshape(1, HEAD_DIM), s_hg0, s_gdn0)


def _mem_attn_kernel(q_ref, k_ref, v_ref, o_ref):
    for h in range(HEADS):
        sl = _head(h)
        s = _bmm_nt(_bf(q_ref[:, :, sl]), _bf(k_ref[:, :, sl])) * (HEAD_DIM ** -0.5)
        e = jnp.exp(s - jnp.max(s, axis=-1, keepdims=True))
        o_ref[:, :, sl] = _bmm(_bf(e), _bf(v_ref[:, :, sl])) / jnp.sum(e, axis=-1, keepdims=True)


def _mem_attn(mq, mem_k, mem_v, bb, tq):
    bsz, t, w = mq.shape
    m = mem_k.shape[1]
    assert bsz % bb == 0 and t % tq == 0
    kv = pl.BlockSpec((bb, m, w), lambda b, n: (b, 0, 0))
    return pl.pallas_call(
        _mem_attn_kernel,
        grid=(bsz // bb, t // tq),
        in_specs=[pl.BlockSpec((bb, tq, w), lambda b, n: (b, n, 0)), kv, kv],
        out_specs=pl.BlockSpec((bb, tq, w), lambda b, n: (b, n, 0)),
        out_shape=jax.ShapeDtypeStruct((bsz, t, w), F32),
        compiler_params=_cparams("arbitrary", "arbitrary"),
        name="mem_attn",
    )(mq, mem_k, mem_v)


def _mem_attn_cache_kernel(q_ref, k_hbm, v_hbm, o_ref, kbuf, vbuf, sem):
    i = pl.program_id(0)
    bb = q_ref.shape[0]

    def copies(step, slot):
        out = []
        for h in range(HEADS):
            for j, (src, dst) in enumerate(((k_hbm, kbuf), (v_hbm, vbuf))):
                out.append(pltpu.make_async_copy(src.at[pl.ds(step * bb, bb), :, h, :],
                                                 dst.at[slot, h], sem.at[j, slot, h]))
        return out

    slot = i % 2

    @pl.when(i == 0)
    def _():
        for cp in copies(0, 0):
            cp.start()

    @pl.when(i + 1 < pl.num_programs(0))
    def _():
        for cp in copies(i + 1, 1 - slot):
            cp.start()

    for cp in copies(i, slot):
        cp.wait()
    t = q_ref.shape[1]
    for h in range(HEADS):
        sl = _head(h)
        q = q_ref[:, :, sl]
        if t < SUBLANES:
            q = jnp.concatenate([q, jnp.zeros((bb, SUBLANES - t, HEAD_DIM), q.dtype)], axis=1)
        s = _bmm_nt(_bf(q), _bf(kbuf[slot, h])) * (HEAD_DIM ** -0.5)
        e = jnp.exp(s - jnp.max(s, axis=-1, keepdims=True))
        o = _bmm(_bf(e), _bf(vbuf[slot, h])) / jnp.sum(e, axis=-1, keepdims=True)
        o_ref[:, :, sl] = o[:, :t]


def _mem_attn_cache(mq, cache_k, cache_v, bb):
    bsz, t, w = mq.shape
    m = cache_k.shape[1]
    assert bsz % bb == 0 and cache_k.shape == (bsz, m, HEADS, HEAD_DIM)
    return pl.pallas_call(
        _mem_attn_cache_kernel,
        grid=(bsz // bb,),
        in_specs=[pl.BlockSpec((bb, t, w), lambda b: (b, 0, 0)),
                  pl.BlockSpec(memory_space=pl.ANY), pl.BlockSpec(memory_space=pl.ANY)],
        out_specs=pl.BlockSpec((bb, t, w), lambda b: (b, 0, 0)),
        out_shape=jax.ShapeDtypeStruct((bsz, t, w), F32),
        scratch_shapes=[pltpu.VMEM((2, HEADS, bb, m, HEAD_DIM), F32),
                        pltpu.VMEM((2, HEADS, bb, m, HEAD_DIM), F32),
                        pltpu.SemaphoreType.DMA((2, 2, HEADS))],
        compiler_params=_cparams("arbitrary"),
        name="mem_attn_cache",
    )(mq, cache_k, cache_v)


def _merge_ffn_kernel(x_ref, ohg_ref, ogd_ref, omem_ref, gpre_ref, wgates_ref,
                      wbh_ref, wbg_ref, wbm_ref, wout_ref, gpm_ref, gpf_ref,
                      wffn_ref, wdown_ref, gpo_ref, y_ref):
    d = x_ref.shape[-1]
    x = x_ref[...]
    xn = _bf(_rms(x, gpre_ref[...]))
    merged = None
    for j, (o_ref, wb_ref) in enumerate(((ohg_ref, wbh_ref), (ogd_ref, wbg_ref), (omem_ref, wbm_ref))):
        gate = jax.nn.sigmoid(_dot(xn, wgates_ref[:, j * d:(j + 1) * d]))
        term = gate * _dot(_bf(o_ref[...]), wb_ref[...])
        merged = term if merged is None else merged + term
    h = x + _rms(_dot(_bf(merged), wout_ref[...]), gpm_ref[...])
    hn = _bf(_rms(h, gpf_ref[...]))
    hidden = wdown_ref.shape[0]
    ff = None
    for c0 in range(0, hidden, FFN_CHUNK):
        a = (jax.nn.silu(_dot(hn, wffn_ref[:, c0:c0 + FFN_CHUNK]))
             * _dot(hn, wffn_ref[:, hidden + c0:hidden + c0 + FFN_CHUNK]))
        p = _dot(_bf(a), wdown_ref[c0:c0 + FFN_CHUNK, :])
        ff = p if ff is None else ff + p
    y_ref[...] = h + _rms(ff, gpo_ref[...])


def _merge_ffn(x2d, ohg, ogd, omem, gpre, wgates, wbh, wbg, wbm, wout, gpm, gpf, wffn, wdown, gpo):
    n, d = x2d.shape
    tm = min(TOKEN_TILE, n)
    hidden = wdown.shape[0]
    assert n % tm == 0 and hidden % FFN_CHUNK == 0 and wffn.shape[1] == 2 * hidden
    assert wgates.shape == (d, N_BRANCH * d)
    tok = lambda width: pl.BlockSpec((tm, width), lambda i: (i, 0))
    row = lambda a: a.reshape(1, d)
    return pl.pallas_call(
        _merge_ffn_kernel,
        grid=(n // tm,),
        in_specs=[tok(d), tok(ohg.shape[1]), tok(ogd.shape[1]), tok(omem.shape[1]),
                  _resident((1, d)), _resident(wgates.shape),
                  _resident(wbh.shape), _resident(wbg.shape), _resident(wbm.shape), _resident(wout.shape),
                  _resident((1, d)), _resident((1, d)),
                  _resident(wffn.shape), _resident(wdown.shape), _resident((1, d))],
        out_specs=tok(d),
        out_shape=jax.ShapeDtypeStruct((n, d), F32),
        compiler_params=_cparams("arbitrary"),
        name="merge_ffn",
    )(x2d, ohg, ogd, omem, row(gpre), wgates, wbh, wbg, wbm, wout, row(gpm), row(gpf), wffn, wdown, row(gpo))


def _pack_w_in_kernel(wt_ref, mix_ref, gates_ref):
    w = HEADS * HEAD_DIM
    c_ab = 8 * w
    c_mq = c_ab + 2 * HEADS
    c_gt = c_mq + w

    def put(dst_ref, dst0, src0, n):
        for j in range(0, n, w):
            nb = min(w, n - j)
            dst_ref[:, dst0 + j:dst0 + j + nb] = _bf(wt_ref[pl.ds(src0 + j, nb), :].T)

    put(mix_ref, 0, 0, c_ab)
    put(mix_ref, c_ab, c_mq, w)
    ab = jnp.concatenate([wt_ref[pl.ds(c_ab, 2 * HEADS), :],
                          jnp.zeros((LANES - 2 * HEADS, wt_ref.shape[1]), F32)], axis=0)
    mix_ref[:, c_ab + w:c_ab + w + LANES] = _bf(ab.T)
    put(gates_ref, 0, c_gt, gates_ref.shape[1])


def _pack_w_in(w_in_t):
    total, d = w_in_t.shape
    w = HEADS * HEAD_DIM
    widths = (4 * w, 3 * w, w, w, LANES)
    n_gates = total - (9 * w + 2 * HEADS)
    assert n_gates % LANES == 0 and d % LANES == 0
    mixers, gates = pl.pallas_call(
        _pack_w_in_kernel,
        out_shape=[jax.ShapeDtypeStruct((d, sum(widths)), BF16), jax.ShapeDtypeStruct((d, n_gates), BF16)],
        compiler_params=pltpu.CompilerParams(vmem_limit_bytes=VMEM_LIMIT),
        name="pack_w_in",
    )(w_in_t)
    return mixers, widths, gates


def _of_layer(a, l):
    return a.reshape(a.shape[1:]) if a.shape[0] == 1 else a[l]


def _layer(x, mem_k, mem_v, conv_buf, s_hg, s_gdn, layer, lb_logits, p):
    bsz, t, d = x.shape
    t_valid = t
    short = t <= SUBLANES
    x2d = x.reshape(bsz * t, d)
    w = HEADS * HEAD_DIM

    hgp, gqkv, gz, mq, ab = _norm_proj(x2d, p["g_pre_mix"], p["w_in"], p["w_in_widths"], "in_proj")
    as3d = lambda a: a.reshape(bsz, t, a.shape[-1])
    gqkv3 = as3d(gqkv)
    if short:
        o_hg, o_gdn, s_hg_new, s_gdn_new = _mixers_short(
            as3d(hgp), gqkv3, as3d(gz), as3d(ab), conv_buf, lb_logits, p["g_hg_out"], p["w_conv"],
            p["a_log"], p["dt_bias"], p["g_gdn_out"], s_hg, s_gdn, t_valid, layer)
        o_mem = _mem_attn_cache(as3d(mq), mem_k, mem_v, ATTN_DECODE_BLOCK)
    else:
        o_hg, s_hg_new = _hgrn_long(as3d(hgp), lb_logits, p["g_hg_out"], s_hg, layer)
        o_gdn, s_gdn_new = _gdn_long(gqkv3, as3d(gz), as3d(ab), conv_buf, p["w_conv"], p["a_log"],
                                     p["dt_bias"], p["g_gdn_out"], s_gdn)
        o_mem = _mem_attn(as3d(mq), mem_k, mem_v, 1, min(ATTN_TILE, t))
    assert t_valid >= CONV_W - 1
    conv_new = gqkv3[:, t_valid - (CONV_W - 1):t_valid, :]

    flat = lambda a: a.reshape(bsz * t, w)
    y = _merge_ffn(x2d, flat(o_hg), flat(o_gdn), flat(o_mem), p["g_pre_mix"], p["w_gates"],
                   p["w_br_hg"], p["w_br_gdn"], p["w_br_mem"], p["w_out"], p["g_post_mix"],
                   p["g_pre_ffn"], p["w_ffn_in"], p["w_ffn_out"], p["g_post_ffn"])
    return y.reshape(bsz, t, d), conv_new, s_hg_new, s_gdn_new


def kernel(x_prompt, x_sample, mem_prompt, cache_mem_k, cache_mem_v, state_hgrn, state_gdn, state_gdn_conv, hg_lb_logits, g_pre_mix, w_in, w_conv, a_log, dt_bias, g_hg_out, g_gdn_out, g_mem, w_mem_kv, w_br_hg, w_br_gdn, w_br_mem, w_out, g_post_mix, g_pre_ffn, w_ffn_in, w_ffn_out, g_post_ffn):
    depth = w_in.shape[0]
    bp, _, d = x_prompt.shape
    m = mem_prompt.shape[1]
    w = HEADS * HEAD_DIM
    yp, ys = x_prompt, x_sample
    outs = [[] for _ in range(8)]
    for l in range(depth):
        w_in_packed, widths, w_gates = _pack_w_in(_of_layer(w_in, l).T)
        p = dict(g_pre_mix=g_pre_mix[l], w_in=w_in_packed, w_in_widths=widths, w_gates=w_gates,
                 w_conv=w_conv[l],
                 a_log=a_log[l], dt_bias=dt_bias[l], g_hg_out=g_hg_out[l], g_gdn_out=g_gdn_out[l],
                 w_br_hg=_bf(w_br_hg[l]), w_br_gdn=_bf(w_br_gdn[l]), w_br_mem=_bf(w_br_mem[l]),
                 w_out=_bf(w_out[l]), g_post_mix=g_post_mix[l], g_pre_ffn=g_pre_ffn[l],
                 w_ffn_in=_bf(w_ffn_in[l]), w_ffn_out=_bf(w_ffn_out[l]), g_post_ffn=g_post_ffn[l])
        mk, mv = _norm_proj(mem_prompt.reshape(bp * m, d), g_mem[l], _bf(w_mem_kv[l]), (w, w), "mem_kv")
        mk, mv = mk.reshape(bp, m, w), mv.reshape(bp, m, w)
        zeros_state = jnp.zeros((bp, HEADS, HEAD_DIM, HEAD_DIM), F32)
        yp, cb, sh, sg = _layer(yp, mk, mv, jnp.zeros((bp, CONV_W - 1, 3 * w), F32),
                                zeros_state, zeros_state, l, hg_lb_logits, p)
        bs = x_sample.shape[0]
        ys, cb2, sh2, sg2 = _layer(ys, _of_layer(cache_mem_k, l), _of_layer(cache_mem_v, l),
                                   _of_layer(state_gdn_conv, l), _of_layer(state_hgrn, l),
                                   _of_layer(state_gdn, l), l, hg_lb_logits, p)
        for lst, val in zip(outs, (mk.reshape(bp, m, HEADS, HEAD_DIM), mv.reshape(bp, m, HEADS, HEAD_DIM),
                                   sh, sg, cb, sh2, sg2, cb2)):
            lst.append(val)
    return (yp, ys) + tuple(o[0].reshape((1,) + o[0].shape) if depth == 1 else jnp.stack(o) for o in outs)
```

```python
import functools
import math

import numpy as np
import jax
import jax.numpy as jnp
from jax import lax
from jax.experimental import pallas as pl
from jax.experimental.pallas import tpu as pltpu

F32 = jnp.float32
BF16 = jnp.bfloat16

EPS = 1e-6
HEADS = 4
HEAD_DIM = 128
CONV_W = 4
N_BRANCH = 3

LANES = 128
SUBLANES = 8
VMEM_LIMIT = 56 * 1024 * 1024

SAFE_LOG_RANGE = 70.0
HG_CHUNK = 128
HG_BOUNDED_CHUNK = 64
HG_CHUNKS_PER_STEP = 8
GDN_CHUNK = 64
GDN_CHUNKS_PER_STEP = 16
TOKEN_TILE = 512
ATTN_TILE = 512
FFN_CHUNK = 256
DECODE_BLOCK = 8
ATTN_DECODE_BLOCK = 8


def _cparams(*sem):
    return pltpu.CompilerParams(dimension_semantics=sem, vmem_limit_bytes=VMEM_LIMIT)


def _resident(shape):
    nd = len(shape)
    return pl.BlockSpec(shape, lambda *_: (0,) * nd, pipeline_mode=pl.Buffered(1))


def _bf(x):
    return x.astype(BF16)


def _dot(a, b):
    return jnp.dot(a, b, preferred_element_type=F32)


def _dot_nt(a, b):
    return lax.dot_general(a, b, (((1,), (1,)), ((), ())), preferred_element_type=F32)


def _dot_tn(a, b):
    return lax.dot_general(a, b, (((0,), (0,)), ((), ())), preferred_element_type=F32)


def _bmm(a, b):
    return jnp.einsum("nij,njk->nik", a, b, preferred_element_type=F32)


def _bmm_nt(a, b):
    return jnp.einsum("nid,njd->nij", a, b, preferred_element_type=F32)


def _bmm_tn(a, b):
    return jnp.einsum("nci,ncj->nij", a, b, preferred_element_type=F32)


def _split2(x):
    hi = _bf(x)
    lo = _bf(x - hi.astype(F32))
    return hi, lo


def _dot_exact_lhs(w_bf16, x):
    hi, lo = _split2(x)
    return _dot(w_bf16, hi) + _dot(w_bf16, lo)


def _rms(x, g):
    return x * lax.rsqrt(jnp.mean(x * x, axis=-1, keepdims=True) + EPS) * g


def _head(h):
    return slice(h * HEAD_DIM, (h + 1) * HEAD_DIM)


def _block_diag(x, nblk):
    wblk = x.shape[-1] // nblk
    if wblk % LANES == 0:
        zeros = lambda n: [jnp.zeros(x.shape[:-1] + (n * wblk,), x.dtype)] if n else []
        parts = [jnp.concatenate(zeros(j) + [x[..., j * wblk:(j + 1) * wblk]] + zeros(nblk - 1 - j), axis=-1)
                 for j in range(nblk)]
    else:
        lane = lax.broadcasted_iota(jnp.int32, x.shape, x.ndim - 1)
        parts = [jnp.where((lane >= j * wblk) & (lane < (j + 1) * wblk), x, jnp.zeros_like(x))
                 for j in range(nblk)]
    return jnp.concatenate(parts, axis=-2)


def _norm_proj_kernel(x_ref, g_ref, w_ref, *out_refs):
    xn = _bf(_rms(x_ref[...], g_ref[...]))
    off = 0
    for o_ref in out_refs:
        width = o_ref.shape[-1]
        for c0 in range(0, width, 512):
            cw = min(512, width - c0)
            o_ref[:, c0:c0 + cw] = _dot(xn, w_ref[:, off + c0:off + c0 + cw])
        off += width


def _norm_proj(x2d, g, w_bf16, widths, name):
    n, d = x2d.shape
    tm = min(TOKEN_TILE, n)
    assert n % tm == 0 and sum(widths) == w_bf16.shape[1]
    return pl.pallas_call(
        _norm_proj_kernel,
        grid=(n // tm,),
        in_specs=[pl.BlockSpec((tm, d), lambda i: (i, 0)),
                  _resident((1, d)),
                  _resident(w_bf16.shape)],
        out_specs=[pl.BlockSpec((tm, w), lambda i: (i, 0)) for w in widths],
        out_shape=[jax.ShapeDtypeStruct((n, w), F32) for w in widths],
        compiler_params=_cparams("arbitrary"),
        name=name,
    )(x2d, g.reshape(1, d), w_bf16)


def _forget_lower_bound(logits, layer):
    m = jnp.max(logits, axis=0, keepdims=True)
    e = jnp.exp(logits - m)
    return jnp.sum(e[:layer + 1], axis=0, keepdims=True) / jnp.sum(e, axis=0, keepdims=True)


def _col_to_matrix_mxu(row):
    b, _, n = row.shape
    tile = jnp.concatenate([row, jnp.zeros((b, SUBLANES - 1, n), row.dtype)], axis=1)
    hi = _bf(tile)
    mid = _bf(tile - hi.astype(F32))
    lo = _bf(tile - hi.astype(F32) - mid.astype(F32))
    ones = jnp.ones((b, SUBLANES, n), BF16)
    return _bmm_tn(hi, ones) + _bmm_tn(mid, ones) + _bmm_tn(lo, ones)


def _gdn_inputs(acc, ab, alog, dtb, t_valid):
    qkv = jax.nn.silu(acc)
    w = HEADS * HEAD_DIM
    qs, ks = [], []
    for h in range(HEADS):
        qh = qkv[..., h * HEAD_DIM:(h + 1) * HEAD_DIM]
        kh = qkv[..., w + h * HEAD_DIM:w + (h + 1) * HEAD_DIM]
        qs.append(qh * lax.rsqrt(jnp.sum(qh * qh, axis=-1, keepdims=True) + EPS) * (HEAD_DIM ** -0.5))
        ks.append(kh * lax.rsqrt(jnp.sum(kh * kh, axis=-1, keepdims=True) + EPS))
    v = qkv[..., 2 * w:3 * w]
    pre = ab + dtb
    softplus = jnp.maximum(pre, 0.0) + jnp.log(1.0 + jnp.exp(-jnp.abs(pre)))
    glog = -jnp.exp(alog) * softplus
    beta = jax.nn.sigmoid(ab)
    if t_valid is not None:
        valid = lax.broadcasted_iota(jnp.int32, ab.shape, ab.ndim - 2) < t_valid
        glog = jnp.where(valid, glog, 0.0)
        beta = jnp.where(valid, beta, 0.0)
    return qs, ks, v, glog, beta


def _hier_tables(c):
    t = np.arange(c)
    u = np.arange(c)
    tri = (u[None, :] <= t[:, None]).astype(np.float32)
    rows, masks = [], []
    b = c // 2
    while b >= 1:
        ref = (t // (2 * b)) * 2 * b + b - 1
        rows.append(tri - (u[None, :] <= ref[:, None]).astype(np.float32))
        right = (t % (2 * b)) >= b
        left = (t % (2 * b)) < b
        same = (t[:, None] // (2 * b)) == (t[None, :] // (2 * b))
        masks.append((right[:, None] & left[None, :] & same).astype(np.float32))
        b //= 2
    rows.append(tri)
    return np.concatenate(rows, 0), np.stack(masks, 0)


def _hgrn_chunk(q, k, v, logf_hi, logf_lo, s, wall_ref, masks, c):
    nlev = len(masks)
    d = _dot(wall_ref[...], jnp.concatenate([logf_hi, logf_lo], axis=0))
    g = d[nlev * c:(nlev + 1) * c]
    gc = g[c - 1:c, :]
    qg = _bf(q * jnp.exp(g))
    kg = _bf(k * jnp.exp(gc - g))
    egc = jnp.exp(gc)
    vb = _bf(v)
    npair = HEADS // 2
    a = [None] * npair
    for l in range(nlev):
        e = jnp.exp(-jnp.abs(d[l * c:(l + 1) * c]))
        ql, kl = _bf(q * e), _bf(k * e)
        for pr in range(npair):
            pair = slice(2 * pr * HEAD_DIM, (2 * pr + 2) * HEAD_DIM)
            p = _dot_nt(ql[:, pair], _block_diag(kl[:, pair], 2))
            a[pr] = jnp.where(masks[l], p, 0.0) if a[pr] is None else jnp.where(masks[l], p, a[pr])
    qk = q * k
    outs, s_new = [], []
    for h in range(HEADS):
        sl = _head(h)
        a_h = a[h // 2][:, (h % 2) * c:(h % 2 + 1) * c]
        o = _dot_nt(qg[:, sl], _bf(s[h])) + _dot(_bf(a_h), vb[:, sl])
        outs.append(o + jnp.sum(qk[:, sl], axis=-1, keepdims=True) * v[:, sl])
        s_new.append(egc[:, sl] * s[h] + _dot_tn(vb[:, sl], kg[:, sl]))
    return outs, s_new


def _hgrn_chunk_bounded(q, k, v, g, s, c):
    gc = g[c - 1:c, :]
    qg = _bf(q * jnp.exp(g))
    kn = _bf(k * jnp.exp(-g))
    kg = _bf(k * jnp.exp(gc - g))
    egc = jnp.exp(gc)
    vb = _bf(v)
    ti = lax.broadcasted_iota(jnp.int32, (c, HEADS * c), 0)
    si = lax.broadcasted_iota(jnp.int32, (c, HEADS * c), 1) & (c - 1)
    a = jnp.where(ti > si, _dot_nt(qg, _block_diag(kn, HEADS)), 0.0)
    v_rows = jnp.concatenate([vb[:, _head(h)] for h in range(HEADS)], axis=0)
    intra = _dot(_block_diag(_bf(a), HEADS), v_rows)
    qk = q * k
    outs, s_new = [], []
    for h in range(HEADS):
        sl = _head(h)
        outs.append(_dot_nt(qg[:, sl], _bf(s[h])) + intra[h * c:(h + 1) * c]
                    + jnp.sum(qk[:, sl], axis=-1, keepdims=True) * v[:, sl])
        s_new.append(egc[:, sl] * s[h] + _dot_tn(vb[:, sl], kg[:, sl]))
    return outs, s_new


def _hgrn_kernel(q_ref, f_ref, i_ref, gate_ref, lbl_ref, gout_ref, wall_ref, mask_ref, tri_ref, s0_ref,
                 o_ref, sn_ref, s_scr, *, c, nch, layer):
    n = pl.program_id(1)

    @pl.when(n == 0)
    def _():
        for h in range(HEADS):
            s_scr[h] = s0_ref[0, h].T

    lb = _forget_lower_bound(lbl_ref[...], layer)
    gout = gout_ref[...]
    f = lb + (1.0 - lb) * jax.nn.sigmoid(f_ref[0])
    hi, lo = _split2(jnp.log(f))
    k = 1.0 - f
    nlev = mask_ref.shape[0]

    def emit(r, outs):
        for h in range(HEADS):
            sl = _head(h)
            o_ref[0, r, sl] = _rms(outs[h], gout[:, sl]) * jax.nn.silu(gate_ref[0, r, sl])

    def run(chunk_len, chunk_fn):
        s = [s_scr[h] for h in range(HEADS)]
        for ch in range(nch * c // chunk_len):
            r = slice(ch * chunk_len, (ch + 1) * chunk_len)
            outs, s = chunk_fn(ch, r, s)
            emit(r, outs)
        for h in range(HEADS):
            s_scr[h] = s[h]

    cb = tri_ref.shape[0]
    half = lambda ch: slice(ch * cb, (ch + 1) * cb)
    gs = [_dot(tri_ref[...], jnp.concatenate([hi[half(ch)], lo[half(ch)]], axis=0))
          for ch in range(nch * c // cb)]
    gmin = functools.reduce(jnp.minimum, [jnp.min(g) for g in gs])
    bounded = gmin >= -SAFE_LOG_RANGE

    @pl.when(bounded)
    def _():
        run(cb, lambda ch, r, s: _hgrn_chunk_bounded(q_ref[0, r, :], k[r], i_ref[0, r, :], gs[ch], s, cb))

    @pl.when(jnp.logical_not(bounded))
    def _():
        masks = [mask_ref[l] != 0.0 for l in range(nlev)]
        run(c, lambda ch, r, s: _hgrn_chunk(q_ref[0, r, :], k[r], i_ref[0, r, :], hi[r], lo[r], s,
                                            wall_ref, masks, c))

    @pl.when(n == pl.num_programs(1) - 1)
    def _():
        for h in range(HEADS):
            sn_ref[0, h] = s_scr[h].T


def _hgrn_long(hgp, lb_logits, g_out, s0, layer):
    bsz, t, _ = hgp.shape
    c = HG_CHUNK
    nch = HG_CHUNKS_PER_STEP
    tb = c * nch
    assert t % tb == 0 and c == HEAD_DIM and HEADS % 2 == 0
    w = HEADS * HEAD_DIM
    wall, masks = _hier_tables(c)
    wall = np.concatenate([wall, wall], axis=1)
    masks = np.concatenate([masks, masks], axis=2)
    cb = HG_BOUNDED_CHUNK
    tri = np.tril(np.ones((cb, cb), np.float32))
    tri = np.concatenate([tri, tri], axis=1)
    col = lambda j: pl.BlockSpec((1, tb, w), lambda b, n, j=j: (b, n, j))
    st = pl.BlockSpec((1, HEADS, HEAD_DIM, HEAD_DIM), lambda b, n: (b, 0, 0, 0))
    return pl.pallas_call(
        functools.partial(_hgrn_kernel, c=c, nch=nch, layer=layer),
        grid=(bsz, t // tb),
        in_specs=[col(0), col(1), col(2), col(3),
                  _resident(lb_logits.shape), _resident((1, w)),
                  _resident(wall.shape), _resident(masks.shape), _resident(tri.shape), st],
        out_specs=[pl.BlockSpec((1, tb, w), lambda b, n: (b, n, 0)), st],
        out_shape=[jax.ShapeDtypeStruct((bsz, t, w), F32),
                   jax.ShapeDtypeStruct(s0.shape, F32)],
        scratch_shapes=[pltpu.VMEM((HEADS, HEAD_DIM, HEAD_DIM), F32)],
        compiler_params=_cparams("arbitrary", "arbitrary"),
        name="hgrn_long",
    )(hgp, hgp, hgp, hgp, lb_logits, g_out.reshape(1, w),
      jnp.asarray(wall, BF16), jnp.asarray(masks, F32), jnp.asarray(tri, BF16), s0)


def _gdn_prepare(qs, ks, v, glog, beta, tri_bf, c, nch):
    nb = nch * HEADS
    wc = HEADS * c
    rows = lambda x, ch: x[ch * c:(ch + 1) * c]
    per = lambda f: jnp.stack([f(ch, h) for ch in range(nch) for h in range(HEADS)], axis=0)
    q = per(lambda ch, h: rows(qs[h], ch))
    k = per(lambda ch, h: rows(ks[h], ch))
    vv = per(lambda ch, h: rows(v, ch)[:, _head(h)])
    beta_b = per(lambda ch, h: jnp.broadcast_to(rows(beta, ch)[:, HEADS + h:HEADS + h + 1], (c, LANES)))
    lane = lax.broadcasted_iota(jnp.int32, (c, LANES), 1)
    g_small = None
    for ch in range(nch):
        part = jnp.where(lane < HEADS, rows(glog, ch), 0.0)
        part = part if ch == 0 else pltpu.roll(part, ch * HEADS, axis=1)
        g_small = part if g_small is None else g_small + part
    gcum = _dot_exact_lhs(tri_bf, g_small)
    gcum_t = jnp.concatenate([gcum, jnp.zeros((LANES - c, LANES), F32)], axis=0).T
    g = jnp.stack([jnp.broadcast_to(gcum[:, n:n + 1], (c, LANES)) for n in range(nb)], axis=0)
    packed = lambda f: jnp.stack([jnp.concatenate([f(ch * HEADS + h) for h in range(HEADS)], axis=1)
                                  for ch in range(nch)], axis=0)
    g_col = packed(lambda n: jnp.broadcast_to(gcum[:, n:n + 1], (c, c)))
    g_row = packed(lambda n: jnp.broadcast_to(gcum_t[n:n + 1, :c], (c, c)))
    ti = lax.broadcasted_iota(jnp.int32, (nch, c, wc), 1)
    si = lax.broadcasted_iota(jnp.int32, (nch, c, wc), 2) & (c - 1)
    decay = jnp.exp(jnp.minimum(g_col - g_row, 0.0))
    kb = k * beta_b
    heads_on_lanes = lambda x: jnp.stack(
        [jnp.concatenate([x[ch * HEADS + h] for h in range(HEADS)], axis=1) for ch in range(nch)], axis=0)
    kbq = _bf(jnp.concatenate([heads_on_lanes(kb), heads_on_lanes(q)], axis=1))
    kq = _bmm_nt(kbq, _block_diag(_bf(heads_on_lanes(k)), HEADS))
    m = jnp.where(ti > si, kq[:, :c] * decay, 0.0)
    aqk = jnp.where(ti > si, kq[:, c:] * decay, 0.0)
    qk_diag = jnp.sum(q * k, axis=-1, keepdims=True)
    eg = jnp.exp(g)
    rhs = jnp.concatenate([vv * beta_b, kb * eg], axis=-1)
    npow = -m
    poff = npow
    npow_bd = _block_diag(_bf(npow), HEADS)
    for _ in range(int(math.log2(c)) - 1):
        npow = _bmm(_bf(npow), npow_bd)
        npow_bd = _block_diag(_bf(npow), HEADS)
        poff = poff + npow + _bmm(_bf(poff), npow_bd)
    xoff = _bmm(_block_diag(_bf(poff), HEADS), _bf(rhs.reshape(nch, wc, 2 * HEAD_DIM)))
    x = rhs + xoff.reshape(nb, c, 2 * HEAD_DIM)
    u, w = x[:, :, :HEAD_DIM], x[:, :, HEAD_DIM:]
    gc = g[:, c - 1:c, :]
    return (u, _bf(jnp.concatenate([w, q * eg], axis=1)), _block_diag(_bf(aqk), HEADS), qk_diag,
            _bf(k * jnp.exp(gc - g)), jnp.exp(gc))


def _gdn_kernel(qkv_ref, z_ref, ab_ref, buf_ref, wconv_ref, alog_ref, dtb_ref, gout_ref, tri_ref,
                s0_ref, o_ref, sn_ref, s_scr, ext_scr, *, c, nch):
    n = pl.program_id(1)
    tb = c * nch

    @pl.when(n == 0)
    def _():
        s_scr[...] = s0_ref[0]
        ext_scr[0:SUBLANES - 3, :] = jnp.zeros((SUBLANES - 3, ext_scr.shape[1]), F32)
        ext_scr[SUBLANES - 3:SUBLANES, :] = buf_ref[0]

    @pl.when(n > 0)
    def _():
        ext_scr[0:SUBLANES, :] = ext_scr[tb:tb + SUBLANES, :]

    ext_scr[SUBLANES:SUBLANES + tb, :] = qkv_ref[0]
    full = ext_scr[...]
    wconv = wconv_ref[...]
    acc = full * wconv[CONV_W - 1:CONV_W, :]
    for j in range(CONV_W - 1):
        acc = acc + pltpu.roll(full, CONV_W - 1 - j, axis=0) * wconv[j:j + 1, :]
    qs, ks, v, glog, beta = _gdn_inputs(acc[SUBLANES:SUBLANES + tb], ab_ref[0], alog_ref[...], dtb_ref[...], None)
    u, wq, aqk_bd, qk_diag, kg, egc = _gdn_prepare(qs, ks, v, glog, beta, tri_ref[...], c, nch)
    gout = gout_ref[...]
    s = s_scr[...]
    for ch in range(nch):
        sel = slice(ch * HEADS, (ch + 1) * HEADS)
        ws_qs = _bmm(wq[sel], _bf(s))
        v_new = u[sel] - ws_qs[:, :c]
        intra = _dot(aqk_bd[ch], _bf(v_new.reshape(HEADS * c, HEAD_DIM)))
        o = ws_qs[:, c:] + intra.reshape(HEADS, c, HEAD_DIM) + qk_diag[sel] * v_new
        s = egc[sel] * s + _bmm_tn(kg[sel], _bf(v_new))
        for h in range(HEADS):
            sl = _head(h)
            o_ref[0, ch * c:(ch + 1) * c, sl] = _rms(o[h], gout) * jax.nn.silu(z_ref[0, ch * c:(ch + 1) * c, sl])
    s_scr[...] = s

    @pl.when(n == pl.num_programs(1) - 1)
    def _():
        sn_ref[0] = s


def _pad_lanes(x):
    return jnp.zeros((1, LANES), F32).at[0, :x.shape[0]].set(x.astype(F32))


def _gdn_long(gqkv, gz, ab, conv_buf, w_conv, a_log, dt_bias, g_out, s0):
    bsz, t, wq = gqkv.shape
    c = GDN_CHUNK
    nch = GDN_CHUNKS_PER_STEP
    tb = c * nch
    assert t % tb == 0
    w = HEADS * HEAD_DIM
    tri = np.tril(np.ones((c, c), np.float32))
    st = pl.BlockSpec((1, HEADS, HEAD_DIM, HEAD_DIM), lambda b, n: (b, 0, 0, 0))
    return pl.pallas_call(
        functools.partial(_gdn_kernel, c=c, nch=nch),
        grid=(bsz, t // tb),
        in_specs=[pl.BlockSpec((1, tb, wq), lambda b, n: (b, n, 0)),
                  pl.BlockSpec((1, tb, w), lambda b, n: (b, n, 0)),
                  pl.BlockSpec((1, tb, LANES), lambda b, n: (b, n, 0)),
                  pl.BlockSpec((1, CONV_W - 1, wq), lambda b, n: (b, 0, 0)),
                  _resident((CONV_W, wq)), _resident((1, LANES)), _resident((1, LANES)),
                  _resident((1, HEAD_DIM)), _resident((c, c)), st],
        out_specs=[pl.BlockSpec((1, tb, w), lambda b, n: (b, n, 0)), st],
        out_shape=[jax.ShapeDtypeStruct((bsz, t, w), F32),
                   jax.ShapeDtypeStruct((bsz, HEADS, HEAD_DIM, HEAD_DIM), F32)],
        scratch_shapes=[pltpu.VMEM((HEADS, HEAD_DIM, HEAD_DIM), F32),
                        pltpu.VMEM((tb + SUBLANES, wq), F32)],
        compiler_params=_cparams("arbitrary", "arbitrary"),
        name="gdn_long",
    )(gqkv, gz, ab, conv_buf, w_conv,
      _pad_lanes(a_log), _pad_lanes(dt_bias),
      g_out.reshape(1, HEAD_DIM), jnp.asarray(tri, BF16), s0)


def _shift_rows(x, d):
    return x if d == 0 else pltpu.roll(x, d, axis=1)


def _cumsum_rows8(x, rows):
    for d in (1, 2, 4):
        x = x + jnp.where(rows >= d, pltpu.roll(x, d, axis=1), 0.0)
    return x


def _hgrn_short(q, hf, v, lb, s, rows, t_valid):
    f = lb + (1.0 - lb) * jax.nn.sigmoid(hf)
    valid = rows < t_valid
    logf = jnp.where(valid, jnp.log(f), 0.0)
    k = jnp.where(valid, 1.0 - f, 0.0)
    g = _cumsum_rows8(logf, rows)
    gc = g[:, SUBLANES - 1:SUBLANES, :]
    o = _bmm(_bf(q * jnp.exp(g)), _bf(s))
    for d in range(t_valid):
        ok = rows >= d
        dec = jnp.exp(jnp.where(ok, g - _shift_rows(g, d), 0.0))
        a = jnp.sum(jnp.where(ok, q * _shift_rows(k, d) * dec, 0.0), axis=-1, keepdims=True)
        o = o + a * _shift_rows(v, d)
    s_new = _col_to_matrix_mxu(jnp.exp(gc)) * s + _bmm_tn(_bf(k * jnp.exp(gc - g)), _bf(v))
    return o, s_new


def _gdn_short(q, k, v, g_b, beta_b, s, rows, t_valid):
    g = _cumsum_rows8(g_b, rows)
    gc = g[:, SUBLANES - 1:SUBLANES, :]
    eg = jnp.exp(g)
    kb = k * beta_b
    dec = [None] + [jnp.exp(jnp.where(rows >= d, g - _shift_rows(g, d), 0.0)) for d in range(1, t_valid)]
    mcol = [None] + [jnp.where(rows >= d, jnp.sum(kb * _shift_rows(k, d), axis=-1, keepdims=True) * dec[d], 0.0)
                     for d in range(1, t_valid)]
    ru, rw = v * beta_b, kb * eg
    xu, xw = ru, rw
    for _ in range(t_valid - 1):
        nu, nw = ru, rw
        for d in range(1, t_valid):
            nu = nu - mcol[d] * _shift_rows(xu, d)
            nw = nw - mcol[d] * _shift_rows(xw, d)
        xu, xw = nu, nw
    ws_qs = _bmm(_bf(jnp.concatenate([xw, q * eg], axis=1)), _bf(s))
    v_new = xu - ws_qs[:, :SUBLANES]
    o = ws_qs[:, SUBLANES:] + jnp.sum(q * k, axis=-1, keepdims=True) * v_new
    for d in range(1, t_valid):
        a = jnp.where(rows >= d, jnp.sum(q * _shift_rows(k, d), axis=-1, keepdims=True) * dec[d], 0.0)
        o = o + a * _shift_rows(v_new, d)
    s_new = jnp.exp(gc) * s + _bmm_tn(_bf(k * jnp.exp(gc - g)), _bf(v_new))
    return o, s_new


def _short_kernel(hq_ref, hf_ref, hi_ref, hgate_ref, qkv_ref, z_ref, ab_ref, buf_ref,
                  lbl_ref, hgout_ref, wconv_ref, alog_ref, dtb_ref, ggout_ref, shg0_ref, sgd0_ref,
                  ohg_ref, ogd_ref, shg_ref, sgd_ref, ext_scr, *, t_valid, layer):
    bb = hq_ref.shape[0]
    rows = lax.broadcasted_iota(jnp.int32, (bb, SUBLANES, LANES), 1)
    lb = _forget_lower_bound(lbl_ref[...], layer)
    hgout = hgout_ref[...]
    ggout = ggout_ref[...]
    wconv = wconv_ref[...]

    def tile8(x):
        if t_valid == SUBLANES:
            return x
        return jnp.concatenate([x, jnp.zeros((bb, SUBLANES - t_valid, x.shape[2]), x.dtype)], axis=1)

    ext_scr[:, SUBLANES - 3:SUBLANES, :] = buf_ref[...]
    ext_scr[:, SUBLANES:2 * SUBLANES, :] = tile8(qkv_ref[...])
    acc = ext_scr[:, pl.ds(SUBLANES - 3, SUBLANES), :] * wconv[0:1, :]
    for j in range(1, CONV_W):
        acc = acc + ext_scr[:, pl.ds(SUBLANES - 3 + j, SUBLANES), :] * wconv[j:j + 1, :]
    qs, ks, v, glog, beta = _gdn_inputs(acc, tile8(ab_ref[...]), alog_ref[...], dtb_ref[...], t_valid)
    for h in range(HEADS):
        sl = _head(h)
        o, s_new = _hgrn_short(tile8(hq_ref[:, :, sl]), tile8(hf_ref[:, :, sl]), tile8(hi_ref[:, :, sl]),
                               lb[:, sl], shg0_ref[:, h], rows, t_valid)
        shg_ref[:, h] = s_new
        ohg_ref[:, :, sl] = (_rms(o, hgout[:, sl])[:, :t_valid]) * jax.nn.silu(hgate_ref[:, :, sl])
        g_b = jnp.broadcast_to(glog[:, :, h:h + 1], rows.shape)
        beta_b = jnp.broadcast_to(beta[:, :, HEADS + h:HEADS + h + 1], rows.shape)
        o, s_new = _gdn_short(qs[h], ks[h], v[:, :, sl], g_b, beta_b, sgd0_ref[:, h], rows, t_valid)
        sgd_ref[:, h] = s_new
        ogd_ref[:, :, sl] = (_rms(o, ggout)[:, :t_valid]) * jax.nn.silu(z_ref[:, :, sl])


def _mixers_short(hgp, gqkv, gz, ab, conv_buf, lb_logits, hg_out, w_conv, a_log, dt_bias, gdn_out,
                  s_hg0, s_gdn0, t_valid, layer):
    bsz, t, wq = gqkv.shape
    assert t == t_valid and CONV_W - 1 <= t_valid <= SUBLANES
    bb = DECODE_BLOCK
    assert bsz % bb == 0
    w = HEADS * HEAD_DIM
    col = lambda j: pl.BlockSpec((bb, t, w), lambda b, j=j: (b, 0, j))
    tok = lambda width: pl.BlockSpec((bb, t, width), lambda b: (b, 0, 0))
    st = pl.BlockSpec((bb, HEADS, HEAD_DIM, HEAD_DIM), lambda b: (b, 0, 0, 0))
    return pl.pallas_call(
        functools.partial(_short_kernel, t_valid=t_valid, layer=layer),
        grid=(bsz // bb,),
        in_specs=[col(0), col(1), col(2), col(3), tok(wq), tok(w), tok(LANES),
                  pl.BlockSpec((bb, CONV_W - 1, wq), lambda b: (b, 0, 0)),
                  _resident(lb_logits.shape), _resident((1, w)), _resident((CONV_W, wq)),
                  _resident((1, LANES)), _resident((1, LANES)), _resident((1, HEAD_DIM)), st, st],
        out_specs=[tok(w), tok(w), st, st],
        out_shape=[jax.ShapeDtypeStruct((bsz, t, w), F32), jax.ShapeDtypeStruct((bsz, t, w), F32),
                   jax.ShapeDtypeStruct(s_hg0.shape, F32), jax.ShapeDtypeStruct(s_gdn0.shape, F32)],
        scratch_shapes=[pltpu.VMEM((bb, 2 * SUBLANES, wq), F32)],
        compiler_params=_cparams("arbitrary"),
        name="mixers_short",
    )(hgp, hgp, hgp, hgp, gqkv, gz, ab, conv_buf, lb_logits, hg_out.reshape(1, w), w_conv,
      _pad_lanes(a_log), _pad_lanes(dt_bias), gdn_out.reshape(1, HEAD_DIM), s_hg0, s_gdn0)


def _mem_attn_kernel(q_ref, k_ref, v_ref, o_ref):
    for h in range(HEADS):
        sl = _head(h)
        s = _bmm_nt(_bf(q_ref[:, :, sl]), _bf(k_ref[:, :, sl])) * (HEAD_DIM ** -0.5)
        e = jnp.exp(s - jnp.max(s, axis=-1, keepdims=True))
        o_ref[:, :, sl] = _bmm(_bf(e), _bf(v_ref[:, :, sl])) / jnp.sum(e, axis=-1, keepdims=True)


def _mem_attn(mq, mem_k, mem_v, bb, tq):
    bsz, t, w = mq.shape
    m = mem_k.shape[1]
    assert bsz % bb == 0 and t % tq == 0
    kv = pl.BlockSpec((bb, m, w), lambda b, n: (b, 0, 0))
    return pl.pallas_call(
        _mem_attn_kernel,
        grid=(bsz // bb, t // tq),
        in_specs=[pl.BlockSpec((bb, tq, w), lambda b, n: (b, n, 0)), kv, kv],
        out_specs=pl.BlockSpec((bb, tq, w), lambda b, n: (b, n, 0)),
        out_shape=jax.ShapeDtypeStruct((bsz, t, w), F32),
        compiler_params=_cparams("arbitrary", "arbitrary"),
        name="mem_attn",
    )(mq, mem_k, mem_v)


def _mem_attn_cache_kernel(q_ref, k_hbm, v_hbm, o_ref, kbuf, vbuf, sem):
    i = pl.program_id(0)
    bb = q_ref.shape[0]

    def copies(step, slot):
        out = []
        for h in range(HEADS):
            for j, (src, dst) in enumerate(((k_hbm, kbuf), (v_hbm, vbuf))):
                out.append(pltpu.make_async_copy(src.at[pl.ds(step * bb, bb), :, h, :],
                                                 dst.at[slot, h], sem.at[j, slot, h]))
        return out

    slot = i % 2

    @pl.when(i == 0)
    def _():
        for cp in copies(0, 0):
            cp.start()

    @pl.when(i + 1 < pl.num_programs(0))
    def _():
        for cp in copies(i + 1, 1 - slot):
            cp.start()

    for cp in copies(i, slot):
        cp.wait()
    t = q_ref.shape[1]
    for h in range(HEADS):
        sl = _head(h)
        q = q_ref[:, :, sl]
        if t < SUBLANES:
            q = jnp.concatenate([q, jnp.zeros((bb, SUBLANES - t, HEAD_DIM), q.dtype)], axis=1)
        s = _bmm_nt(_bf(q), _bf(kbuf[slot, h])) * (HEAD_DIM ** -0.5)
        e = jnp.exp(s - jnp.max(s, axis=-1, keepdims=True))
        o = _bmm(_bf(e), _bf(vbuf[slot, h])) / jnp.sum(e, axis=-1, keepdims=True)
        o_ref[:, :, sl] = o[:, :t]


def _mem_attn_cache(mq, cache_k, cache_v, bb):
    bsz, t, w = mq.shape
    m = cache_k.shape[1]
    assert bsz % bb == 0 and cache_k.shape == (bsz, m, HEADS, HEAD_DIM)
    return pl.pallas_call(
        _mem_attn_cache_kernel,
        grid=(bsz // bb,),
        in_specs=[pl.BlockSpec((bb, t, w), lambda b: (b, 0, 0)),
                  pl.BlockSpec(memory_space=pl.ANY), pl.BlockSpec(memory_space=pl.ANY)],
        out_specs=pl.BlockSpec((bb, t, w), lambda b: (b, 0, 0)),
        out_shape=jax.ShapeDtypeStruct((bsz, t, w), F32),
        scratch_shapes=[pltpu.VMEM((2, HEADS, bb, m, HEAD_DIM), F32),
                        pltpu.VMEM((2, HEADS, bb, m, HEAD_DIM), F32),
                        pltpu.SemaphoreType.DMA((2, 2, HEADS))],
        compiler_params=_cparams("arbitrary"),
        name="mem_attn_cache",
    )(mq, cache_k, cache_v)


def _merge_ffn_kernel(x_ref, ohg_ref, ogd_ref, omem_ref, gpre_ref, wgates_ref,
                      wbh_ref, wbg_ref, wbm_ref, wout_ref, gpm_ref, gpf_ref,
                      wffn_ref, wdown_ref, gpo_ref, y_ref):
    d = x_ref.shape[-1]
    x = x_ref[...]
    xn = _bf(_rms(x, gpre_ref[...]))
    merged = None
    for j, (o_ref, wb_ref) in enumerate(((ohg_ref, wbh_ref), (ogd_ref, wbg_ref), (omem_ref, wbm_ref))):
        gate = jax.nn.sigmoid(_dot(xn, wgates_ref[:, j * d:(j + 1) * d]))
        term = gate * _dot(_bf(o_ref[...]), wb_ref[...])
        merged = term if merged is None else merged + term
    h = x + _rms(_dot(_bf(merged), wout_ref[...]), gpm_ref[...])
    hn = _bf(_rms(h, gpf_ref[...]))
    hidden = wdown_ref.shape[0]
    ff = None
    for c0 in range(0, hidden, FFN_CHUNK):
        a = (jax.nn.silu(_dot(hn, wffn_ref[:, c0:c0 + FFN_CHUNK]))
             * _dot(hn, wffn_ref[:, hidden + c0:hidden + c0 + FFN_CHUNK]))
        p = _dot(_bf(a), wdown_ref[c0:c0 + FFN_CHUNK, :])
        ff = p if ff is None else ff + p
    y_ref[...] = h + _rms(ff, gpo_ref[...])


def _merge_ffn(x2d, ohg, ogd, omem, gpre, wgates, wbh, wbg, wbm, wout, gpm, gpf, wffn, wdown, gpo):
    n, d = x2d.shape
    tm = min(TOKEN_TILE, n)
    hidden = wdown.shape[0]
    assert n % tm == 0 and hidden % FFN_CHUNK == 0 and wffn.shape[1] == 2 * hidden
    assert wgates.shape == (d, N_BRANCH * d)
    tok = lambda width: pl.BlockSpec((tm, width), lambda i: (i, 0))
    row = lambda a: a.reshape(1, d)
    return pl.pallas_call(
        _merge_ffn_kernel,
        grid=(n // tm,),
        in_specs=[tok(d), tok(ohg.shape[1]), tok(ogd.shape[1]), tok(omem.shape[1]),
                  _resident((1, d)), _resident(wgates.shape),
                  _resident(wbh.shape), _resident(wbg.shape), _resident(wbm.shape), _resident(wout.shape),
                  _resident((1, d)), _resident((1, d)),
                  _resident(wffn.shape), _resident(wdown.shape), _resident((1, d))],
        out_specs=tok(d),
        out_shape=jax.ShapeDtypeStruct((n, d), F32),
        compiler_params=_cparams("arbitrary"),
        name="merge_ffn",
    )(x2d, ohg, ogd, omem, row(gpre), wgates, wbh, wbg, wbm, wout, row(gpm), row(gpf), wffn, wdown, row(gpo))


def _pack_w_in_kernel(wt_ref, mix_ref, gates_ref):
    w = HEADS * HEAD_DIM
    c_ab = 8 * w
    c_mq = c_ab + 2 * HEADS
    c_gt = c_mq + w

    def put(dst_ref, dst0, src0, n):
        for j in range(0, n, w):
            nb = min(w, n - j)
            dst_ref[:, dst0 + j:dst0 + j + nb] = _bf(wt_ref[pl.ds(src0 + j, nb), :].T)

    put(mix_ref, 0, 0, c_ab)
    put(mix_ref, c_ab, c_mq, w)
    ab = jnp.concatenate([wt_ref[pl.ds(c_ab, 2 * HEADS), :],
                          jnp.zeros((LANES - 2 * HEADS, wt_ref.shape[1]), F32)], axis=0)
    mix_ref[:, c_ab + w:c_ab + w + LANES] = _bf(ab.T)
    put(gates_ref, 0, c_gt, gates_ref.shape[1])


def _pack_w_in(w_in_t):
    total, d = w_in_t.shape
    w = HEADS * HEAD_DIM
    widths = (4 * w, 3 * w, w, w, LANES)
    n_gates = total - (9 * w + 2 * HEADS)
    assert n_gates % LANES == 0 and d % LANES == 0
    mixers, gates = pl.pallas_call(
        _pack_w_in_kernel,
        out_shape=[jax.ShapeDtypeStruct((d, sum(widths)), BF16), jax.ShapeDtypeStruct((d, n_gates), BF16)],
        compiler_params=pltpu.CompilerParams(vmem_limit_bytes=VMEM_LIMIT),
        name="pack_w_in",
    )(w_in_t)
    return mixers, widths, gates


def _of_layer(a, l):
    return a.reshape(a.shape[1:]) if a.shape[0] == 1 else a[l]


def _layer(x, mem_k, mem_v, conv_buf, s_hg, s_gdn, layer, lb_logits, p):
    bsz, t, d = x.shape
    t_valid = t
    short = t <= SUBLANES
    x2d = x.reshape(bsz * t, d)
    w = HEADS * HEAD_DIM

    hgp, gqkv, gz, mq, ab = _norm_proj(x2d, p["g_pre_mix"], p["w_in"], p["w_in_widths"], "in_proj")
    as3d = lambda a: a.reshape(bsz, t, a.shape[-1])
    gqkv3 = as3d(gqkv)
    if short:
        o_hg, o_gdn, s_hg_new, s_gdn_new = _mixers_short(
            as3d(hgp), gqkv3, as3d(gz), as3d(ab), conv_buf, lb_logits, p["g_hg_out"], p["w_conv"],
            p["a_log"], p["dt_bias"], p["g_gdn_out"], s_hg, s_gdn, t_valid, layer)
        o_mem = _mem_attn_cache(as3d(mq), mem_k, mem_v, ATTN_DECODE_BLOCK)
    else:
        o_hg, s_hg_new = _hgrn_long(as3d(hgp), lb_logits, p["g_hg_out"], s_hg, layer)
        o_gdn, s_gdn_new = _gdn_long(gqkv3, as3d(gz), as3d(ab), conv_buf, p["w_conv"], p["a_log"],
                                     p["dt_bias"], p["g_gdn_out"], s_gdn)
        o_mem = _mem_attn(as3d(mq), mem_k, mem_v, 1, min(ATTN_TILE, t))
    assert t_valid >= CONV_W - 1
    conv_new = gqkv3[:, t_valid - (CONV_W - 1):t_valid, :]

    flat = lambda a: a.reshape(bsz * t, w)
    y = _merge_ffn(x2d, flat(o_hg), flat(o_gdn), flat(o_mem), p["g_pre_mix"], p["w_gates"],
                   p["w_br_hg"], p["w_br_gdn"], p["w_br_mem"], p["w_out"], p["g_post_mix"],
                   p["g_pre_ffn"], p["w_ffn_in"], p["w_ffn_out"], p["g_post_ffn"])
    return y.reshape(bsz, t, d), conv_new, s_hg_new, s_gdn_new


def kernel(x_prompt, x_sample, mem_prompt, cache_mem_k, cache_mem_v, state_hgrn, state_gdn, state_gdn_conv, hg_lb_logits, g_pre_mix, w_in, w_conv, a_log, dt_bias, g_hg_out, g_gdn_out, g_mem, w_mem_kv, w_br_hg, w_br_gdn, w_br_mem, w_out, g_post_mix, g_pre_ffn, w_ffn_in, w_ffn_out, g_post_ffn):
    depth = w_in.shape[0]
    bp, _, d = x_prompt.shape
    m = mem_prompt.shape[1]
    w = HEADS * HEAD_DIM
    yp, ys = x_prompt, x_sample
    outs = [[] for _ in range(8)]
    for l in range(depth):
        w_in_packed, widths, w_gates = _pack_w_in(_of_layer(w_in, l).T)
        p = dict(g_pre_mix=g_pre_mix[l], w_in=w_in_packed, w_in_widths=widths, w_gates=w_gates,
                 w_conv=w_conv[l],
                 a_log=a_log[l], dt_bias=dt_bias[l], g_hg_out=g_hg_out[l], g_gdn_out=g_gdn_out[l],
                 w_br_hg=_bf(w_br_hg[l]), w_br_gdn=_bf(w_br_gdn[l]), w_br_mem=_bf(w_br_mem[l]),
                 w_out=_bf(w_out[l]), g_post_mix=g_post_mix[l], g_pre_ffn=g_pre_ffn[l],
                 w_ffn_in=_bf(w_ffn_in[l]), w_ffn_out=_bf(w_ffn_out[l]), g_post_ffn=g_post_ffn[l])
        mk, mv = _norm_proj(mem_prompt.reshape(bp * m, d), g_mem[l], _bf(w_mem_kv[l]), (w, w), "mem_kv")
        mk, mv = mk.reshape(bp, m, w), mv.reshape(bp, m, w)
        zeros_state = jnp.zeros((bp, HEADS, HEAD_DIM, HEAD_DIM), F32)
        yp, cb, sh, sg = _layer(yp, mk, mv, jnp.zeros((bp, CONV_W - 1, 3 * w), F32),
                                zeros_state, zeros_state, l, hg_lb_logits, p)
        bs = x_sample.shape[0]
        ys, cb2, sh2, sg2 = _layer(ys, _of_layer(cache_mem_k, l), _of_layer(cache_mem_v, l),
                                   _of_layer(state_gdn_conv, l), _of_layer(state_hgrn, l),
                                   _of_layer(state_gdn, l), l, hg_lb_logits, p)
        for lst, val in zip(outs, (mk.reshape(bp, m, HEADS, HEAD_DIM), mv.reshape(bp, m, HEADS, HEAD_DIM),
                                   sh, sg, cb, sh2, sg2, cb2)):
            lst.append(val)
    return (yp, ys) + tuple(o[0].reshape((1,) + o[0].shape) if depth == 1 else jnp.stack(o) for o in outs)
```

```python
import functools
import math

import numpy as np
import jax
import jax.numpy as jnp
from jax import lax
from jax.experimental import pallas as pl
from jax.experimental.pallas import tpu as pltpu

F32 = jnp.float32
BF16 = jnp.bfloat16

EPS = 1e-6
HEADS = 4
HEAD_DIM = 128
CONV_W = 4
N_BRANCH = 3

LANES = 128
SUBLANES = 8
VMEM_LIMIT = 56 * 1024 * 1024

SAFE_LOG_RANGE = 70.0
HG_CHUNK = 128
HG_BOUNDED_CHUNK = 64
HG_CHUNKS_PER_STEP = 8
GDN_CHUNK = 64
GDN_CHUNKS_PER_STEP = 16
TOKEN_TILE = 512
ATTN_TILE = 1024
FFN_CHUNK = 256
DECODE_BLOCK = 16
ATTN_DECODE_BLOCK = 8


def _cparams(*sem):
    return pltpu.CompilerParams(dimension_semantics=sem, vmem_limit_bytes=VMEM_LIMIT)


def _resident(shape):
    nd = len(shape)
    return pl.BlockSpec(shape, lambda *_: (0,) * nd, pipeline_mode=pl.Buffered(1))


def _bf(x):
    return x.astype(BF16)


def _dot(a, b):
    return jnp.dot(a, b, preferred_element_type=F32)


def _dot_nt(a, b):
    return lax.dot_general(a, b, (((1,), (1,)), ((), ())), preferred_element_type=F32)


def _dot_tn(a, b):
    return lax.dot_general(a, b, (((0,), (0,)), ((), ())), preferred_element_type=F32)


def _bmm(a, b):
    return jnp.einsum("nij,njk->nik", a, b, preferred_element_type=F32)


def _bmm_nt(a, b):
    return jnp.einsum("nid,njd->nij", a, b, preferred_element_type=F32)


def _bmm_tn(a, b):
    return jnp.einsum("nci,ncj->nij", a, b, preferred_element_type=F32)


def _split2(x):
    hi = _bf(x)
    lo = _bf(x - hi.astype(F32))
    return hi, lo


def _dot_exact_lhs(w_bf16, x):
    hi, lo = _split2(x)
    return _dot(w_bf16, hi) + _dot(w_bf16, lo)


def _rms(x, g):
    return x * lax.rsqrt(jnp.mean(x * x, axis=-1, keepdims=True) + EPS) * g


def _head(h):
    return slice(h * HEAD_DIM, (h + 1) * HEAD_DIM)


def _block_diag(x, nblk):
    wblk = x.shape[-1] // nblk
    if wblk % LANES == 0:
        zeros = lambda n: [jnp.zeros(x.shape[:-1] + (n * wblk,), x.dtype)] if n else []
        parts = [jnp.concatenate(zeros(j) + [x[..., j * wblk:(j + 1) * wblk]] + zeros(nblk - 1 - j), axis=-1)
                 for j in range(nblk)]
    else:
        lane = lax.broadcasted_iota(jnp.int32, x.shape, x.ndim - 1)
        parts = [jnp.where((lane >= j * wblk) & (lane < (j + 1) * wblk), x, jnp.zeros_like(x))
                 for j in range(nblk)]
    return jnp.concatenate(parts, axis=-2)


def _norm_proj_kernel(x_ref, g_ref, w_ref, *out_refs):
    xn = _bf(_rms(x_ref[...], g_ref[...]))
    off = 0
    for o_ref in out_refs:
        width = o_ref.shape[-1]
        for c0 in range(0, width, 512):
            cw = min(512, width - c0)
            o_ref[:, c0:c0 + cw] = _dot(xn, w_ref[:, off + c0:off + c0 + cw])
        off += width


def _norm_proj(x2d, g, w_bf16, widths, name):
    n, d = x2d.shape
    tm = min(TOKEN_TILE, n)
    assert n % tm == 0 and sum(widths) == w_bf16.shape[1]
    return pl.pallas_call(
        _norm_proj_kernel,
        grid=(n // tm,),
        in_specs=[pl.BlockSpec((tm, d), lambda i: (i, 0)),
                  _resident((1, d)),
                  _resident(w_bf16.shape)],
        out_specs=[pl.BlockSpec((tm, w), lambda i: (i, 0)) for w in widths],
        out_shape=[jax.ShapeDtypeStruct((n, w), F32) for w in widths],
        compiler_params=_cparams("arbitrary"),
        name=name,
    )(x2d, g.reshape(1, d), w_bf16)


def _forget_lower_bound(logits, layer):
    m = jnp.max(logits, axis=0, keepdims=True)
    e = jnp.exp(logits - m)
    return jnp.sum(e[:layer + 1], axis=0, keepdims=True) / jnp.sum(e, axis=0, keepdims=True)


def _col_to_matrix_mxu(row):
    b, _, n = row.shape
    tile = jnp.concatenate([row, jnp.zeros((b, SUBLANES - 1, n), row.dtype)], axis=1)
    hi = _bf(tile)
    mid = _bf(tile - hi.astype(F32))
    lo = _bf(tile - hi.astype(F32) - mid.astype(F32))
    ones = jnp.ones((b, SUBLANES, n), BF16)
    return _bmm_tn(hi, ones) + _bmm_tn(mid, ones) + _bmm_tn(lo, ones)


def _gdn_inputs(acc, ab, alog, dtb, t_valid):
    qkv = jax.nn.silu(acc)
    w = HEADS * HEAD_DIM
    qs, ks = [], []
    for h in range(HEADS):
        qh = qkv[..., h * HEAD_DIM:(h + 1) * HEAD_DIM]
        kh = qkv[..., w + h * HEAD_DIM:w + (h + 1) * HEAD_DIM]
        qs.append(qh * lax.rsqrt(jnp.sum(qh * qh, axis=-1, keepdims=True) + EPS) * (HEAD_DIM ** -0.5))
        ks.append(kh * lax.rsqrt(jnp.sum(kh * kh, axis=-1, keepdims=True) + EPS))
    v = qkv[..., 2 * w:3 * w]
    pre = ab + dtb
    softplus = jnp.maximum(pre, 0.0) + jnp.log(1.0 + jnp.exp(-jnp.abs(pre)))
    glog = -jnp.exp(alog) * softplus
    beta = jax.nn.sigmoid(ab)
    if t_valid is not None:
        valid = lax.broadcasted_iota(jnp.int32, ab.shape, ab.ndim - 2) < t_valid
        glog = jnp.where(valid, glog, 0.0)
        beta = jnp.where(valid, beta, 0.0)
    return qs, ks, v, glog, beta


def _hier_tables(c):
    t = np.arange(c)
    u = np.arange(c)
    tri = (u[None, :] <= t[:, None]).astype(np.float32)
    rows, masks = [], []
    b = c // 2
    while b >= 1:
        ref = (t // (2 * b)) * 2 * b + b - 1
        rows.append(tri - (u[None, :] <= ref[:, None]).astype(np.float32))
        right = (t % (2 * b)) >= b
        left = (t % (2 * b)) < b
        same = (t[:, None] // (2 * b)) == (t[None, :] // (2 * b))
        masks.append((right[:, None] & left[None, :] & same).astype(np.float32))
        b //= 2
    rows.append(tri)
    return np.concatenate(rows, 0), np.stack(masks, 0)


def _hgrn_chunk(q, k, v, logf_hi, logf_lo, s, wall_ref, masks, c):
    nlev = len(masks)
    d = _dot(wall_ref[...], jnp.concatenate([logf_hi, logf_lo], axis=0))
    g = d[nlev * c:(nlev + 1) * c]
    gc = g[c - 1:c, :]
    qg = _bf(q * jnp.exp(g))
    kg = _bf(k * jnp.exp(gc - g))
    egc = jnp.exp(gc)
    vb = _bf(v)
    npair = HEADS // 2
    a = [None] * npair
    for l in range(nlev):
        e = jnp.exp(-jnp.abs(d[l * c:(l + 1) * c]))
        ql, kl = _bf(q * e), _bf(k * e)
        for pr in range(npair):
            pair = slice(2 * pr * HEAD_DIM, (2 * pr + 2) * HEAD_DIM)
            p = _dot_nt(ql[:, pair], _block_diag(kl[:, pair], 2))
            a[pr] = jnp.where(masks[l], p, 0.0) if a[pr] is None else jnp.where(masks[l], p, a[pr])
    qk = q * k
    outs, s_new = [], []
    for h in range(HEADS):
        sl = _head(h)
        a_h = a[h // 2][:, (h % 2) * c:(h % 2 + 1) * c]
        o = _dot_nt(qg[:, sl], _bf(s[h])) + _dot(_bf(a_h), vb[:, sl])
        outs.append(o + jnp.sum(qk[:, sl], axis=-1, keepdims=True) * v[:, sl])
        s_new.append(egc[:, sl] * s[h] + _dot_tn(vb[:, sl], kg[:, sl]))
    return outs, s_new


def _hgrn_chunk_bounded(q, k, v, g, s, c):
    gc = g[c - 1:c, :]
    qg = _bf(q * jnp.exp(g))
    kn = _bf(k * jnp.exp(-g))
    kg = _bf(k * jnp.exp(gc - g))
    egc = jnp.exp(gc)
    vb = _bf(v)
    ti = lax.broadcasted_iota(jnp.int32, (c, HEADS * c), 0)
    si = lax.broadcasted_iota(jnp.int32, (c, HEADS * c), 1) & (c - 1)
    a = jnp.where(ti > si, _dot_nt(qg, _block_diag(kn, HEADS)), 0.0)
    v_rows = jnp.concatenate([vb[:, _head(h)] for h in range(HEADS)], axis=0)
    intra = _dot(_block_diag(_bf(a), HEADS), v_rows)
    qk = q * k
    outs, s_new = [], []
    for h in range(HEADS):
        sl = _head(h)
        outs.append(_dot_nt(qg[:, sl], _bf(s[h])) + intra[h * c:(h + 1) * c]
                    + jnp.sum(qk[:, sl], axis=-1, keepdims=True) * v[:, sl])
        s_new.append(egc[:, sl] * s[h] + _dot_tn(vb[:, sl], kg[:, sl]))
    return outs, s_new


def _hgrn_kernel(q_ref, f_ref, i_ref, gate_ref, lbl_ref, gout_ref, wall_ref, mask_ref, tri_ref, s0_ref,
                 o_ref, sn_ref, s_scr, *, c, nch, layer):
    n = pl.program_id(1)

    @pl.when(n == 0)
    def _():
        for h in range(HEADS):
            s_scr[h] = s0_ref[0, h].T

    lb = _forget_lower_bound(lbl_ref[...], layer)
    gout = gout_ref[...]
    f = lb + (1.0 - lb) * jax.nn.sigmoid(f_ref[0])
    hi, lo = _split2(jnp.log(f))
    k = 1.0 - f
    nlev = mask_ref.shape[0]

    def emit(r, outs):
        for h in range(HEADS):
            sl = _head(h)
            o_ref[0, r, sl] = _bf(_rms(outs[h], gout[:, sl]) * jax.nn.silu(gate_ref[0, r, sl]))

    def run(chunk_len, chunk_fn):
        s = [s_scr[h] for h in range(HEADS)]
        for ch in range(nch * c // chunk_len):
            r = slice(ch * chunk_len, (ch + 1) * chunk_len)
            outs, s = chunk_fn(ch, r, s)
            emit(r, outs)
        for h in range(HEADS):
            s_scr[h] = s[h]

    cb = tri_ref.shape[0]
    half = lambda ch: slice(ch * cb, (ch + 1) * cb)
    gs = [_dot(tri_ref[...], jnp.concatenate([hi[half(ch)], lo[half(ch)]], axis=0))
          for ch in range(nch * c // cb)]
    gmin = functools.reduce(jnp.minimum, [jnp.min(g) for g in gs])
    bounded = gmin >= -SAFE_LOG_RANGE

    @pl.when(bounded)
    def _():
        run(cb, lambda ch, r, s: _hgrn_chunk_bounded(q_ref[0, r, :], k[r], i_ref[0, r, :], gs[ch], s, cb))

    @pl.when(jnp.logical_not(bounded))
    def _():
        masks = [mask_ref[l] != 0.0 for l in range(nlev)]
        run(c, lambda ch, r, s: _hgrn_chunk(q_ref[0, r, :], k[r], i_ref[0, r, :], hi[r], lo[r], s,
                                            wall_ref, masks, c))

    @pl.when(n == pl.num_programs(1) - 1)
    def _():
        for h in range(HEADS):
            sn_ref[0, h] = s_scr[h].T


def _hgrn_long(hgp, lb_logits, g_out, s0, layer):
    bsz, t, _ = hgp.shape
    c = HG_CHUNK
    nch = HG_CHUNKS_PER_STEP
    tb = c * nch
    assert t % tb == 0 and c == HEAD_DIM and HEADS % 2 == 0
    w = HEADS * HEAD_DIM
    wall, masks = _hier_tables(c)
    wall = np.concatenate([wall, wall], axis=1)
    masks = np.concatenate([masks, masks], axis=2)
    cb = HG_BOUNDED_CHUNK
    tri = np.tril(np.ones((cb, cb), np.float32))
    tri = np.concatenate([tri, tri], axis=1)
    col = lambda j: pl.BlockSpec((1, tb, w), lambda b, n, j=j: (b, n, j))
    st = pl.BlockSpec((1, HEADS, HEAD_DIM, HEAD_DIM), lambda b, n: (b, 0, 0, 0))
    return pl.pallas_call(
        functools.partial(_hgrn_kernel, c=c, nch=nch, layer=layer),
        grid=(bsz, t // tb),
        in_specs=[col(0), col(1), col(2), col(3),
                  _resident(lb_logits.shape), _resident((1, w)),
                  _resident(wall.shape), _resident(masks.shape), _resident(tri.shape), st],
        out_specs=[pl.BlockSpec((1, tb, w), lambda b, n: (b, n, 0)), st],
        out_shape=[jax.ShapeDtypeStruct((bsz, t, w), BF16),
                   jax.ShapeDtypeStruct(s0.shape, F32)],
        scratch_shapes=[pltpu.VMEM((HEADS, HEAD_DIM, HEAD_DIM), F32)],
        compiler_params=_cparams("arbitrary", "arbitrary"),
        name="hgrn_long",
    )(hgp, hgp, hgp, hgp, lb_logits, g_out.reshape(1, w),
      jnp.asarray(wall, BF16), jnp.asarray(masks, F32), jnp.asarray(tri, BF16), s0)


def _gdn_prepare(qs, ks, v, glog, beta, tri_bf, c, nch):
    nb = nch * HEADS
    wc = HEADS * c
    rows = lambda x, ch: x[ch * c:(ch + 1) * c]
    per = lambda f: jnp.stack([f(ch, h) for ch in range(nch) for h in range(HEADS)], axis=0)
    q = per(lambda ch, h: rows(qs[h], ch))
    k = per(lambda ch, h: rows(ks[h], ch))
    vv = per(lambda ch, h: rows(v, ch)[:, _head(h)])
    beta_b = per(lambda ch, h: jnp.broadcast_to(rows(beta, ch)[:, HEADS + h:HEADS + h + 1], (c, LANES)))
    lane = lax.broadcasted_iota(jnp.int32, (c, LANES), 1)
    g_small = None
    for ch in range(nch):
        part = jnp.where(lane < HEADS, rows(glog, ch), 0.0)
        part = part if ch == 0 else pltpu.roll(part, ch * HEADS, axis=1)
        g_small = part if g_small is None else g_small + part
    gcum = _dot_exact_lhs(tri_bf, g_small)
    gcum_t = jnp.concatenate([gcum, jnp.zeros((LANES - c, LANES), F32)], axis=0).T
    g = jnp.stack([jnp.broadcast_to(gcum[:, n:n + 1], (c, LANES)) for n in range(nb)], axis=0)
    packed = lambda f: jnp.stack([jnp.concatenate([f(ch * HEADS + h) for h in range(HEADS)], axis=1)
                                  for ch in range(nch)], axis=0)
    g_col = packed(lambda n: jnp.broadcast_to(gcum[:, n:n + 1], (c, c)))
    g_row = packed(lambda n: jnp.broadcast_to(gcum_t[n:n + 1, :c], (c, c)))
    ti = lax.broadcasted_iota(jnp.int32, (nch, c, wc), 1)
    si = lax.broadcasted_iota(jnp.int32, (nch, c, wc), 2) & (c - 1)
    decay = jnp.exp(jnp.minimum(g_col - g_row, 0.0))
    kb = k * beta_b
    heads_on_lanes = lambda x: jnp.stack(
        [jnp.concatenate([x[ch * HEADS + h] for h in range(HEADS)], axis=1) for ch in range(nch)], axis=0)
    kbq = _bf(jnp.concatenate([heads_on_lanes(kb), heads_on_lanes(q)], axis=1))
    kq = _bmm_nt(kbq, _block_diag(_bf(heads_on_lanes(k)), HEADS))
    m = jnp.where(ti > si, kq[:, :c] * decay, 0.0)
    aqk = jnp.where(ti > si, kq[:, c:] * decay, 0.0)
    qk_diag = jnp.sum(q * k, axis=-1, keepdims=True)
    eg = jnp.exp(g)
    rhs = jnp.concatenate([vv * beta_b, kb * eg], axis=-1)
    npow = -m
    poff = npow
    npow_bd = _block_diag(_bf(npow), HEADS)
    for _ in range(int(math.log2(c)) - 1):
        npow = _bmm(_bf(npow), npow_bd)
        npow_bd = _block_diag(_bf(npow), HEADS)
        poff = poff + npow + _bmm(_bf(poff), npow_bd)
    xoff = _bmm(_block_diag(_bf(poff), HEADS), _bf(rhs.reshape(nch, wc, 2 * HEAD_DIM)))
    x = rhs + xoff.reshape(nb, c, 2 * HEAD_DIM)
    u, w = x[:, :, :HEAD_DIM], x[:, :, HEAD_DIM:]
    gc = g[:, c - 1:c, :]
    return (u, _bf(jnp.concatenate([w, q * eg], axis=1)), _block_diag(_bf(aqk), HEADS), qk_diag,
            _bf(k * jnp.exp(gc - g)), jnp.exp(gc))


def _gdn_kernel(qkv_ref, z_ref, ab_ref, buf_ref, wconv_ref, alog_ref, dtb_ref, gout_ref, tri_ref,
                s0_ref, o_ref, sn_ref, s_scr, ext_scr, *, c, nch):
    n = pl.program_id(1)
    tb = c * nch

    @pl.when(n == 0)
    def _():
        s_scr[...] = s0_ref[0]
        ext_scr[0:SUBLANES - 3, :] = jnp.zeros((SUBLANES - 3, ext_scr.shape[1]), F32)
        ext_scr[SUBLANES - 3:SUBLANES, :] = buf_ref[0]

    @pl.when(n > 0)
    def _():
        ext_scr[0:SUBLANES, :] = ext_scr[tb:tb + SUBLANES, :]

    ext_scr[SUBLANES:SUBLANES + tb, :] = qkv_ref[0]
    full = ext_scr[...]
    wconv = wconv_ref[...]
    acc = full * wconv[CONV_W - 1:CONV_W, :]
    for j in range(CONV_W - 1):
        acc = acc + pltpu.roll(full, CONV_W - 1 - j, axis=0) * wconv[j:j + 1, :]
    qs, ks, v, glog, beta = _gdn_inputs(acc[SUBLANES:SUBLANES + tb], ab_ref[0], alog_ref[...], dtb_ref[...], None)
    u, wq, aqk_bd, qk_diag, kg, egc = _gdn_prepare(qs, ks, v, glog, beta, tri_ref[...], c, nch)
    gout = gout_ref[...]
    s = s_scr[...]
    for ch in range(nch):
        sel = slice(ch * HEADS, (ch + 1) * HEADS)
        ws_qs = _bmm(wq[sel], _bf(s))
        v_new = u[sel] - ws_qs[:, :c]
        intra = _dot(aqk_bd[ch], _bf(v_new.reshape(HEADS * c, HEAD_DIM)))
        o = ws_qs[:, c:] + intra.reshape(HEADS, c, HEAD_DIM) + qk_diag[sel] * v_new
        s = egc[sel] * s + _bmm_tn(kg[sel], _bf(v_new))
        for h in range(HEADS):
            sl = _head(h)
            o_ref[0, ch * c:(ch + 1) * c, sl] = _bf(_rms(o[h], gout)
                                                    * jax.nn.silu(z_ref[0, ch * c:(ch + 1) * c, sl]))
    s_scr[...] = s

    @pl.when(n == pl.num_programs(1) - 1)
    def _():
        sn_ref[0] = s


def _pad_lanes(x):
    return jnp.zeros((1, LANES), F32).at[0, :x.shape[0]].set(x.astype(F32))


def _gdn_long(gqkv, gz, ab, conv_buf, w_conv, a_log, dt_bias, g_out, s0):
    bsz, t, wq = gqkv.shape
    c = GDN_CHUNK
    nch = GDN_CHUNKS_PER_STEP
    tb = c * nch
    assert t % tb == 0
    w = HEADS * HEAD_DIM
    tri = np.tril(np.ones((c, c), np.float32))
    st = pl.BlockSpec((1, HEADS, HEAD_DIM, HEAD_DIM), lambda b, n: (b, 0, 0, 0))
    return pl.pallas_call(
        functools.partial(_gdn_kernel, c=c, nch=nch),
        grid=(bsz, t // tb),
        in_specs=[pl.BlockSpec((1, tb, wq), lambda b, n: (b, n, 0)),
                  pl.BlockSpec((1, tb, w), lambda b, n: (b, n, 0)),
                  pl.BlockSpec((1, tb, LANES), lambda b, n: (b, n, 0)),
                  pl.BlockSpec((1, CONV_W - 1, wq), lambda b, n: (b, 0, 0)),
                  _resident((CONV_W, wq)), _resident((1, LANES)), _resident((1, LANES)),
                  _resident((1, HEAD_DIM)), _resident((c, c)), st],
        out_specs=[pl.BlockSpec((1, tb, w), lambda b, n: (b, n, 0)), st],
        out_shape=[jax.ShapeDtypeStruct((bsz, t, w), BF16),
                   jax.ShapeDtypeStruct((bsz, HEADS, HEAD_DIM, HEAD_DIM), F32)],
        scratch_shapes=[pltpu.VMEM((HEADS, HEAD_DIM, HEAD_DIM), F32),
                        pltpu.VMEM((tb + SUBLANES, wq), F32)],
        compiler_params=_cparams("arbitrary", "arbitrary"),
        name="gdn_long",
    )(gqkv, gz, ab, conv_buf, w_conv,
      _pad_lanes(a_log), _pad_lanes(dt_bias),
      g_out.reshape(1, HEAD_DIM), jnp.asarray(tri, BF16), s0)


def _shift_rows(x, d):
    return x if d == 0 else pltpu.roll(x, d, axis=1)


def _cumsum_rows8(x, rows):
    for d in (1, 2, 4):
        x = x + jnp.where(rows >= d, pltpu.roll(x, d, axis=1), 0.0)
    return x


def _hgrn_short(q, hf, v, lb, s, rows, t_valid):
    f = lb + (1.0 - lb) * jax.nn.sigmoid(hf)
    valid = rows < t_valid
    logf = jnp.where(valid, jnp.log(f), 0.0)
    k = jnp.where(valid, 1.0 - f, 0.0)
    g = _cumsum_rows8(logf, rows)
    gc = g[:, SUBLANES - 1:SUBLANES, :]
    o = _bmm(_bf(q * jnp.exp(g)), _bf(s))
    for d in range(t_valid):
        ok = rows >= d
        dec = jnp.exp(jnp.where(ok, g - _shift_rows(g, d), 0.0))
        a = jnp.sum(jnp.where(ok, q * _shift_rows(k, d) * dec, 0.0), axis=-1, keepdims=True)
        o = o + a * _shift_rows(v, d)
    s_new = _col_to_matrix_mxu(jnp.exp(gc)) * s + _bmm_tn(_bf(k * jnp.exp(gc - g)), _bf(v))
    return o, s_new


def _gdn_short(q, k, v, g_b, beta_b, s, rows, t_valid):
    g = _cumsum_rows8(g_b, rows)
    gc = g[:, SUBLANES - 1:SUBLANES, :]
    eg = jnp.exp(g)
    kb = k * beta_b
    dec = [None] + [jnp.exp(jnp.where(rows >= d, g - _shift_rows(g, d), 0.0)) for d in range(1, t_valid)]
    mcol = [None] + [jnp.where(rows >= d, jnp.sum(kb * _shift_rows(k, d), axis=-1, keepdims=True) * dec[d], 0.0)
                     for d in range(1, t_valid)]
    ru, rw = v * beta_b, kb * eg
    xu, xw = ru, rw
    for _ in range(t_valid - 1):
        nu, nw = ru, rw
        for d in range(1, t_valid):
            nu = nu - mcol[d] * _shift_rows(xu, d)
            nw = nw - mcol[d] * _shift_rows(xw, d)
        xu, xw = nu, nw
    ws_qs = _bmm(_bf(jnp.concatenate([xw, q * eg], axis=1)), _bf(s))
    v_new = xu - ws_qs[:, :SUBLANES]
    o = ws_qs[:, SUBLANES:] + jnp.sum(q * k, axis=-1, keepdims=True) * v_new
    for d in range(1, t_valid):
        a = jnp.where(rows >= d, jnp.sum(q * _shift_rows(k, d), axis=-1, keepdims=True) * dec[d], 0.0)
        o = o + a * _shift_rows(v_new, d)
    s_new = jnp.exp(gc) * s + _bmm_tn(_bf(k * jnp.exp(gc - g)), _bf(v_new))
    return o, s_new


def _short_kernel(hq_ref, hf_ref, hi_ref, hgate_ref, qkv_ref, z_ref, ab_ref, buf_ref,
                  lbl_ref, hgout_ref, wconv_ref, alog_ref, dtb_ref, ggout_ref, shg0_ref, sgd0_ref,
                  ohg_ref, ogd_ref, shg_ref, sgd_ref, ext_scr, *, t_valid, layer):
    bb = hq_ref.shape[0]
    rows = lax.broadcasted_iota(jnp.int32, (bb, SUBLANES, LANES), 1)
    lb = _forget_lower_bound(lbl_ref[...], layer)
    hgout = hgout_ref[...]
    ggout = ggout_ref[...]
    wconv = wconv_ref[...]

    def tile8(x):
        if t_valid == SUBLANES:
            return x
        return jnp.concatenate([x, jnp.zeros((bb, SUBLANES - t_valid, x.shape[2]), x.dtype)], axis=1)

    ext_scr[:, SUBLANES - 3:SUBLANES, :] = buf_ref[...]
    ext_scr[:, SUBLANES:2 * SUBLANES, :] = tile8(qkv_ref[...])
    acc = ext_scr[:, pl.ds(SUBLANES - 3, SUBLANES), :] * wconv[0:1, :]
    for j in range(1, CONV_W):
        acc = acc + ext_scr[:, pl.ds(SUBLANES - 3 + j, SUBLANES), :] * wconv[j:j + 1, :]
    qs, ks, v, glog, beta = _gdn_inputs(acc, tile8(ab_ref[...]), alog_ref[...], dtb_ref[...], t_valid)
    for h in range(HEADS):
        sl = _head(h)
        o, s_new = _hgrn_short(tile8(hq_ref[:, :, sl]), tile8(hf_ref[:, :, sl]), tile8(hi_ref[:, :, sl]),
                               lb[:, sl], shg0_ref[:, h], rows, t_valid)
        shg_ref[:, h] = s_new
        ohg_ref[:, :, sl] = (_rms(o, hgout[:, sl])[:, :t_valid]) * jax.nn.silu(hgate_ref[:, :, sl])
        g_b = jnp.broadcast_to(glog[:, :, h:h + 1], rows.shape)
        beta_b = jnp.broadcast_to(beta[:, :, HEADS + h:HEADS + h + 1], rows.shape)
        o, s_new = _gdn_short(qs[h], ks[h], v[:, :, sl], g_b, beta_b, sgd0_ref[:, h], rows, t_valid)
        sgd_ref[:, h] = s_new
        ogd_ref[:, :, sl] = (_rms(o, ggout)[:, :t_valid]) * jax.nn.silu(z_ref[:, :, sl])


def _mixers_short(hgp, gqkv, gz, ab, conv_buf, lb_logits, hg_out, w_conv, a_log, dt_bias, gdn_out,
                  s_hg0, s_gdn0, t_valid, layer):
    bsz, t, wq = gqkv.shape
    assert t == t_valid and CONV_W - 1 <= t_valid <= SUBLANES
    bb = DECODE_BLOCK
    assert bsz % bb == 0
    w = HEADS * HEAD_DIM
    col = lambda j: pl.BlockSpec((bb, t, w), lambda b, j=j: (b, 0, j))
    tok = lambda width: pl.BlockSpec((bb, t, width), lambda b: (b, 0, 0))
    st = pl.BlockSpec((bb, HEADS, HEAD_DIM, HEAD_DIM), lambda b: (b, 0, 0, 0))
    return pl.pallas_call(
        functools.partial(_short_kernel, t_valid=t_valid, layer=layer),
        grid=(bsz // bb,),
        in_specs=[col(0), col(1), col(2), col(3), tok(wq), tok(w), tok(LANES),
                  pl.BlockSpec((bb, CONV_W - 1, wq), lambda b: (b, 0, 0)),
                  _resident(lb_logits.shape), _resident((1, w)), _resident((CONV_W, wq)),
                  _resident((1, LANES)), _resident((1, LANES)), _resident((1, HEAD_DIM)), st, st],
        out_specs=[tok(w), tok(w), st, st],
        out_shape=[jax.ShapeDtypeStruct((bsz, t, w), F32), jax.ShapeDtypeStruct((bsz, t, w), F32),
                   jax.ShapeDtypeStruct(s_hg0.shape, F32), jax.ShapeDtypeStruct(s_gdn0.shape, F32)],
        scratch_shapes=[pltpu.VMEM((bb, 2 * SUBLANES, wq), F32)],
        compiler_params=_cparams("arbitrary"),
        name="mixers_short",
    )(hgp, hgp, hgp, hgp, gqkv, gz, ab, conv_buf, lb_logits, hg_out.reshape(1, w), w_conv,
      _pad_lanes(a_log), _pad_lanes(dt_bias), gdn_out.reshape(1, HEAD_DIM), s_hg0, s_gdn0)


def _mem_attn_kernel(q_ref, k_ref, v_ref, o_ref):
    for h in range(HEADS):
        sl = _head(h)
        s = _bmm_nt(_bf(q_ref[:, :, sl]), _bf(k_ref[:, :, sl])) * (HEAD_DIM ** -0.5)
        e = jnp.exp(s - jnp.max(s, axis=-1, keepdims=True))
        o_ref[:, :, sl] = _bf(_bmm(_bf(e), _bf(v_ref[:, :, sl])) / jnp.sum(e, axis=-1, keepdims=True))


def _mem_attn(mq, mem_k, mem_v, bb, tq):
    bsz, t, w = mq.shape
    m = mem_k.shape[1]
    assert bsz % bb == 0 and t % tq == 0
    kv = pl.BlockSpec((bb, m, w), lambda b, n: (b, 0, 0))
    return pl.pallas_call(
        _mem_attn_kernel,
        grid=(bsz // bb, t // tq),
        in_specs=[pl.BlockSpec((bb, tq, w), lambda b, n: (b, n, 0)), kv, kv],
        out_specs=pl.BlockSpec((bb, tq, w), lambda b, n: (b, n, 0)),
        out_shape=jax.ShapeDtypeStruct((bsz, t, w), BF16),
        compiler_params=_cparams("arbitrary", "arbitrary"),
        name="mem_attn",
    )(mq, mem_k, mem_v)


def _mem_attn_cache_kernel(q_ref, k_hbm, v_hbm, o_ref, kbuf, vbuf, sem):
    i = pl.program_id(0)
    bb = q_ref.shape[0]

    def copies(step, slot):
        out = []
        for h in range(HEADS):
            for j, (src, dst) in enumerate(((k_hbm, kbuf), (v_hbm, vbuf))):
                out.append(pltpu.make_async_copy(src.at[pl.ds(step * bb, bb), :, h, :],
                                                 dst.at[slot, h], sem.at[j, slot, h]))
        return out

    slot = i % 2

    @pl.when(i == 0)
    def _():
        for cp in copies(0, 0):
            cp.start()

    @pl.when(i + 1 < pl.num_programs(0))
    def _():
        for cp in copies(i + 1, 1 - slot):
            cp.start()

    for cp in copies(i, slot):
        cp.wait()
    t = q_ref.shape[1]
    for h in range(HEADS):
        sl = _head(h)
        q = q_ref[:, :, sl]
        if t < SUBLANES:
            q = jnp.concatenate([q, jnp.zeros((bb, SUBLANES - t, HEAD_DIM), q.dtype)], axis=1)
        s = _bmm_nt(_bf(q), _bf(kbuf[slot, h])) * (HEAD_DIM ** -0.5)
        e = jnp.exp(s - jnp.max(s, axis=-1, keepdims=True))
        o = _bmm(_bf(e), _bf(vbuf[slot, h])) / jnp.sum(e, axis=-1, keepdims=True)
        o_ref[:, :, sl] = o[:, :t]


def _mem_attn_cache(mq, cache_k, cache_v, bb):
    bsz, t, w = mq.shape
    m = cache_k.shape[1]
    assert bsz % bb == 0 and cache_k.shape == (bsz, m, HEADS, HEAD_DIM)
    return pl.pallas_call(
        _mem_attn_cache_kernel,
        grid=(bsz // bb,),
        in_specs=[pl.BlockSpec((bb, t, w), lambda b: (b, 0, 0)),
                  pl.BlockSpec(memory_space=pl.ANY), pl.BlockSpec(memory_space=pl.ANY)],
        out_specs=pl.BlockSpec((bb, t, w), lambda b: (b, 0, 0)),
        out_shape=jax.ShapeDtypeStruct((bsz, t, w), F32),
        scratch_shapes=[pltpu.VMEM((2, HEADS, bb, m, HEAD_DIM), F32),
                        pltpu.VMEM((2, HEADS, bb, m, HEAD_DIM), F32),
                        pltpu.SemaphoreType.DMA((2, 2, HEADS))],
        compiler_params=_cparams("arbitrary"),
        name="mem_attn_cache",
    )(mq, cache_k, cache_v)


def _merge_ffn_kernel(x_ref, ohg_ref, ogd_ref, omem_ref, gpre_ref, wgates_ref,
                      wbh_ref, wbg_ref, wbm_ref, wout_ref, gpm_ref, gpf_ref,
                      wffn_ref, wdown_ref, gpo_ref, y_ref):
    d = x_ref.shape[-1]
    x = x_ref[...]
    xn = _bf(_rms(x, gpre_ref[...]))
    merged = None
    for j, (o_ref, wb_ref) in enumerate(((ohg_ref, wbh_ref), (ogd_ref, wbg_ref), (omem_ref, wbm_ref))):
        gate = jax.nn.sigmoid(_dot(xn, wgates_ref[:, j * d:(j + 1) * d]))
        term = gate * _dot(_bf(o_ref[...]), wb_ref[...])
        merged = term if merged is None else merged + term
    h = x + _rms(_dot(_bf(merged), wout_ref[...]), gpm_ref[...])
    hn = _bf(_rms(h, gpf_ref[...]))
    hidden = wdown_ref.shape[0]
    ff = None
    for c0 in range(0, hidden, FFN_CHUNK):
        a = (jax.nn.silu(_dot(hn, wffn_ref[:, c0:c0 + FFN_CHUNK]))
             * _dot(hn, wffn_ref[:, hidden + c0:hidden + c0 + FFN_CHUNK]))
        p = _dot(_bf(a), wdown_ref[c0:c0 + FFN_CHUNK, :])
        ff = p if ff is None else ff + p
    y_ref[...] = h + _rms(ff, gpo_ref[...])


def _merge_ffn(x2d, ohg, ogd, omem, gpre, wgates, wbh, wbg, wbm, wout, gpm, gpf, wffn, wdown, gpo):
    n, d = x2d.shape
    tm = min(TOKEN_TILE, n)
    hidden = wdown.shape[0]
    assert n % tm == 0 and hidden % FFN_CHUNK == 0 and wffn.shape[1] == 2 * hidden
    assert wgates.shape == (d, N_BRANCH * d)
    tok = lambda width: pl.BlockSpec((tm, width), lambda i: (i, 0))
    row = lambda a: a.reshape(1, d)
    return pl.pallas_call(
        _merge_ffn_kernel,
        grid=(n // tm,),
        in_specs=[tok(d), tok(ohg.shape[1]), tok(ogd.shape[1]), tok(omem.shape[1]),
                  _resident((1, d)), _resident(wgates.shape),
                  _resident(wbh.shape), _resident(wbg.shape), _resident(wbm.shape), _resident(wout.shape),
                  _resident((1, d)), _resident((1, d)),
                  _resident(wffn.shape), _resident(wdown.shape), _resident((1, d))],
        out_specs=tok(d),
        out_shape=jax.ShapeDtypeStruct((n, d), F32),
        compiler_params=_cparams("arbitrary"),
        name="merge_ffn",
    )(x2d, ohg, ogd, omem, row(gpre), wgates, wbh, wbg, wbm, wout, row(gpm), row(gpf), wffn, wdown, row(gpo))


def _pack_w_in_kernel(wt_ref, mix_ref, gates_ref):
    w = HEADS * HEAD_DIM
    c_ab = 8 * w
    c_mq = c_ab + 2 * HEADS
    c_gt = c_mq + w

    def put(dst_ref, dst0, src0, n):
        for j in range(0, n, w):
            nb = min(w, n - j)
            dst_ref[:, dst0 + j:dst0 + j + nb] = _bf(wt_ref[pl.ds(src0 + j, nb), :].T)

    put(mix_ref, 0, 0, c_ab)
    put(mix_ref, c_ab, c_mq, w)
    ab = jnp.concatenate([wt_ref[pl.ds(c_ab, 2 * HEADS), :],
                          jnp.zeros((LANES - 2 * HEADS, wt_ref.shape[1]), F32)], axis=0)
    mix_ref[:, c_ab + w:c_ab + w + LANES] = _bf(ab.T)
    put(gates_ref, 0, c_gt, gates_ref.shape[1])


def _pack_w_in(w_in_t):
    total, d = w_in_t.shape
    w = HEADS * HEAD_DIM
    widths = (4 * w, 3 * w, w, w, LANES)
    n_gates = total - (9 * w + 2 * HEADS)
    assert n_gates % LANES == 0 and d % LANES == 0
    mixers, gates = pl.pallas_call(
        _pack_w_in_kernel,
        out_shape=[jax.ShapeDtypeStruct((d, sum(widths)), BF16), jax.ShapeDtypeStruct((d, n_gates), BF16)],
        compiler_params=pltpu.CompilerParams(vmem_limit_bytes=VMEM_LIMIT),
        name="pack_w_in",
    )(w_in_t)
    return mixers, widths, gates


def _of_layer(a, l):
    return a.reshape(a.shape[1:]) if a.shape[0] == 1 else a[l]


def _layer(x, mem_k, mem_v, conv_buf, s_hg, s_gdn, layer, lb_logits, p):
    bsz, t, d = x.shape
    t_valid = t
    short = t <= SUBLANES
    x2d = x.reshape(bsz * t, d)
    w = HEADS * HEAD_DIM

    hgp, gqkv, gz, mq, ab = _norm_proj(x2d, p["g_pre_mix"], p["w_in"], p["w_in_widths"], "in_proj")
    as3d = lambda a: a.reshape(bsz, t, a.shape[-1])
    gqkv3 = as3d(gqkv)
    if short:
        o_hg, o_gdn, s_hg_new, s_gdn_new = _mixers_short(
            as3d(hgp), gqkv3, as3d(gz), as3d(ab), conv_buf, lb_logits, p["g_hg_out"], p["w_conv"],
            p["a_log"], p["dt_bias"], p["g_gdn_out"], s_hg, s_gdn, t_valid, layer)
        o_mem = _mem_attn_cache(as3d(mq), mem_k, mem_v, ATTN_DECODE_BLOCK)
    else:
        o_hg, s_hg_new = _hgrn_long(as3d(hgp), lb_logits, p["g_hg_out"], s_hg, layer)
        o_gdn, s_gdn_new = _gdn_long(gqkv3, as3d(gz), as3d(ab), conv_buf, p["w_conv"], p["a_log"],
                                     p["dt_bias"], p["g_gdn_out"], s_gdn)
        o_mem = _mem_attn(as3d(mq), mem_k, mem_v, 1, min(ATTN_TILE, t))
    assert t_valid >= CONV_W - 1
    conv_new = gqkv3[:, t_valid - (CONV_W - 1):t_valid, :]

    flat = lambda a: a.reshape(bsz * t, w)
    y = _merge_ffn(x2d, flat(o_hg), flat(o_gdn), flat(o_mem), p["g_pre_mix"], p["w_gates"],
                   p["w_br_hg"], p["w_br_gdn"], p["w_br_mem"], p["w_out"], p["g_post_mix"],
                   p["g_pre_ffn"], p["w_ffn_in"], p["w_ffn_out"], p["g_post_ffn"])
    return y.reshape(bsz, t, d), conv_new, s_hg_new, s_gdn_new


def kernel(x_prompt, x_sample, mem_prompt, cache_mem_k, cache_mem_v, state_hgrn, state_gdn, state_gdn_conv, hg_lb_logits, g_pre_mix, w_in, w_conv, a_log, dt_bias, g_hg_out, g_gdn_out, g_mem, w_mem_kv, w_br_hg, w_br_gdn, w_br_mem, w_out, g_post_mix, g_pre_ffn, w_ffn_in, w_ffn_out, g_post_ffn):
    depth = w_in.shape[0]
    bp, _, d = x_prompt.shape
    m = mem_prompt.shape[1]
    w = HEADS * HEAD_DIM
    yp, ys = x_prompt, x_sample
    outs = [[] for _ in range(8)]
    for l in range(depth):
        w_in_packed, widths, w_gates = _pack_w_in(_of_layer(w_in, l).T)
        p = dict(g_pre_mix=g_pre_mix[l], w_in=w_in_packed, w_in_widths=widths, w_gates=w_gates,
                 w_conv=w_conv[l],
                 a_log=a_log[l], dt_bias=dt_bias[l], g_hg_out=g_hg_out[l], g_gdn_out=g_gdn_out[l],
                 w_br_hg=_bf(w_br_hg[l]), w_br_gdn=_bf(w_br_gdn[l]), w_br_mem=_bf(w_br_mem[l]),
                 w_out=_bf(w_out[l]), g_post_mix=g_post_mix[l], g_pre_ffn=g_pre_ffn[l],
                 w_ffn_in=_bf(w_ffn_in[l]), w_ffn_out=_bf(w_ffn_out[l]), g_post_ffn=g_post_ffn[l])
        mk, mv = _norm_proj(mem_prompt.reshape(bp * m, d), g_mem[l], _bf(w_mem_kv[l]), (w, w), "mem_kv")
        mk, mv = mk.reshape(bp, m, w), mv.reshape(bp, m, w)
        zeros_state = jnp.zeros((bp, HEADS, HEAD_DIM, HEAD_DIM), F32)
        yp, cb, sh, sg = _layer(yp, mk, mv, jnp.zeros((bp, CONV_W - 1, 3 * w), F32),
                                zeros_state, zeros_state, l, hg_lb_logits, p)
        bs = x_sample.shape[0]
        ys, cb2, sh2, sg2 = _layer(ys, _of_layer(cache_mem_k, l), _of_layer(cache_mem_v, l),
                                   _of_layer(state_gdn_conv, l), _of_layer(state_hgrn, l),
                                   _of_layer(state_gdn, l), l, hg_lb_logits, p)
        for lst, val in zip(outs, (mk.reshape(bp, m, HEADS, HEAD_DIM), mv.reshape(bp, m, HEADS, HEAD_DIM),
                                   sh, sg, cb, sh2, sg2, cb2)):
            lst.append(val)
    return (yp, ys) + tuple(o[0].reshape((1,) + o[0].shape) if depth == 1 else jnp.stack(o) for o in outs)
```

```python
import functools
import math

import numpy as np
import jax
import jax.numpy as jnp
from jax import lax
from jax.experimental import pallas as pl
from jax.experimental.pallas import tpu as pltpu

F32 = jnp.float32
BF16 = jnp.bfloat16

EPS = 1e-6
HEADS = 4
HEAD_DIM = 128
CONV_W = 4
N_BRANCH = 3

LANES = 128
SUBLANES = 8
VMEM_LIMIT = 56 * 1024 * 1024

SAFE_LOG_RANGE = 70.0
HG_CHUNK = 128
HG_BOUNDED_CHUNK = 64
HG_CHUNKS_PER_STEP = 4
GDN_CHUNK = 64
GDN_CHUNKS_PER_STEP = 16
TOKEN_TILE = 512
ATTN_TILE = 1024
FFN_CHUNK = 256
DECODE_BLOCK = 16
ATTN_DECODE_BLOCK = 8


def _cparams(*sem):
    return pltpu.CompilerParams(dimension_semantics=sem, vmem_limit_bytes=VMEM_LIMIT)


def _resident(shape):
    nd = len(shape)
    return pl.BlockSpec(shape, lambda *_: (0,) * nd, pipeline_mode=pl.Buffered(1))


def _bf(x):
    return x.astype(BF16)


def _dot(a, b):
    return jnp.dot(a, b, preferred_element_type=F32)


def _dot_nt(a, b):
    return lax.dot_general(a, b, (((1,), (1,)), ((), ())), preferred_element_type=F32)


def _dot_tn(a, b):
    return lax.dot_general(a, b, (((0,), (0,)), ((), ())), preferred_element_type=F32)


def _bmm(a, b):
    return jnp.einsum("nij,njk->nik", a, b, preferred_element_type=F32)


def _bmm_nt(a, b):
    return jnp.einsum("nid,njd->nij", a, b, preferred_element_type=F32)


def _bmm_tn(a, b):
    return jnp.einsum("nci,ncj->nij", a, b, preferred_element_type=F32)


def _split2(x):
    hi = _bf(x)
    lo = _bf(x - hi.astype(F32))
    return hi, lo


def _dot_exact_lhs(w_bf16, x):
    hi, lo = _split2(x)
    return _dot(w_bf16, hi) + _dot(w_bf16, lo)


def _rms(x, g):
    return x * lax.rsqrt(jnp.mean(x * x, axis=-1, keepdims=True) + EPS) * g


def _head(h):
    return slice(h * HEAD_DIM, (h + 1) * HEAD_DIM)


def _block_diag(x, nblk):
    wblk = x.shape[-1] // nblk
    if wblk % LANES == 0:
        zeros = lambda n: [jnp.zeros(x.shape[:-1] + (n * wblk,), x.dtype)] if n else []
        parts = [jnp.concatenate(zeros(j) + [x[..., j * wblk:(j + 1) * wblk]] + zeros(nblk - 1 - j), axis=-1)
                 for j in range(nblk)]
    else:
        lane = lax.broadcasted_iota(jnp.int32, x.shape, x.ndim - 1)
        parts = [jnp.where((lane >= j * wblk) & (lane < (j + 1) * wblk), x, jnp.zeros_like(x))
                 for j in range(nblk)]
    return jnp.concatenate(parts, axis=-2)


def _norm_proj_kernel(x_ref, g_ref, w_ref, *out_refs):
    xn = _bf(_rms(x_ref[...], g_ref[...]))
    off = 0
    for o_ref in out_refs:
        width = o_ref.shape[-1]
        for c0 in range(0, width, 512):
            cw = min(512, width - c0)
            o_ref[:, c0:c0 + cw] = _dot(xn, w_ref[:, off + c0:off + c0 + cw])
        off += width


def _norm_proj(x2d, g, w_bf16, widths, name):
    n, d = x2d.shape
    tm = min(TOKEN_TILE, n)
    assert n % tm == 0 and sum(widths) == w_bf16.shape[1]
    return pl.pallas_call(
        _norm_proj_kernel,
        grid=(n // tm,),
        in_specs=[pl.BlockSpec((tm, d), lambda i: (i, 0)),
                  _resident((1, d)),
                  _resident(w_bf16.shape)],
        out_specs=[pl.BlockSpec((tm, w), lambda i: (i, 0)) for w in widths],
        out_shape=[jax.ShapeDtypeStruct((n, w), F32) for w in widths],
        compiler_params=_cparams("arbitrary"),
        name=name,
    )(x2d, g.reshape(1, d), w_bf16)


def _mem_kv_kernel(x_ref, g_ref, w_ref, k_ref, v_ref, k4_hbm, v4_hbm, sem):
    i = pl.program_id(0)
    tm = x_ref.shape[0]
    w = HEADS * HEAD_DIM
    xn = _bf(_rms(x_ref[...], g_ref[...]))
    k_ref[...] = _dot(xn, w_ref[:, 0:w])
    v_ref[...] = _dot(xn, w_ref[:, w:2 * w])
    copies = [pltpu.make_async_copy(src.at[:, _head(h)], dst.at[pl.ds(i * tm, tm), h, :], sem.at[j, h])
              for j, (src, dst) in enumerate(((k_ref, k4_hbm), (v_ref, v4_hbm))) for h in range(HEADS)]
    for cp in copies:
        cp.start()
    for cp in copies:
        cp.wait()


def _mem_kv(x2d, g, w_bf16):
    n, d = x2d.shape
    w = HEADS * HEAD_DIM
    tm = min(TOKEN_TILE, n)
    assert n % tm == 0 and w_bf16.shape == (d, 2 * w)
    tok = pl.BlockSpec((tm, w), lambda i: (i, 0))
    anyspace = pl.BlockSpec(memory_space=pl.ANY)
    return pl.pallas_call(
        _mem_kv_kernel,
        grid=(n // tm,),
        in_specs=[pl.BlockSpec((tm, d), lambda i: (i, 0)), _resident((1, d)), _resident(w_bf16.shape)],
        out_specs=[tok, tok, anyspace, anyspace],
        out_shape=[jax.ShapeDtypeStruct((n, w), F32), jax.ShapeDtypeStruct((n, w), F32),
                   jax.ShapeDtypeStruct((n, HEADS, HEAD_DIM), F32), jax.ShapeDtypeStruct((n, HEADS, HEAD_DIM), F32)],
        scratch_shapes=[pltpu.SemaphoreType.DMA((2, HEADS))],
        compiler_params=_cparams("arbitrary"),
        name="mem_kv",
    )(x2d, g.reshape(1, d), w_bf16)


def _forget_lower_bound(logits, layer):
    m = jnp.max(logits, axis=0, keepdims=True)
    e = jnp.exp(logits - m)
    return jnp.sum(e[:layer + 1], axis=0, keepdims=True) / jnp.sum(e, axis=0, keepdims=True)


def _col_to_matrix_mxu(row):
    b, _, n = row.shape
    tile = jnp.concatenate([row, jnp.zeros((b, SUBLANES - 1, n), row.dtype)], axis=1)
    hi = _bf(tile)
    mid = _bf(tile - hi.astype(F32))
    lo = _bf(tile - hi.astype(F32) - mid.astype(F32))
    ones = jnp.ones((b, SUBLANES, n), BF16)
    return _bmm_tn(hi, ones) + _bmm_tn(mid, ones) + _bmm_tn(lo, ones)


def _gdn_inputs(acc, ab, alog, dtb, t_valid):
    qkv = jax.nn.silu(acc)
    w = HEADS * HEAD_DIM
    qs, ks = [], []
    for h in range(HEADS):
        qh = qkv[..., h * HEAD_DIM:(h + 1) * HEAD_DIM]
        kh = qkv[..., w + h * HEAD_DIM:w + (h + 1) * HEAD_DIM]
        qs.append(qh * lax.rsqrt(jnp.sum(qh * qh, axis=-1, keepdims=True) + EPS) * (HEAD_DIM ** -0.5))
        ks.append(kh * lax.rsqrt(jnp.sum(kh * kh, axis=-1, keepdims=True) + EPS))
    v = qkv[..., 2 * w:3 * w]
    pre = ab + dtb
    softplus = jnp.maximum(pre, 0.0) + jnp.log(1.0 + jnp.exp(-jnp.abs(pre)))
    glog = -jnp.exp(alog) * softplus
    beta = jax.nn.sigmoid(ab)
    if t_valid is not None:
        valid = lax.broadcasted_iota(jnp.int32, ab.shape, ab.ndim - 2) < t_valid
        glog = jnp.where(valid, glog, 0.0)
        beta = jnp.where(valid, beta, 0.0)
    return qs, ks, v, glog, beta


def _hier_tables(c):
    t = np.arange(c)
    u = np.arange(c)
    tri = (u[None, :] <= t[:, None]).astype(np.float32)
    rows, masks = [], []
    b = c // 2
    while b >= 1:
        ref = (t // (2 * b)) * 2 * b + b - 1
        rows.append(tri - (u[None, :] <= ref[:, None]).astype(np.float32))
        right = (t % (2 * b)) >= b
        left = (t % (2 * b)) < b
        same = (t[:, None] // (2 * b)) == (t[None, :] // (2 * b))
        masks.append((right[:, None] & left[None, :] & same).astype(np.float32))
        b //= 2
    rows.append(tri)
    return np.concatenate(rows, 0), np.stack(masks, 0)


def _hgrn_chunk(q, k, v, logf_hi, logf_lo, s, wall_ref, masks, c):
    nlev = len(masks)
    d = _dot(wall_ref[...], jnp.concatenate([logf_hi, logf_lo], axis=0))
    g = d[nlev * c:(nlev + 1) * c]
    gc = g[c - 1:c, :]
    qg = _bf(q * jnp.exp(g))
    kg = _bf(k * jnp.exp(gc - g))
    egc = jnp.exp(gc)
    vb = _bf(v)
    npair = HEADS // 2
    a = [None] * npair
    for l in range(nlev):
        e = jnp.exp(-jnp.abs(d[l * c:(l + 1) * c]))
        ql, kl = _bf(q * e), _bf(k * e)
        for pr in range(npair):
            pair = slice(2 * pr * HEAD_DIM, (2 * pr + 2) * HEAD_DIM)
            p = _dot_nt(ql[:, pair], _block_diag(kl[:, pair], 2))
            a[pr] = jnp.where(masks[l], p, 0.0) if a[pr] is None else jnp.where(masks[l], p, a[pr])
    qk = q * k
    outs, s_new = [], []
    for h in range(HEADS):
        sl = _head(h)
        a_h = a[h // 2][:, (h % 2) * c:(h % 2 + 1) * c]
        o = _dot_nt(qg[:, sl], _bf(s[h])) + _dot(_bf(a_h), vb[:, sl])
        outs.append(o + jnp.sum(qk[:, sl], axis=-1, keepdims=True) * v[:, sl])
        s_new.append(egc[:, sl] * s[h] + _dot_tn(vb[:, sl], kg[:, sl]))
    return outs, s_new


def _hgrn_chunk_bounded(q, k, v, g, s, c):
    gc = g[c - 1:c, :]
    qg = _bf(q * jnp.exp(g))
    kn = _bf(k * jnp.exp(-g))
    kg = _bf(k * jnp.exp(gc - g))
    egc = jnp.exp(gc)
    vb = _bf(v)
    ti = lax.broadcasted_iota(jnp.int32, (c, HEADS * c), 0)
    si = lax.broadcasted_iota(jnp.int32, (c, HEADS * c), 1) & (c - 1)
    a = jnp.where(ti > si, _dot_nt(qg, _block_diag(kn, HEADS)), 0.0)
    v_rows = jnp.concatenate([vb[:, _head(h)] for h in range(HEADS)], axis=0)
    intra = _dot(_block_diag(_bf(a), HEADS), v_rows)
    qk = q * k
    outs, s_new = [], []
    for h in range(HEADS):
        sl = _head(h)
        outs.append(_dot_nt(qg[:, sl], _bf(s[h])) + intra[h * c:(h + 1) * c]
                    + jnp.sum(qk[:, sl], axis=-1, keepdims=True) * v[:, sl])
        s_new.append(egc[:, sl] * s[h] + _dot_tn(vb[:, sl], kg[:, sl]))
    return outs, s_new


def _hgrn_kernel(q_ref, f_ref, i_ref, gate_ref, lbl_ref, gout_ref, wall_ref, mask_ref, tri_ref, s0_ref,
                 o_ref, sn_ref, s_scr, *, c, nch, layer):
    n = pl.program_id(1)

    @pl.when(n == 0)
    def _():
        for h in range(HEADS):
            s_scr[h] = s0_ref[0, h].T

    lb = _forget_lower_bound(lbl_ref[...], layer)
    gout = gout_ref[...]
    f = lb + (1.0 - lb) * jax.nn.sigmoid(f_ref[0])
    hi, lo = _split2(jnp.log(f))
    k = 1.0 - f
    nlev = mask_ref.shape[0]

    def emit(r, outs):
        for h in range(HEADS):
            sl = _head(h)
            o_ref[0, r, sl] = _bf(_rms(outs[h], gout[:, sl]) * jax.nn.silu(gate_ref[0, r, sl]))

    def run(chunk_len, chunk_fn):
        s = [s_scr[h] for h in range(HEADS)]
        for ch in range(nch * c // chunk_len):
            r = slice(ch * chunk_len, (ch + 1) * chunk_len)
            outs, s = chunk_fn(ch, r, s)
            emit(r, outs)
        for h in range(HEADS):
            s_scr[h] = s[h]

    cb = tri_ref.shape[0]
    half = lambda ch: slice(ch * cb, (ch + 1) * cb)
    gs = [_dot(tri_ref[...], jnp.concatenate([hi[half(ch)], lo[half(ch)]], axis=0))
          for ch in range(nch * c // cb)]
    gmin = functools.reduce(jnp.minimum, [jnp.min(g) for g in gs])
    bounded = gmin >= -SAFE_LOG_RANGE

    @pl.when(bounded)
    def _():
        run(cb, lambda ch, r, s: _hgrn_chunk_bounded(q_ref[0, r, :], k[r], i_ref[0, r, :], gs[ch], s, cb))

    @pl.when(jnp.logical_not(bounded))
    def _():
        masks = [mask_ref[l] != 0.0 for l in range(nlev)]
        run(c, lambda ch, r, s: _hgrn_chunk(q_ref[0, r, :], k[r], i_ref[0, r, :], hi[r], lo[r], s,
                                            wall_ref, masks, c))

    @pl.when(n == pl.num_programs(1) - 1)
    def _():
        for h in range(HEADS):
            sn_ref[0, h] = s_scr[h].T


def _hgrn_long(hgp, lb_logits, g_out, s0, layer):
    bsz, t, _ = hgp.shape
    c = HG_CHUNK
    nch = HG_CHUNKS_PER_STEP
    tb = c * nch
    assert t % tb == 0 and c == HEAD_DIM and HEADS % 2 == 0
    w = HEADS * HEAD_DIM
    wall, masks = _hier_tables(c)
    wall = np.concatenate([wall, wall], axis=1)
    masks = np.concatenate([masks, masks], axis=2)
    cb = HG_BOUNDED_CHUNK
    tri = np.tril(np.ones((cb, cb), np.float32))
    tri = np.concatenate([tri, tri], axis=1)
    col = lambda j: pl.BlockSpec((1, tb, w), lambda b, n, j=j: (b, n, j))
    st = pl.BlockSpec((1, HEADS, HEAD_DIM, HEAD_DIM), lambda b, n: (b, 0, 0, 0))
    return pl.pallas_call(
        functools.partial(_hgrn_kernel, c=c, nch=nch, layer=layer),
        grid=(bsz, t // tb),
        in_specs=[col(0), col(1), col(2), col(3),
                  _resident(lb_logits.shape), _resident((1, w)),
                  _resident(wall.shape), _resident(masks.shape), _resident(tri.shape), st],
        out_specs=[pl.BlockSpec((1, tb, w), lambda b, n: (b, n, 0)), st],
        out_shape=[jax.ShapeDtypeStruct((bsz, t, w), BF16),
                   jax.ShapeDtypeStruct(s0.shape, F32)],
        scratch_shapes=[pltpu.VMEM((HEADS, HEAD_DIM, HEAD_DIM), F32)],
        compiler_params=_cparams("arbitrary", "arbitrary"),
        name="hgrn_long",
    )(hgp, hgp, hgp, hgp, lb_logits, g_out.reshape(1, w),
      jnp.asarray(wall, BF16), jnp.asarray(masks, F32), jnp.asarray(tri, BF16), s0)


def _gdn_prepare(qs, ks, v, glog, beta, tri_bf, c, nch):
    nb = nch * HEADS
    wc = HEADS * c
    rows = lambda x, ch: x[ch * c:(ch + 1) * c]
    per = lambda f: jnp.stack([f(ch, h) for ch in range(nch) for h in range(HEADS)], axis=0)
    q = per(lambda ch, h: rows(qs[h], ch))
    k = per(lambda ch, h: rows(ks[h], ch))
    vv = per(lambda ch, h: rows(v, ch)[:, _head(h)])
    beta_b = per(lambda ch, h: jnp.broadcast_to(rows(beta, ch)[:, HEADS + h:HEADS + h + 1], (c, LANES)))
    lane = lax.broadcasted_iota(jnp.int32, (c, LANES), 1)
    g_small = None
    for ch in range(nch):
        part = jnp.where(lane < HEADS, rows(glog, ch), 0.0)
        part = part if ch == 0 else pltpu.roll(part, ch * HEADS, axis=1)
        g_small = part if g_small is None else g_small + part
    gcum = _dot_exact_lhs(tri_bf, g_small)
    gcum_t = jnp.concatenate([gcum, jnp.zeros((LANES - c, LANES), F32)], axis=0).T
    g = jnp.stack([jnp.broadcast_to(gcum[:, n:n + 1], (c, LANES)) for n in range(nb)], axis=0)
    packed = lambda f: jnp.stack([jnp.concatenate([f(ch * HEADS + h) for h in range(HEADS)], axis=1)
                                  for ch in range(nch)], axis=0)
    g_col = packed(lambda n: jnp.broadcast_to(gcum[:, n:n + 1], (c, c)))
    g_row = packed(lambda n: jnp.broadcast_to(gcum_t[n:n + 1, :c], (c, c)))
    ti = lax.broadcasted_iota(jnp.int32, (nch, c, wc), 1)
    si = lax.broadcasted_iota(jnp.int32, (nch, c, wc), 2) & (c - 1)
    decay = jnp.exp(jnp.minimum(g_col - g_row, 0.0))
    kb = k * beta_b
    heads_on_lanes = lambda x: jnp.stack(
        [jnp.concatenate([x[ch * HEADS + h] for h in range(HEADS)], axis=1) for ch in range(nch)], axis=0)
    kbq = _bf(jnp.concatenate([heads_on_lanes(kb), heads_on_lanes(q)], axis=1))
    kq = _bmm_nt(kbq, _block_diag(_bf(heads_on_lanes(k)), HEADS))
    m = jnp.where(ti > si, kq[:, :c] * decay, 0.0)
    aqk = jnp.where(ti > si, kq[:, c:] * decay, 0.0)
    qk_diag = jnp.sum(q * k, axis=-1, keepdims=True)
    eg = jnp.exp(g)
    rhs = jnp.concatenate([vv * beta_b, kb * eg], axis=-1)
    npow = -m
    poff = npow
    npow_bd = _block_diag(_bf(npow), HEADS)
    for _ in range(int(math.log2(c)) - 1):
        npow = _bmm(_bf(npow), npow_bd)
        npow_bd = _block_diag(_bf(npow), HEADS)
        poff = poff + npow + _bmm(_bf(poff), npow_bd)
    xoff = _bmm(_block_diag(_bf(poff), HEADS), _bf(rhs.reshape(nch, wc, 2 * HEAD_DIM)))
    x = rhs + xoff.reshape(nb, c, 2 * HEAD_DIM)
    u, w = x[:, :, :HEAD_DIM], x[:, :, HEAD_DIM:]
    gc = g[:, c - 1:c, :]
    return (u, _bf(jnp.concatenate([w, q * eg], axis=1)), _block_diag(_bf(aqk), HEADS), qk_diag,
            _bf(k * jnp.exp(gc - g)), jnp.exp(gc))


def _gdn_kernel(qkv_ref, z_ref, ab_ref, buf_ref, wconv_ref, alog_ref, dtb_ref, gout_ref, tri_ref,
                s0_ref, o_ref, sn_ref, s_scr, ext_scr, *, c, nch):
    n = pl.program_id(1)
    tb = c * nch

    @pl.when(n == 0)
    def _():
        s_scr[...] = s0_ref[0]
        ext_scr[0:SUBLANES - 3, :] = jnp.zeros((SUBLANES - 3, ext_scr.shape[1]), F32)
        ext_scr[SUBLANES - 3:SUBLANES, :] = buf_ref[0]

    @pl.when(n > 0)
    def _():
        ext_scr[0:SUBLANES, :] = ext_scr[tb:tb + SUBLANES, :]

    ext_scr[SUBLANES:SUBLANES + tb, :] = qkv_ref[0]
    full = ext_scr[...]
    wconv = wconv_ref[...]
    acc = full * wconv[CONV_W - 1:CONV_W, :]
    for j in range(CONV_W - 1):
        acc = acc + pltpu.roll(full, CONV_W - 1 - j, axis=0) * wconv[j:j + 1, :]
    qs, ks, v, glog, beta = _gdn_inputs(acc[SUBLANES:SUBLANES + tb], ab_ref[0], alog_ref[...], dtb_ref[...], None)
    u, wq, aqk_bd, qk_diag, kg, egc = _gdn_prepare(qs, ks, v, glog, beta, tri_ref[...], c, nch)
    gout = gout_ref[...]
    s = s_scr[...]
    for ch in range(nch):
        sel = slice(ch * HEADS, (ch + 1) * HEADS)
        ws_qs = _bmm(wq[sel], _bf(s))
        v_new = u[sel] - ws_qs[:, :c]
        intra = _dot(aqk_bd[ch], _bf(v_new.reshape(HEADS * c, HEAD_DIM)))
        o = ws_qs[:, c:] + intra.reshape(HEADS, c, HEAD_DIM) + qk_diag[sel] * v_new
        s = egc[sel] * s + _bmm_tn(kg[sel], _bf(v_new))
        for h in range(HEADS):
            sl = _head(h)
            o_ref[0, ch * c:(ch + 1) * c, sl] = _bf(_rms(o[h], gout)
                                                    * jax.nn.silu(z_ref[0, ch * c:(ch + 1) * c, sl]))
    s_scr[...] = s

    @pl.when(n == pl.num_programs(1) - 1)
    def _():
        sn_ref[0] = s


def _pad_lanes(x):
    return jnp.zeros((1, LANES), F32).at[0, :x.shape[0]].set(x.astype(F32))


def _gdn_long(gqkv, gz, ab, conv_buf, w_conv, a_log, dt_bias, g_out, s0):
    bsz, t, wq = gqkv.shape
    c = GDN_CHUNK
    nch = GDN_CHUNKS_PER_STEP
    tb = c * nch
    assert t % tb == 0
    w = HEADS * HEAD_DIM
    tri = np.tril(np.ones((c, c), np.float32))
    st = pl.BlockSpec((1, HEADS, HEAD_DIM, HEAD_DIM), lambda b, n: (b, 0, 0, 0))
    return pl.pallas_call(
        functools.partial(_gdn_kernel, c=c, nch=nch),
        grid=(bsz, t // tb),
        in_specs=[pl.BlockSpec((1, tb, wq), lambda b, n: (b, n, 0)),
                  pl.BlockSpec((1, tb, w), lambda b, n: (b, n, 0)),
                  pl.BlockSpec((1, tb, LANES), lambda b, n: (b, n, 0)),
                  pl.BlockSpec((1, CONV_W - 1, wq), lambda b, n: (b, 0, 0)),
                  _resident((CONV_W, wq)), _resident((1, LANES)), _resident((1, LANES)),
                  _resident((1, HEAD_DIM)), _resident((c, c)), st],
        out_specs=[pl.BlockSpec((1, tb, w), lambda b, n: (b, n, 0)), st],
        out_shape=[jax.ShapeDtypeStruct((bsz, t, w), BF16),
                   jax.ShapeDtypeStruct((bsz, HEADS, HEAD_DIM, HEAD_DIM), F32)],
        scratch_shapes=[pltpu.VMEM((HEADS, HEAD_DIM, HEAD_DIM), F32),
                        pltpu.VMEM((tb + SUBLANES, wq), F32)],
        compiler_params=_cparams("arbitrary", "arbitrary"),
        name="gdn_long",
    )(gqkv, gz, ab, conv_buf, w_conv,
      _pad_lanes(a_log), _pad_lanes(dt_bias),
      g_out.reshape(1, HEAD_DIM), jnp.asarray(tri, BF16), s0)


def _shift_rows(x, d):
    return x if d == 0 else pltpu.roll(x, d, axis=1)


def _cumsum_rows8(x, rows):
    for d in (1, 2, 4):
        x = x + jnp.where(rows >= d, pltpu.roll(x, d, axis=1), 0.0)
    return x


def _hgrn_short(q, hf, v, lb, s, rows, t_valid):
    f = lb + (1.0 - lb) * jax.nn.sigmoid(hf)
    valid = rows < t_valid
    logf = jnp.where(valid, jnp.log(f), 0.0)
    k = jnp.where(valid, 1.0 - f, 0.0)
    g = _cumsum_rows8(logf, rows)
    gc = g[:, SUBLANES - 1:SUBLANES, :]
    o = _bmm(_bf(q * jnp.exp(g)), _bf(s))
    for d in range(t_valid):
        ok = rows >= d
        dec = jnp.exp(jnp.where(ok, g - _shift_rows(g, d), 0.0))
        a = jnp.sum(jnp.where(ok, q * _shift_rows(k, d) * dec, 0.0), axis=-1, keepdims=True)
        o = o + a * _shift_rows(v, d)
    s_new = _col_to_matrix_mxu(jnp.exp(gc)) * s + _bmm_tn(_bf(k * jnp.exp(gc - g)), _bf(v))
    return o, s_new


def _gdn_short(q, k, v, g_b, beta_b, s, rows, t_valid):
    g = _cumsum_rows8(g_b, rows)
    gc = g[:, SUBLANES - 1:SUBLANES, :]
    eg = jnp.exp(g)
    kb = k * beta_b
    dec = [None] + [jnp.exp(jnp.where(rows >= d, g - _shift_rows(g, d), 0.0)) for d in range(1, t_valid)]
    mcol = [None] + [jnp.where(rows >= d, jnp.sum(kb * _shift_rows(k, d), axis=-1, keepdims=True) * dec[d], 0.0)
                     for d in range(1, t_valid)]
    ru, rw = v * beta_b, kb * eg
    xu, xw = ru, rw
    for _ in range(t_valid - 1):
        nu, nw = ru, rw
        for d in range(1, t_valid):
            nu = nu - mcol[d] * _shift_rows(xu, d)
            nw = nw - mcol[d] * _shift_rows(xw, d)
        xu, xw = nu, nw
    ws_qs = _bmm(_bf(jnp.concatenate([xw, q * eg], axis=1)), _bf(s))
    v_new = xu - ws_qs[:, :SUBLANES]
    o = ws_qs[:, SUBLANES:] + jnp.sum(q * k, axis=-1, keepdims=True) * v_new
    for d in range(1, t_valid):
        a = jnp.where(rows >= d, jnp.sum(q * _shift_rows(k, d), axis=-1, keepdims=True) * dec[d], 0.0)
        o = o + a * _shift_rows(v_new, d)
    s_new = jnp.exp(gc) * s + _bmm_tn(_bf(k * jnp.exp(gc - g)), _bf(v_new))
    return o, s_new


def _short_kernel(hq_ref, hf_ref, hi_ref, hgate_ref, qkv_ref, z_ref, ab_ref, buf_ref,
                  lbl_ref, hgout_ref, wconv_ref, alog_ref, dtb_ref, ggout_ref, shg0_ref, sgd0_ref,
                  ohg_ref, ogd_ref, shg_ref, sgd_ref, ext_scr, *, t_valid, layer):
    bb = shg0_ref.shape[0]
    rows = lax.broadcasted_iota(jnp.int32, (bb, SUBLANES, LANES), 1)
    lb = _forget_lower_bound(lbl_ref[...], layer)
    hgout = hgout_ref[...]
    ggout = ggout_ref[...]
    wconv = wconv_ref[...]

    def tiles(x):
        x = x.reshape(bb, t_valid, x.shape[-1])
        if t_valid == SUBLANES:
            return x
        return jnp.concatenate([x, jnp.zeros((bb, SUBLANES - t_valid, x.shape[2]), x.dtype)], axis=1)

    def tokens(x):
        return x[:, :t_valid].reshape(bb * t_valid, x.shape[-1])

    ext_scr[:, SUBLANES - 3:SUBLANES, :] = buf_ref[...]
    ext_scr[:, SUBLANES:2 * SUBLANES, :] = tiles(qkv_ref[...])
    acc = ext_scr[:, pl.ds(SUBLANES - 3, SUBLANES), :] * wconv[0:1, :]
    for j in range(1, CONV_W):
        acc = acc + ext_scr[:, pl.ds(SUBLANES - 3 + j, SUBLANES), :] * wconv[j:j + 1, :]
    qs, ks, v, glog, beta = _gdn_inputs(acc, tiles(ab_ref[...]), alog_ref[...], dtb_ref[...], t_valid)
    for h in range(HEADS):
        sl = _head(h)
        o, s_new = _hgrn_short(tiles(hq_ref[:, sl]), tiles(hf_ref[:, sl]), tiles(hi_ref[:, sl]),
                               lb[:, sl], shg0_ref[:, h], rows, t_valid)
        shg_ref[:, h] = s_new
        ohg_ref[:, sl] = tokens(_rms(o, hgout[:, sl])) * jax.nn.silu(hgate_ref[:, sl])
        g_b = jnp.broadcast_to(glog[:, :, h:h + 1], rows.shape)
        beta_b = jnp.broadcast_to(beta[:, :, HEADS + h:HEADS + h + 1], rows.shape)
        o, s_new = _gdn_short(qs[h], ks[h], v[:, :, sl], g_b, beta_b, sgd0_ref[:, h], rows, t_valid)
        sgd_ref[:, h] = s_new
        ogd_ref[:, sl] = tokens(_rms(o, ggout)) * jax.nn.silu(z_ref[:, sl])


def _mixers_short(hgp, gqkv, gz, ab, conv_buf, lb_logits, hg_out, w_conv, a_log, dt_bias, gdn_out,
                  s_hg0, s_gdn0, t_valid, layer):
    n, wq = gqkv.shape
    t = t_valid
    bsz = n // t
    assert n == bsz * t and CONV_W - 1 <= t_valid <= SUBLANES
    bb = DECODE_BLOCK
    assert bsz % bb == 0 and (bb * t) % SUBLANES == 0
    w = HEADS * HEAD_DIM
    col = lambda j: pl.BlockSpec((bb * t, w), lambda b, j=j: (b, j))
    tok = lambda width: pl.BlockSpec((bb * t, width), lambda b: (b, 0))
    st = pl.BlockSpec((bb, HEADS, HEAD_DIM, HEAD_DIM), lambda b: (b, 0, 0, 0))
    return pl.pallas_call(
        functools.partial(_short_kernel, t_valid=t_valid, layer=layer),
        grid=(bsz // bb,),
        in_specs=[col(0), col(1), col(2), col(3), tok(wq), tok(w), tok(LANES),
                  pl.BlockSpec((bb, CONV_W - 1, wq), lambda b: (b, 0, 0)),
                  _resident(lb_logits.shape), _resident((1, w)), _resident((CONV_W, wq)),
                  _resident((1, LANES)), _resident((1, LANES)), _resident((1, HEAD_DIM)), st, st],
        out_specs=[tok(w), tok(w), st, st],
        out_shape=[jax.ShapeDtypeStruct((n, w), F32), jax.ShapeDtypeStruct((n, w), F32),
                   jax.ShapeDtypeStruct(s_hg0.shape, F32), jax.ShapeDtypeStruct(s_gdn0.shape, F32)],
        scratch_shapes=[pltpu.VMEM((bb, 2 * SUBLANES, wq), F32)],
        compiler_params=_cparams("arbitrary"),
        name="mixers_short",
    )(hgp, hgp, hgp, hgp, gqkv, gz, ab, conv_buf, lb_logits, hg_out.reshape(1, w), w_conv,
      _pad_lanes(a_log), _pad_lanes(dt_bias), gdn_out.reshape(1, HEAD_DIM), s_hg0, s_gdn0)


def _mem_attn_kernel(q_ref, k_ref, v_ref, o_ref):
    for h in range(HEADS):
        sl = _head(h)
        s = _bmm_nt(_bf(q_ref[:, :, sl]), _bf(k_ref[:, :, sl])) * (HEAD_DIM ** -0.5)
        e = jnp.exp(s - jnp.max(s, axis=-1, keepdims=True))
        o_ref[:, :, sl] = _bf(_bmm(_bf(e), _bf(v_ref[:, :, sl])) / jnp.sum(e, axis=-1, keepdims=True))


def _mem_attn(mq, mem_k, mem_v, bb, tq):
    bsz, t, w = mq.shape
    m = mem_k.shape[1]
    assert bsz % bb == 0 and t % tq == 0
    kv = pl.BlockSpec((bb, m, w), lambda b, n: (b, 0, 0))
    return pl.pallas_call(
        _mem_attn_kernel,
        grid=(bsz // bb, t // tq),
        in_specs=[pl.BlockSpec((bb, tq, w), lambda b, n: (b, n, 0)), kv, kv],
        out_specs=pl.BlockSpec((bb, tq, w), lambda b, n: (b, n, 0)),
        out_shape=jax.ShapeDtypeStruct((bsz, t, w), BF16),
        compiler_params=_cparams("arbitrary", "arbitrary"),
        name="mem_attn",
    )(mq, mem_k, mem_v)


def _mem_attn_cache_kernel(q_ref, k_hbm, v_hbm, o_ref, kbuf, vbuf, sem):
    i = pl.program_id(0)
    bb = kbuf.shape[2]
    t = q_ref.shape[0] // bb

    def copies(step, slot):
        out = []
        for h in range(HEADS):
            for j, (src, dst) in enumerate(((k_hbm, kbuf), (v_hbm, vbuf))):
                out.append(pltpu.make_async_copy(src.at[pl.ds(step * bb, bb), :, h, :],
                                                 dst.at[slot, h], sem.at[j, slot, h]))
        return out

    slot = i % 2

    @pl.when(i == 0)
    def _():
        for cp in copies(0, 0):
            cp.start()

    @pl.when(i + 1 < pl.num_programs(0))
    def _():
        for cp in copies(i + 1, 1 - slot):
            cp.start()

    for cp in copies(i, slot):
        cp.wait()
    for h in range(HEADS):
        sl = _head(h)
        q = q_ref[:, sl].reshape(bb, t, HEAD_DIM)
        if t < SUBLANES:
            q = jnp.concatenate([q, jnp.zeros((bb, SUBLANES - t, HEAD_DIM), q.dtype)], axis=1)
        s = _bmm_nt(_bf(q), _bf(kbuf[slot, h])) * (HEAD_DIM ** -0.5)
        e = jnp.exp(s - jnp.max(s, axis=-1, keepdims=True))
        o = _bmm(_bf(e), _bf(vbuf[slot, h])) / jnp.sum(e, axis=-1, keepdims=True)
        o_ref[:, sl] = o[:, :t].reshape(bb * t, HEAD_DIM)


def _mem_attn_cache(mq, cache_k, cache_v, bb):
    n, w = mq.shape
    bsz, m = cache_k.shape[:2]
    t = n // bsz
    assert n == bsz * t and bsz % bb == 0 and cache_k.shape == (bsz, m, HEADS, HEAD_DIM)
    assert (bb * t) % SUBLANES == 0
    return pl.pallas_call(
        _mem_attn_cache_kernel,
        grid=(bsz // bb,),
        in_specs=[pl.BlockSpec((bb * t, w), lambda b: (b, 0)),
                  pl.BlockSpec(memory_space=pl.ANY), pl.BlockSpec(memory_space=pl.ANY)],
        out_specs=pl.BlockSpec((bb * t, w), lambda b: (b, 0)),
        out_shape=jax.ShapeDtypeStruct((n, w), F32),
        scratch_shapes=[pltpu.VMEM((2, HEADS, bb, m, HEAD_DIM), F32),
                        pltpu.VMEM((2, HEADS, bb, m, HEAD_DIM), F32),
                        pltpu.SemaphoreType.DMA((2, 2, HEADS))],
        compiler_params=_cparams("arbitrary"),
        name="mem_attn_cache",
    )(mq, cache_k, cache_v)


def _merge_ffn_kernel(x_ref, ohg_ref, ogd_ref, omem_ref, gpre_ref, wgates_ref,
                      wbh_ref, wbg_ref, wbm_ref, wout_ref, gpm_ref, gpf_ref,
                      wffn_ref, wdown_ref, gpo_ref, y_ref):
    d = x_ref.shape[-1]
    x = x_ref[...]
    xn = _bf(_rms(x, gpre_ref[...]))
    merged = None
    for j, (o_ref, wb_ref) in enumerate(((ohg_ref, wbh_ref), (ogd_ref, wbg_ref), (omem_ref, wbm_ref))):
        gate = jax.nn.sigmoid(_dot(xn, wgates_ref[:, j * d:(j + 1) * d]))
        term = gate * _dot(_bf(o_ref[...]), wb_ref[...])
        merged = term if merged is None else merged + term
    h = x + _rms(_dot(_bf(merged), wout_ref[...]), gpm_ref[...])
    hn = _bf(_rms(h, gpf_ref[...]))
    hidden = wdown_ref.shape[0]
    ff = None
    for c0 in range(0, hidden, FFN_CHUNK):
        a = (jax.nn.silu(_dot(hn, wffn_ref[:, c0:c0 + FFN_CHUNK]))
             * _dot(hn, wffn_ref[:, hidden + c0:hidden + c0 + FFN_CHUNK]))
        p = _dot(_bf(a), wdown_ref[c0:c0 + FFN_CHUNK, :])
        ff = p if ff is None else ff + p
    y_ref[...] = h + _rms(ff, gpo_ref[...])


def _merge_ffn(x2d, ohg, ogd, omem, gpre, wgates, wbh, wbg, wbm, wout, gpm, gpf, wffn, wdown, gpo):
    n, d = x2d.shape
    tm = min(TOKEN_TILE, n)
    hidden = wdown.shape[0]
    assert n % tm == 0 and hidden % FFN_CHUNK == 0 and wffn.shape[1] == 2 * hidden
    assert wgates.shape == (d, N_BRANCH * d)
    tok = lambda width: pl.BlockSpec((tm, width), lambda i: (i, 0))
    row = lambda a: a.reshape(1, d)
    return pl.pallas_call(
        _merge_ffn_kernel,
        grid=(n // tm,),
        in_specs=[tok(d), tok(ohg.shape[1]), tok(ogd.shape[1]), tok(omem.shape[1]),
                  _resident((1, d)), _resident(wgates.shape),
                  _resident(wbh.shape), _resident(wbg.shape), _resident(wbm.shape), _resident(wout.shape),
                  _resident((1, d)), _resident((1, d)),
                  _resident(wffn.shape), _resident(wdown.shape), _resident((1, d))],
        out_specs=tok(d),
        out_shape=jax.ShapeDtypeStruct((n, d), F32),
        compiler_params=_cparams("arbitrary"),
        name="merge_ffn",
    )(x2d, ohg, ogd, omem, row(gpre), wgates, wbh, wbg, wbm, wout, row(gpm), row(gpf), wffn, wdown, row(gpo))


def _pack_w_in_kernel(wt_ref, mix_ref, gates_ref):
    w = HEADS * HEAD_DIM
    c_ab = 8 * w
    c_mq = c_ab + 2 * HEADS
    c_gt = c_mq + w

    def put(dst_ref, dst0, src0, n):
        for j in range(0, n, w):
            nb = min(w, n - j)
            dst_ref[:, dst0 + j:dst0 + j + nb] = _bf(wt_ref[pl.ds(src0 + j, nb), :].T)

    put(mix_ref, 0, 0, c_ab)
    put(mix_ref, c_ab, c_mq, w)
    ab = jnp.concatenate([wt_ref[pl.ds(c_ab, 2 * HEADS), :],
                          jnp.zeros((LANES - 2 * HEADS, wt_ref.shape[1]), F32)], axis=0)
    mix_ref[:, c_ab + w:c_ab + w + LANES] = _bf(ab.T)
    put(gates_ref, 0, c_gt, gates_ref.shape[1])


def _pack_w_in(w_in_t):
    total, d = w_in_t.shape
    w = HEADS * HEAD_DIM
    widths = (4 * w, 3 * w, w, w, LANES)
    n_gates = total - (9 * w + 2 * HEADS)
    assert n_gates % LANES == 0 and d % LANES == 0
    mixers, gates = pl.pallas_call(
        _pack_w_in_kernel,
        out_shape=[jax.ShapeDtypeStruct((d, sum(widths)), BF16), jax.ShapeDtypeStruct((d, n_gates), BF16)],
        compiler_params=pltpu.CompilerParams(vmem_limit_bytes=VMEM_LIMIT),
        name="pack_w_in",
    )(w_in_t)
    return mixers, widths, gates


def _of_layer(a, l):
    return a.reshape(a.shape[1:]) if a.shape[0] == 1 else a[l]


def _layer(x, mem_k, mem_v, conv_buf, s_hg, s_gdn, layer, lb_logits, p):
    bsz, t, d = x.shape
    t_valid = t
    short = t <= SUBLANES
    x2d = x.reshape(bsz * t, d)
    w = HEADS * HEAD_DIM

    hgp, gqkv, gz, mq, ab = _norm_proj(x2d, p["g_pre_mix"], p["w_in"], p["w_in_widths"], "in_proj")
    as3d = lambda a: a.reshape(bsz, t, a.shape[-1])
    if short:
        o_hg, o_gdn, s_hg_new, s_gdn_new = _mixers_short(
            hgp, gqkv, gz, ab, conv_buf, lb_logits, p["g_hg_out"], p["w_conv"],
            p["a_log"], p["dt_bias"], p["g_gdn_out"], s_hg, s_gdn, t_valid, layer)
        o_mem = _mem_attn_cache(mq, mem_k, mem_v, ATTN_DECODE_BLOCK)
    else:
        o_hg, s_hg_new = _hgrn_long(as3d(hgp), lb_logits, p["g_hg_out"], s_hg, layer)
        o_gdn, s_gdn_new = _gdn_long(as3d(gqkv), as3d(gz), as3d(ab), conv_buf, p["w_conv"], p["a_log"],
                                     p["dt_bias"], p["g_gdn_out"], s_gdn)
        o_mem = _mem_attn(as3d(mq), mem_k, mem_v, 1, min(ATTN_TILE, t))
    assert t_valid >= CONV_W - 1
    conv_new = as3d(gqkv)[:, t_valid - (CONV_W - 1):t_valid, :]

    flat = lambda a: a.reshape(bsz * t, w)
    y = _merge_ffn(x2d, flat(o_hg), flat(o_gdn), flat(o_mem), p["g_pre_mix"], p["w_gates"],
                   p["w_br_hg"], p["w_br_gdn"], p["w_br_mem"], p["w_out"], p["g_post_mix"],
                   p["g_pre_ffn"], p["w_ffn_in"], p["w_ffn_out"], p["g_post_ffn"])
    return y.reshape(bsz, t, d), conv_new, s_hg_new, s_gdn_new


def kernel(x_prompt, x_sample, mem_prompt, cache_mem_k, cache_mem_v, state_hgrn, state_gdn, state_gdn_conv, hg_lb_logits, g_pre_mix, w_in, w_conv, a_log, dt_bias, g_hg_out, g_gdn_out, g_mem, w_mem_kv, w_br_hg, w_br_gdn, w_br_mem, w_out, g_post_mix, g_pre_ffn, w_ffn_in, w_ffn_out, g_post_ffn):
    depth = w_in.shape[0]
    bp, _, d = x_prompt.shape
    m = mem_prompt.shape[1]
    w = HEADS * HEAD_DIM
    yp, ys = x_prompt, x_sample
    outs = [[] for _ in range(8)]
    for l in range(depth):
        w_in_packed, widths, w_gates = _pack_w_in(_of_layer(w_in, l).T)
        p = dict(g_pre_mix=g_pre_mix[l], w_in=w_in_packed, w_in_widths=widths, w_gates=w_gates,
                 w_conv=w_conv[l],
                 a_log=a_log[l], dt_bias=dt_bias[l], g_hg_out=g_hg_out[l], g_gdn_out=g_gdn_out[l],
                 w_br_hg=_bf(w_br_hg[l]), w_br_gdn=_bf(w_br_gdn[l]), w_br_mem=_bf(w_br_mem[l]),
                 w_out=_bf(w_out[l]), g_post_mix=g_post_mix[l], g_pre_ffn=g_pre_ffn[l],
                 w_ffn_in=_bf(w_ffn_in[l]), w_ffn_out=_bf(w_ffn_out[l]), g_post_ffn=g_post_ffn[l])
        mk, mv, mk4, mv4 = _mem_kv(mem_prompt.reshape(bp * m, d), g_mem[l], _bf(w_mem_kv[l]))
        mk, mv = mk.reshape(bp, m, w), mv.reshape(bp, m, w)
        zeros_state = jnp.zeros((bp, HEADS, HEAD_DIM, HEAD_DIM), F32)
        yp, cb, sh, sg = _layer(yp, mk, mv, jnp.zeros((bp, CONV_W - 1, 3 * w), F32),
                                zeros_state, zeros_state, l, hg_lb_logits, p)
        bs = x_sample.shape[0]
        ys, cb2, sh2, sg2 = _layer(ys, _of_layer(cache_mem_k, l), _of_layer(cache_mem_v, l),
                                   _of_layer(state_gdn_conv, l), _of_layer(state_hgrn, l),
                                   _of_layer(state_gdn, l), l, hg_lb_logits, p)
        for lst, val in zip(outs, (mk4.reshape(bp, m, HEADS, HEAD_DIM), mv4.reshape(bp, m, HEADS, HEAD_DIM),
                                   sh, sg, cb, sh2, sg2, cb2)):
            lst.append(val)
    return (yp, ys) + tuple(o[0].reshape((1,) + o[0].shape) if depth == 1 else jnp.stack(o) for o in outs)
```

```python
import functools
import math

import numpy as np
import jax
import jax.numpy as jnp
from jax import lax
from jax.experimental import pallas as pl
from jax.experimental.pallas import tpu as pltpu

F32 = jnp.float32
BF16 = jnp.bfloat16

EPS = 1e-6
HEADS = 4
HEAD_DIM = 128
CONV_W = 4
N_BRANCH = 3

LANES = 128
SUBLANES = 8
VMEM_LIMIT = 56 * 1024 * 1024

SAFE_LOG_RANGE = 70.0
HG_CHUNK = 128
HG_BOUNDED_CHUNK = 64
HG_CHUNKS_PER_STEP = 4
GDN_CHUNK = 64
GDN_CHUNKS_PER_STEP = 16
TOKEN_TILE = 512
ATTN_TILE = 1024
FFN_CHUNK = 256
DECODE_BLOCK = 16
ATTN_DECODE_BLOCK = 16


def _cparams(*sem):
    return pltpu.CompilerParams(dimension_semantics=sem, vmem_limit_bytes=VMEM_LIMIT)


def _resident(shape):
    nd = len(shape)
    return pl.BlockSpec(shape, lambda *_: (0,) * nd, pipeline_mode=pl.Buffered(1))


def _bf(x):
    return x.astype(BF16)


def _dot(a, b):
    return jnp.dot(a, b, preferred_element_type=F32)


def _dot_nt(a, b):
    return lax.dot_general(a, b, (((1,), (1,)), ((), ())), preferred_element_type=F32)


def _dot_tn(a, b):
    return lax.dot_general(a, b, (((0,), (0,)), ((), ())), preferred_element_type=F32)


def _bmm(a, b):
    return jnp.einsum("nij,njk->nik", a, b, preferred_element_type=F32)


def _bmm_nt(a, b):
    return jnp.einsum("nid,njd->nij", a, b, preferred_element_type=F32)


def _bmm_tn(a, b):
    return jnp.einsum("nci,ncj->nij", a, b, preferred_element_type=F32)


def _split2(x):
    hi = _bf(x)
    lo = _bf(x - hi.astype(F32))
    return hi, lo


def _dot_exact_lhs(w_bf16, x):
    hi, lo = _split2(x)
    return _dot(w_bf16, hi) + _dot(w_bf16, lo)


def _rms(x, g):
    return x * lax.rsqrt(jnp.mean(x * x, axis=-1, keepdims=True) + EPS) * g


def _head(h):
    return slice(h * HEAD_DIM, (h + 1) * HEAD_DIM)


def _block_diag(x, nblk):
    wblk = x.shape[-1] // nblk
    if wblk % LANES == 0:
        zeros = lambda n: [jnp.zeros(x.shape[:-1] + (n * wblk,), x.dtype)] if n else []
        parts = [jnp.concatenate(zeros(j) + [x[..., j * wblk:(j + 1) * wblk]] + zeros(nblk - 1 - j), axis=-1)
                 for j in range(nblk)]
    else:
        lane = lax.broadcasted_iota(jnp.int32, x.shape, x.ndim - 1)
        parts = [jnp.where((lane >= j * wblk) & (lane < (j + 1) * wblk), x, jnp.zeros_like(x))
                 for j in range(nblk)]
    return jnp.concatenate(parts, axis=-2)


def _norm_proj_kernel(x_ref, g_ref, w_ref, *out_refs):
    xn = _bf(_rms(x_ref[...], g_ref[...]))
    off = 0
    for o_ref in out_refs:
        width = o_ref.shape[-1]
        for c0 in range(0, width, 512):
            cw = min(512, width - c0)
            o_ref[:, c0:c0 + cw] = _dot(xn, w_ref[:, off + c0:off + c0 + cw])
        off += width


def _norm_proj(x2d, g, w_bf16, widths, name):
    n, d = x2d.shape
    tm = min(TOKEN_TILE, n)
    assert n % tm == 0 and sum(widths) == w_bf16.shape[1]
    return pl.pallas_call(
        _norm_proj_kernel,
        grid=(n // tm,),
        in_specs=[pl.BlockSpec((tm, d), lambda i: (i, 0)),
                  _resident((1, d)),
                  _resident(w_bf16.shape)],
        out_specs=[pl.BlockSpec((tm, w), lambda i: (i, 0)) for w in widths],
        out_shape=[jax.ShapeDtypeStruct((n, w), F32) for w in widths],
        compiler_params=_cparams("arbitrary"),
        name=name,
    )(x2d, g.reshape(1, d), w_bf16)


def _mem_kv_kernel(x_ref, g_ref, w_ref, k_ref, v_ref, k4_hbm, v4_hbm, sem):
    i = pl.program_id(0)
    tm = x_ref.shape[0]
    w = HEADS * HEAD_DIM
    xn = _bf(_rms(x_ref[...], g_ref[...]))
    k_ref[...] = _dot(xn, w_ref[:, 0:w])
    v_ref[...] = _dot(xn, w_ref[:, w:2 * w])
    copies = [pltpu.make_async_copy(src.at[:, _head(h)], dst.at[pl.ds(i * tm, tm), h, :], sem.at[j, h])
              for j, (src, dst) in enumerate(((k_ref, k4_hbm), (v_ref, v4_hbm))) for h in range(HEADS)]
    for cp in copies:
        cp.start()
    for cp in copies:
        cp.wait()


def _mem_kv(x2d, g, w_bf16):
    n, d = x2d.shape
    w = HEADS * HEAD_DIM
    tm = min(TOKEN_TILE, n)
    assert n % tm == 0 and w_bf16.shape == (d, 2 * w)
    tok = pl.BlockSpec((tm, w), lambda i: (i, 0))
    anyspace = pl.BlockSpec(memory_space=pl.ANY)
    return pl.pallas_call(
        _mem_kv_kernel,
        grid=(n // tm,),
        in_specs=[pl.BlockSpec((tm, d), lambda i: (i, 0)), _resident((1, d)), _resident(w_bf16.shape)],
        out_specs=[tok, tok, anyspace, anyspace],
        out_shape=[jax.ShapeDtypeStruct((n, w), F32), jax.ShapeDtypeStruct((n, w), F32),
                   jax.ShapeDtypeStruct((n, HEADS, HEAD_DIM), F32), jax.ShapeDtypeStruct((n, HEADS, HEAD_DIM), F32)],
        scratch_shapes=[pltpu.SemaphoreType.DMA((2, HEADS))],
        compiler_params=_cparams("arbitrary"),
        name="mem_kv",
    )(x2d, g.reshape(1, d), w_bf16)


def _forget_lower_bound(logits, layer):
    m = jnp.max(logits, axis=0, keepdims=True)
    e = jnp.exp(logits - m)
    return jnp.sum(e[:layer + 1], axis=0, keepdims=True) / jnp.sum(e, axis=0, keepdims=True)


def _col_to_matrix_mxu(row):
    b, _, n = row.shape
    tile = jnp.concatenate([row, jnp.zeros((b, SUBLANES - 1, n), row.dtype)], axis=1)
    hi = _bf(tile)
    mid = _bf(tile - hi.astype(F32))
    lo = _bf(tile - hi.astype(F32) - mid.astype(F32))
    ones = jnp.ones((b, SUBLANES, n), BF16)
    return _bmm_tn(hi, ones) + _bmm_tn(mid, ones) + _bmm_tn(lo, ones)


def _gdn_inputs(acc, ab, alog, dtb, t_valid):
    qkv = jax.nn.silu(acc)
    w = HEADS * HEAD_DIM
    qs, ks = [], []
    for h in range(HEADS):
        qh = qkv[..., h * HEAD_DIM:(h + 1) * HEAD_DIM]
        kh = qkv[..., w + h * HEAD_DIM:w + (h + 1) * HEAD_DIM]
        qs.append(qh * lax.rsqrt(jnp.sum(qh * qh, axis=-1, keepdims=True) + EPS) * (HEAD_DIM ** -0.5))
        ks.append(kh * lax.rsqrt(jnp.sum(kh * kh, axis=-1, keepdims=True) + EPS))
    v = qkv[..., 2 * w:3 * w]
    pre = ab + dtb
    softplus = jnp.maximum(pre, 0.0) + jnp.log(1.0 + jnp.exp(-jnp.abs(pre)))
    glog = -jnp.exp(alog) * softplus
    beta = jax.nn.sigmoid(ab)
    if t_valid is not None:
        valid = lax.broadcasted_iota(jnp.int32, ab.shape, ab.ndim - 2) < t_valid
        glog = jnp.where(valid, glog, 0.0)
        beta = jnp.where(valid, beta, 0.0)
    return qs, ks, v, glog, beta


def _hier_tables(c):
    t = np.arange(c)
    u = np.arange(c)
    tri = (u[None, :] <= t[:, None]).astype(np.float32)
    rows, masks = [], []
    b = c // 2
    while b >= 1:
        ref = (t // (2 * b)) * 2 * b + b - 1
        rows.append(tri - (u[None, :] <= ref[:, None]).astype(np.float32))
        right = (t % (2 * b)) >= b
        left = (t % (2 * b)) < b
        same = (t[:, None] // (2 * b)) == (t[None, :] // (2 * b))
        masks.append((right[:, None] & left[None, :] & same).astype(np.float32))
        b //= 2
    rows.append(tri)
    return np.concatenate(rows, 0), np.stack(masks, 0)


def _hgrn_chunk(q, k, v, logf_hi, logf_lo, s, wall_ref, masks, c):
    nlev = len(masks)
    d = _dot(wall_ref[...], jnp.concatenate([logf_hi, logf_lo], axis=0))
    g = d[nlev * c:(nlev + 1) * c]
    gc = g[c - 1:c, :]
    qg = _bf(q * jnp.exp(g))
    kg = _bf(k * jnp.exp(gc - g))
    egc = jnp.exp(gc)
    vb = _bf(v)
    npair = HEADS // 2
    a = [None] * npair
    for l in range(nlev):
        e = jnp.exp(-jnp.abs(d[l * c:(l + 1) * c]))
        ql, kl = _bf(q * e), _bf(k * e)
        for pr in range(npair):
            pair = slice(2 * pr * HEAD_DIM, (2 * pr + 2) * HEAD_DIM)
            p = _dot_nt(ql[:, pair], _block_diag(kl[:, pair], 2))
            a[pr] = jnp.where(masks[l], p, 0.0) if a[pr] is None else jnp.where(masks[l], p, a[pr])
    qk = q * k
    outs, s_new = [], []
    for h in range(HEADS):
        sl = _head(h)
        a_h = a[h // 2][:, (h % 2) * c:(h % 2 + 1) * c]
        o = _dot_nt(qg[:, sl], _bf(s[h])) + _dot(_bf(a_h), vb[:, sl])
        outs.append(o + jnp.sum(qk[:, sl], axis=-1, keepdims=True) * v[:, sl])
        s_new.append(egc[:, sl] * s[h] + _dot_tn(vb[:, sl], kg[:, sl]))
    return outs, s_new


def _hgrn_chunk_bounded(q, k, v, g, s, c):
    gc = g[c - 1:c, :]
    qg = _bf(q * jnp.exp(g))
    kn = _bf(k * jnp.exp(-g))
    kg = _bf(k * jnp.exp(gc - g))
    egc = jnp.exp(gc)
    vb = _bf(v)
    ti = lax.broadcasted_iota(jnp.int32, (c, HEADS * c), 0)
    si = lax.broadcasted_iota(jnp.int32, (c, HEADS * c), 1) & (c - 1)
    a = jnp.where(ti > si, _dot_nt(qg, _block_diag(kn, HEADS)), 0.0)
    v_rows = jnp.concatenate([vb[:, _head(h)] for h in range(HEADS)], axis=0)
    intra = _dot(_block_diag(_bf(a), HEADS), v_rows)
    qk = q * k
    outs, s_new = [], []
    for h in range(HEADS):
        sl = _head(h)
        outs.append(_dot_nt(qg[:, sl], _bf(s[h])) + intra[h * c:(h + 1) * c]
                    + jnp.sum(qk[:, sl], axis=-1, keepdims=True) * v[:, sl])
        s_new.append(egc[:, sl] * s[h] + _dot_tn(vb[:, sl], kg[:, sl]))
    return outs, s_new


def _hgrn_kernel(q_ref, f_ref, i_ref, gate_ref, lbl_ref, gout_ref, wall_ref, mask_ref, tri_ref, s0_ref,
                 o_ref, sn_ref, s_scr, *, c, nch, layer):
    n = pl.program_id(1)

    @pl.when(n == 0)
    def _():
        for h in range(HEADS):
            s_scr[h] = s0_ref[0, h].T

    lb = _forget_lower_bound(lbl_ref[...], layer)
    gout = gout_ref[...]
    f = lb + (1.0 - lb) * jax.nn.sigmoid(f_ref[0])
    hi, lo = _split2(jnp.log(f))
    k = 1.0 - f
    nlev = mask_ref.shape[0]

    def emit(r, outs):
        for h in range(HEADS):
            sl = _head(h)
            o_ref[0, r, sl] = _bf(_rms(outs[h], gout[:, sl]) * jax.nn.silu(gate_ref[0, r, sl]))

    def run(chunk_len, chunk_fn):
        s = [s_scr[h] for h in range(HEADS)]
        for ch in range(nch * c // chunk_len):
            r = slice(ch * chunk_len, (ch + 1) * chunk_len)
            outs, s = chunk_fn(ch, r, s)
            emit(r, outs)
        for h in range(HEADS):
            s_scr[h] = s[h]

    cb = tri_ref.shape[0]
    half = lambda ch: slice(ch * cb, (ch + 1) * cb)
    gs = [_dot(tri_ref[...], jnp.concatenate([hi[half(ch)], lo[half(ch)]], axis=0))
          for ch in range(nch * c // cb)]
    gmin = functools.reduce(jnp.minimum, [jnp.min(g) for g in gs])
    bounded = gmin >= -SAFE_LOG_RANGE

    @pl.when(bounded)
    def _():
        run(cb, lambda ch, r, s: _hgrn_chunk_bounded(q_ref[0, r, :], k[r], i_ref[0, r, :], gs[ch], s, cb))

    @pl.when(jnp.logical_not(bounded))
    def _():
        masks = [mask_ref[l] != 0.0 for l in range(nlev)]
        run(c, lambda ch, r, s: _hgrn_chunk(q_ref[0, r, :], k[r], i_ref[0, r, :], hi[r], lo[r], s,
                                            wall_ref, masks, c))

    @pl.when(n == pl.num_programs(1) - 1)
    def _():
        for h in range(HEADS):
            sn_ref[0, h] = s_scr[h].T


def _hgrn_long(hgp, lb_logits, g_out, s0, layer):
    bsz, t, _ = hgp.shape
    c = HG_CHUNK
    nch = HG_CHUNKS_PER_STEP
    tb = c * nch
    assert t % tb == 0 and c == HEAD_DIM and HEADS % 2 == 0
    w = HEADS * HEAD_DIM
    wall, masks = _hier_tables(c)
    wall = np.concatenate([wall, wall], axis=1)
    masks = np.concatenate([masks, masks], axis=2)
    cb = HG_BOUNDED_CHUNK
    tri = np.tril(np.ones((cb, cb), np.float32))
    tri = np.concatenate([tri, tri], axis=1)
    col = lambda j: pl.BlockSpec((1, tb, w), lambda b, n, j=j: (b, n, j))
    st = pl.BlockSpec((1, HEADS, HEAD_DIM, HEAD_DIM), lambda b, n: (b, 0, 0, 0))
    return pl.pallas_call(
        functools.partial(_hgrn_kernel, c=c, nch=nch, layer=layer),
        grid=(bsz, t // tb),
        in_specs=[col(0), col(1), col(2), col(3),
                  _resident(lb_logits.shape), _resident((1, w)),
                  _resident(wall.shape), _resident(masks.shape), _resident(tri.shape), st],
        out_specs=[pl.BlockSpec((1, tb, w), lambda b, n: (b, n, 0)), st],
        out_shape=[jax.ShapeDtypeStruct((bsz, t, w), BF16),
                   jax.ShapeDtypeStruct(s0.shape, F32)],
        scratch_shapes=[pltpu.VMEM((HEADS, HEAD_DIM, HEAD_DIM), F32)],
        compiler_params=_cparams("arbitrary", "arbitrary"),
        name="hgrn_long",
    )(hgp, hgp, hgp, hgp, lb_logits, g_out.reshape(1, w),
      jnp.asarray(wall, BF16), jnp.asarray(masks, F32), jnp.asarray(tri, BF16), s0)


def _gdn_prepare(qs, ks, v, glog, beta, tri_bf, c, nch):
    nb = nch * HEADS
    wc = HEADS * c
    rows = lambda x, ch: x[ch * c:(ch + 1) * c]
    per = lambda f: jnp.stack([f(ch, h) for ch in range(nch) for h in range(HEADS)], axis=0)
    q = per(lambda ch, h: rows(qs[h], ch))
    k = per(lambda ch, h: rows(ks[h], ch))
    vv = per(lambda ch, h: rows(v, ch)[:, _head(h)])
    beta_b = per(lambda ch, h: jnp.broadcast_to(rows(beta, ch)[:, HEADS + h:HEADS + h + 1], (c, LANES)))
    lane = lax.broadcasted_iota(jnp.int32, (c, LANES), 1)
    g_small = None
    for ch in range(nch):
        part = jnp.where(lane < HEADS, rows(glog, ch), 0.0)
        part = part if ch == 0 else pltpu.roll(part, ch * HEADS, axis=1)
        g_small = part if g_small is None else g_small + part
    gcum = _dot_exact_lhs(tri_bf, g_small)
    gcum_t = jnp.concatenate([gcum, jnp.zeros((LANES - c, LANES), F32)], axis=0).T
    g = jnp.stack([jnp.broadcast_to(gcum[:, n:n + 1], (c, LANES)) for n in range(nb)], axis=0)
    packed = lambda f: jnp.stack([jnp.concatenate([f(ch * HEADS + h) for h in range(HEADS)], axis=1)
                                  for ch in range(nch)], axis=0)
    g_col = packed(lambda n: jnp.broadcast_to(gcum[:, n:n + 1], (c, c)))
    g_row = packed(lambda n: jnp.broadcast_to(gcum_t[n:n + 1, :c], (c, c)))
    ti = lax.broadcasted_iota(jnp.int32, (nch, c, wc), 1)
    si = lax.broadcasted_iota(jnp.int32, (nch, c, wc), 2) & (c - 1)
    decay = jnp.exp(jnp.minimum(g_col - g_row, 0.0))
    kb = k * beta_b
    heads_on_lanes = lambda x: jnp.stack(
        [jnp.concatenate([x[ch * HEADS + h] for h in range(HEADS)], axis=1) for ch in range(nch)], axis=0)
    kbq = _bf(jnp.concatenate([heads_on_lanes(kb), heads_on_lanes(q)], axis=1))
    kq = _bmm_nt(kbq, _block_diag(_bf(heads_on_lanes(k)), HEADS))
    m = jnp.where(ti > si, kq[:, :c] * decay, 0.0)
    aqk = jnp.where(ti > si, kq[:, c:] * decay, 0.0)
    qk_diag = jnp.sum(q * k, axis=-1, keepdims=True)
    eg = jnp.exp(g)
    rhs = jnp.concatenate([vv * beta_b, kb * eg], axis=-1)
    npow = -m
    poff = npow
    npow_bd = _block_diag(_bf(npow), HEADS)
    for _ in range(int(math.log2(c)) - 1):
        npow = _bmm(_bf(npow), npow_bd)
        npow_bd = _block_diag(_bf(npow), HEADS)
        poff = poff + npow + _bmm(_bf(poff), npow_bd)
    xoff = _bmm(_block_diag(_bf(poff), HEADS), _bf(rhs.reshape(nch, wc, 2 * HEAD_DIM)))
    x = rhs + xoff.reshape(nb, c, 2 * HEAD_DIM)
    u, w = x[:, :, :HEAD_DIM], x[:, :, HEAD_DIM:]
    gc = g[:, c - 1:c, :]
    return (u, _bf(jnp.concatenate([w, q * eg], axis=1)), _block_diag(_bf(aqk), HEADS), qk_diag,
            _bf(k * jnp.exp(gc - g)), jnp.exp(gc))


def _gdn_kernel(qkv_ref, z_ref, ab_ref, buf_ref, wconv_ref, alog_ref, dtb_ref, gout_ref, tri_ref,
                s0_ref, o_ref, sn_ref, s_scr, ext_scr, *, c, nch):
    n = pl.program_id(1)
    tb = c * nch

    @pl.when(n == 0)
    def _():
        s_scr[...] = s0_ref[0]
        ext_scr[0:SUBLANES - 3, :] = jnp.zeros((SUBLANES - 3, ext_scr.shape[1]), F32)
        ext_scr[SUBLANES - 3:SUBLANES, :] = buf_ref[0]

    @pl.when(n > 0)
    def _():
        ext_scr[0:SUBLANES, :] = ext_scr[tb:tb + SUBLANES, :]

    ext_scr[SUBLANES:SUBLANES + tb, :] = qkv_ref[0]
    full = ext_scr[...]
    wconv = wconv_ref[...]
    acc = full * wconv[CONV_W - 1:CONV_W, :]
    for j in range(CONV_W - 1):
        acc = acc + pltpu.roll(full, CONV_W - 1 - j, axis=0) * wconv[j:j + 1, :]
    qs, ks, v, glog, beta = _gdn_inputs(acc[SUBLANES:SUBLANES + tb], ab_ref[0], alog_ref[...], dtb_ref[...], None)
    u, wq, aqk_bd, qk_diag, kg, egc = _gdn_prepare(qs, ks, v, glog, beta, tri_ref[...], c, nch)
    gout = gout_ref[...]
    s = s_scr[...]
    for ch in range(nch):
        sel = slice(ch * HEADS, (ch + 1) * HEADS)
        ws_qs = _bmm(wq[sel], _bf(s))
        v_new = u[sel] - ws_qs[:, :c]
        intra = _dot(aqk_bd[ch], _bf(v_new.reshape(HEADS * c, HEAD_DIM)))
        o = ws_qs[:, c:] + intra.reshape(HEADS, c, HEAD_DIM) + qk_diag[sel] * v_new
        s = egc[sel] * s + _bmm_tn(kg[sel], _bf(v_new))
        for h in range(HEADS):
            sl = _head(h)
            o_ref[0, ch * c:(ch + 1) * c, sl] = _bf(_rms(o[h], gout)
                                                    * jax.nn.silu(z_ref[0, ch * c:(ch + 1) * c, sl]))
    s_scr[...] = s

    @pl.when(n == pl.num_programs(1) - 1)
    def _():
        sn_ref[0] = s


def _pad_lanes(x):
    return jnp.zeros((1, LANES), F32).at[0, :x.shape[0]].set(x.astype(F32))


def _gdn_long(gqkv, gz, ab, conv_buf, w_conv, a_log, dt_bias, g_out, s0):
    bsz, t, wq = gqkv.shape
    c = GDN_CHUNK
    nch = GDN_CHUNKS_PER_STEP
    tb = c * nch
    assert t % tb == 0
    w = HEADS * HEAD_DIM
    tri = np.tril(np.ones((c, c), np.float32))
    st = pl.BlockSpec((1, HEADS, HEAD_DIM, HEAD_DIM), lambda b, n: (b, 0, 0, 0))
    return pl.pallas_call(
        functools.partial(_gdn_kernel, c=c, nch=nch),
        grid=(bsz, t // tb),
        in_specs=[pl.BlockSpec((1, tb, wq), lambda b, n: (b, n, 0)),
                  pl.BlockSpec((1, tb, w), lambda b, n: (b, n, 0)),
                  pl.BlockSpec((1, tb, LANES), lambda b, n: (b, n, 0)),
                  pl.BlockSpec((1, CONV_W - 1, wq), lambda b, n: (b, 0, 0)),
                  _resident((CONV_W, wq)), _resident((1, LANES)), _resident((1, LANES)),
                  _resident((1, HEAD_DIM)), _resident((c, c)), st],
        out_specs=[pl.BlockSpec((1, tb, w), lambda b, n: (b, n, 0)), st],
        out_shape=[jax.ShapeDtypeStruct((bsz, t, w), BF16),
                   jax.ShapeDtypeStruct((bsz, HEADS, HEAD_DIM, HEAD_DIM), F32)],
        scratch_shapes=[pltpu.VMEM((HEADS, HEAD_DIM, HEAD_DIM), F32),
                        pltpu.VMEM((tb + SUBLANES, wq), F32)],
        compiler_params=_cparams("arbitrary", "arbitrary"),
        name="gdn_long",
    )(gqkv, gz, ab, conv_buf, w_conv,
      _pad_lanes(a_log), _pad_lanes(dt_bias),
      g_out.reshape(1, HEAD_DIM), jnp.asarray(tri, BF16), s0)


def _shift_rows(x, d):
    return x if d == 0 else pltpu.roll(x, d, axis=1)


def _cumsum_rows8(x, rows):
    for d in (1, 2, 4):
        x = x + jnp.where(rows >= d, pltpu.roll(x, d, axis=1), 0.0)
    return x


def _hgrn_short(q, hf, v, lb, s, rows, t_valid):
    f = lb + (1.0 - lb) * jax.nn.sigmoid(hf)
    valid = rows < t_valid
    logf = jnp.where(valid, jnp.log(f), 0.0)
    k = jnp.where(valid, 1.0 - f, 0.0)
    g = _cumsum_rows8(logf, rows)
    gc = g[:, SUBLANES - 1:SUBLANES, :]
    o = _bmm(_bf(q * jnp.exp(g)), _bf(s))
    for d in range(t_valid):
        ok = rows >= d
        dec = jnp.exp(jnp.where(ok, g - _shift_rows(g, d), 0.0))
        a = jnp.sum(jnp.where(ok, q * _shift_rows(k, d) * dec, 0.0), axis=-1, keepdims=True)
        o = o + a * _shift_rows(v, d)
    s_new = _col_to_matrix_mxu(jnp.exp(gc)) * s + _bmm_tn(_bf(k * jnp.exp(gc - g)), _bf(v))
    return o, s_new


def _gdn_short(q, k, v, g_b, beta_b, s, rows, t_valid):
    g = _cumsum_rows8(g_b, rows)
    gc = g[:, SUBLANES - 1:SUBLANES, :]
    eg = jnp.exp(g)
    kb = k * beta_b
    dec = [None] + [jnp.exp(jnp.where(rows >= d, g - _shift_rows(g, d), 0.0)) for d in range(1, t_valid)]
    mcol = [None] + [jnp.where(rows >= d, jnp.sum(kb * _shift_rows(k, d), axis=-1, keepdims=True) * dec[d], 0.0)
                     for d in range(1, t_valid)]
    ru, rw = v * beta_b, kb * eg
    xu, xw = ru, rw
    for _ in range(t_valid - 1):
        nu, nw = ru, rw
        for d in range(1, t_valid):
            nu = nu - mcol[d] * _shift_rows(xu, d)
            nw = nw - mcol[d] * _shift_rows(xw, d)
        xu, xw = nu, nw
    ws_qs = _bmm(_bf(jnp.concatenate([xw, q * eg], axis=1)), _bf(s))
    v_new = xu - ws_qs[:, :SUBLANES]
    o = ws_qs[:, SUBLANES:] + jnp.sum(q * k, axis=-1, keepdims=True) * v_new
    for d in range(1, t_valid):
        a = jnp.where(rows >= d, jnp.sum(q * _shift_rows(k, d), axis=-1, keepdims=True) * dec[d], 0.0)
        o = o + a * _shift_rows(v_new, d)
    s_new = jnp.exp(gc) * s + _bmm_tn(_bf(k * jnp.exp(gc - g)), _bf(v_new))
    return o, s_new


def _short_kernel(hq_ref, hf_ref, hi_ref, hgate_ref, qkv_ref, z_ref, ab_ref, buf_ref,
                  lbl_ref, hgout_ref, wconv_ref, alog_ref, dtb_ref, ggout_ref, shg0_ref, sgd0_ref,
                  ohg_ref, ogd_ref, shg_ref, sgd_ref, ext_scr, *, t_valid, layer):
    bb = shg0_ref.shape[0]
    rows = lax.broadcasted_iota(jnp.int32, (bb, SUBLANES, LANES), 1)
    lb = _forget_lower_bound(lbl_ref[...], layer)
    hgout = hgout_ref[...]
    ggout = ggout_ref[...]
    wconv = wconv_ref[...]

    def tiles(x):
        x = x.reshape(bb, t_valid, x.shape[-1])
        if t_valid == SUBLANES:
            return x
        return jnp.concatenate([x, jnp.zeros((bb, SUBLANES - t_valid, x.shape[2]), x.dtype)], axis=1)

    def tokens(x):
        return x[:, :t_valid].reshape(bb * t_valid, x.shape[-1])

    ext_scr[:, SUBLANES - 3:SUBLANES, :] = buf_ref[...]
    ext_scr[:, SUBLANES:2 * SUBLANES, :] = tiles(qkv_ref[...])
    acc = ext_scr[:, pl.ds(SUBLANES - 3, SUBLANES), :] * wconv[0:1, :]
    for j in range(1, CONV_W):
        acc = acc + ext_scr[:, pl.ds(SUBLANES - 3 + j, SUBLANES), :] * wconv[j:j + 1, :]
    qs, ks, v, glog, beta = _gdn_inputs(acc, tiles(ab_ref[...]), alog_ref[...], dtb_ref[...], t_valid)
    for h in range(HEADS):
        sl = _head(h)
        o, s_new = _hgrn_short(tiles(hq_ref[:, sl]), tiles(hf_ref[:, sl]), tiles(hi_ref[:, sl]),
                               lb[:, sl], shg0_ref[:, h], rows, t_valid)
        shg_ref[:, h] = s_new
        ohg_ref[:, sl] = tokens(_rms(o, hgout[:, sl])) * jax.nn.silu(hgate_ref[:, sl])
        g_b = jnp.broadcast_to(glog[:, :, h:h + 1], rows.shape)
        beta_b = jnp.broadcast_to(beta[:, :, HEADS + h:HEADS + h + 1], rows.shape)
        o, s_new = _gdn_short(qs[h], ks[h], v[:, :, sl], g_b, beta_b, sgd0_ref[:, h], rows, t_valid)
        sgd_ref[:, h] = s_new
        ogd_ref[:, sl] = tokens(_rms(o, ggout)) * jax.nn.silu(z_ref[:, sl])


def _mixers_short(hgp, gqkv, gz, ab, conv_buf, lb_logits, hg_out, w_conv, a_log, dt_bias, gdn_out,
                  s_hg0, s_gdn0, t_valid, layer):
    n, wq = gqkv.shape
    t = t_valid
    bsz = n // t
    assert n == bsz * t and CONV_W - 1 <= t_valid <= SUBLANES
    bb = DECODE_BLOCK
    assert bsz % bb == 0 and (bb * t) % SUBLANES == 0
    w = HEADS * HEAD_DIM
    col = lambda j: pl.BlockSpec((bb * t, w), lambda b, j=j: (b, j))
    tok = lambda width: pl.BlockSpec((bb * t, width), lambda b: (b, 0))
    st = pl.BlockSpec((bb, HEADS, HEAD_DIM, HEAD_DIM), lambda b: (b, 0, 0, 0))
    return pl.pallas_call(
        functools.partial(_short_kernel, t_valid=t_valid, layer=layer),
        grid=(bsz // bb,),
        in_specs=[col(0), col(1), col(2), col(3), tok(wq), tok(w), tok(LANES),
                  pl.BlockSpec((bb, CONV_W - 1, wq), lambda b: (b, 0, 0)),
                  _resident(lb_logits.shape), _resident((1, w)), _resident((CONV_W, wq)),
                  _resident((1, LANES)), _resident((1, LANES)), _resident((1, HEAD_DIM)), st, st],
        out_specs=[tok(w), tok(w), st, st],
        out_shape=[jax.ShapeDtypeStruct((n, w), F32), jax.ShapeDtypeStruct((n, w), F32),
                   jax.ShapeDtypeStruct(s_hg0.shape, F32), jax.ShapeDtypeStruct(s_gdn0.shape, F32)],
        scratch_shapes=[pltpu.VMEM((bb, 2 * SUBLANES, wq), F32)],
        compiler_params=_cparams("arbitrary"),
        name="mixers_short",
    )(hgp, hgp, hgp, hgp, gqkv, gz, ab, conv_buf, lb_logits, hg_out.reshape(1, w), w_conv,
      _pad_lanes(a_log), _pad_lanes(dt_bias), gdn_out.reshape(1, HEAD_DIM), s_hg0, s_gdn0)


def _mem_attn_kernel(q_ref, k_ref, v_ref, o_ref):
    for h in range(HEADS):
        sl = _head(h)
        s = _bmm_nt(_bf(q_ref[:, :, sl]), _bf(k_ref[:, :, sl])) * (HEAD_DIM ** -0.5)
        e = jnp.exp(s - jnp.max(s, axis=-1, keepdims=True))
        o_ref[:, :, sl] = _bf(_bmm(_bf(e), _bf(v_ref[:, :, sl])) / jnp.sum(e, axis=-1, keepdims=True))


def _mem_attn(mq, mem_k, mem_v, bb, tq):
    bsz, t, w = mq.shape
    m = mem_k.shape[1]
    assert bsz % bb == 0 and t % tq == 0
    kv = pl.BlockSpec((bb, m, w), lambda b, n: (b, 0, 0))
    return pl.pallas_call(
        _mem_attn_kernel,
        grid=(bsz // bb, t // tq),
        in_specs=[pl.BlockSpec((bb, tq, w), lambda b, n: (b, n, 0)), kv, kv],
        out_specs=pl.BlockSpec((bb, tq, w), lambda b, n: (b, n, 0)),
        out_shape=jax.ShapeDtypeStruct((bsz, t, w), BF16),
        compiler_params=_cparams("arbitrary", "arbitrary"),
        name="mem_attn",
    )(mq, mem_k, mem_v)


def _mem_attn_cache_kernel(q_ref, k_hbm, v_hbm, o_ref, kbuf, vbuf, sem):
    i = pl.program_id(0)
    bb = kbuf.shape[2]
    t = q_ref.shape[0] // bb

    def copies(step, slot):
        out = []
        for h in range(HEADS):
            for j, (src, dst) in enumerate(((k_hbm, kbuf), (v_hbm, vbuf))):
                out.append(pltpu.make_async_copy(src.at[pl.ds(step * bb, bb), :, h, :],
                                                 dst.at[slot, h], sem.at[j, slot, h]))
        return out

    slot = i % 2

    @pl.when(i == 0)
    def _():
        for cp in copies(0, 0):
            cp.start()

    @pl.when(i + 1 < pl.num_programs(0))
    def _():
        for cp in copies(i + 1, 1 - slot):
            cp.start()

    for cp in copies(i, slot):
        cp.wait()
    for h in range(HEADS):
        sl = _head(h)
        q = q_ref[:, sl].reshape(bb, t, HEAD_DIM)
        if t < SUBLANES:
            q = jnp.concatenate([q, jnp.zeros((bb, SUBLANES - t, HEAD_DIM), q.dtype)], axis=1)
        s = _bmm_nt(_bf(q), _bf(kbuf[slot, h])) * (HEAD_DIM ** -0.5)
        e = jnp.exp(s - jnp.max(s, axis=-1, keepdims=True))
        o = _bmm(_bf(e), _bf(vbuf[slot, h])) / jnp.sum(e, axis=-1, keepdims=True)
        o_ref[:, sl] = o[:, :t].reshape(bb * t, HEAD_DIM)


def _mem_attn_cache(mq, cache_k, cache_v, bb):
    n, w = mq.shape
    bsz, m = cache_k.shape[:2]
    t = n // bsz
    assert n == bsz * t and bsz % bb == 0 and cache_k.shape == (bsz, m, HEADS, HEAD_DIM)
    assert (bb * t) % SUBLANES == 0
    return pl.pallas_call(
        _mem_attn_cache_kernel,
        grid=(bsz // bb,),
        in_specs=[pl.BlockSpec((bb * t, w), lambda b: (b, 0)),
                  pl.BlockSpec(memory_space=pl.ANY), pl.BlockSpec(memory_space=pl.ANY)],
        out_specs=pl.BlockSpec((bb * t, w), lambda b: (b, 0)),
        out_shape=jax.ShapeDtypeStruct((n, w), F32),
        scratch_shapes=[pltpu.VMEM((2, HEADS, bb, m, HEAD_DIM), F32),
                        pltpu.VMEM((2, HEADS, bb, m, HEAD_DIM), F32),
                        pltpu.SemaphoreType.DMA((2, 2, HEADS))],
        compiler_params=_cparams("arbitrary"),
        name="mem_attn_cache",
    )(mq, cache_k, cache_v)


def _merge_ffn_kernel(x_ref, ohg_ref, ogd_ref, omem_ref, gpre_ref, wgates_ref,
                      wbh_ref, wbg_ref, wbm_ref, wout_ref, gpm_ref, gpf_ref,
                      wffn_ref, wdown_ref, gpo_ref, y_ref):
    d = x_ref.shape[-1]
    x = x_ref[...]
    xn = _bf(_rms(x, gpre_ref[...]))
    merged = None
    for j, (o_ref, wb_ref) in enumerate(((ohg_ref, wbh_ref), (ogd_ref, wbg_ref), (omem_ref, wbm_ref))):
        gate = jax.nn.sigmoid(_dot(xn, wgates_ref[:, j * d:(j + 1) * d]))
        term = gate * _dot(_bf(o_ref[...]), wb_ref[...])
        merged = term if merged is None else merged + term
    h = x + _rms(_dot(_bf(merged), wout_ref[...]), gpm_ref[...])
    hn = _bf(_rms(h, gpf_ref[...]))
    hidden = wdown_ref.shape[0]
    ff = None
    for c0 in range(0, hidden, FFN_CHUNK):
        a = (jax.nn.silu(_dot(hn, wffn_ref[:, c0:c0 + FFN_CHUNK]))
             * _dot(hn, wffn_ref[:, hidden + c0:hidden + c0 + FFN_CHUNK]))
        p = _dot(_bf(a), wdown_ref[c0:c0 + FFN_CHUNK, :])
        ff = p if ff is None else ff + p
    y_ref[...] = h + _rms(ff, gpo_ref[...])


def _merge_ffn(x2d, ohg, ogd, omem, gpre, wgates, wbh, wbg, wbm, wout, gpm, gpf, wffn, wdown, gpo):
    n, d = x2d.shape
    tm = min(TOKEN_TILE, n)
    hidden = wdown.shape[0]
    assert n % tm == 0 and hidden % FFN_CHUNK == 0 and wffn.shape[1] == 2 * hidden
    assert wgates.shape == (d, N_BRANCH * d)
    tok = lambda width: pl.BlockSpec((tm, width), lambda i: (i, 0))
    row = lambda a: a.reshape(1, d)
    return pl.pallas_call(
        _merge_ffn_kernel,
        grid=(n // tm,),
        in_specs=[tok(d), tok(ohg.shape[1]), tok(ogd.shape[1]), tok(omem.shape[1]),
                  _resident((1, d)), _resident(wgates.shape),
                  _resident(wbh.shape), _resident(wbg.shape), _resident(wbm.shape), _resident(wout.shape),
                  _resident((1, d)), _resident((1, d)),
                  _resident(wffn.shape), _resident(wdown.shape), _resident((1, d))],
        out_specs=tok(d),
        out_shape=jax.ShapeDtypeStruct((n, d), F32),
        compiler_params=_cparams("arbitrary"),
        name="merge_ffn",
    )(x2d, ohg, ogd, omem, row(gpre), wgates, wbh, wbg, wbm, wout, row(gpm), row(gpf), wffn, wdown, row(gpo))


def _pack_w_in_kernel(wt_ref, mix_ref, gates_ref):
    w = HEADS * HEAD_DIM
    c_ab = 8 * w
    c_mq = c_ab + 2 * HEADS
    c_gt = c_mq + w

    def put(dst_ref, dst0, src0, n):
        for j in range(0, n, w):
            nb = min(w, n - j)
            dst_ref[:, dst0 + j:dst0 + j + nb] = _bf(wt_ref[pl.ds(src0 + j, nb), :].T)

    put(mix_ref, 0, 0, c_ab)
    put(mix_ref, c_ab, c_mq, w)
    ab = jnp.concatenate([wt_ref[pl.ds(c_ab, 2 * HEADS), :],
                          jnp.zeros((LANES - 2 * HEADS, wt_ref.shape[1]), F32)], axis=0)
    mix_ref[:, c_ab + w:c_ab + w + LANES] = _bf(ab.T)
    put(gates_ref, 0, c_gt, gates_ref.shape[1])


def _pack_w_in(w_in_t):
    total, d = w_in_t.shape
    w = HEADS * HEAD_DIM
    widths = (4 * w, 3 * w, w, w, LANES)
    n_gates = total - (9 * w + 2 * HEADS)
    assert n_gates % LANES == 0 and d % LANES == 0
    mixers, gates = pl.pallas_call(
        _pack_w_in_kernel,
        out_shape=[jax.ShapeDtypeStruct((d, sum(widths)), BF16), jax.ShapeDtypeStruct((d, n_gates), BF16)],
        compiler_params=pltpu.CompilerParams(vmem_limit_bytes=VMEM_LIMIT),
        name="pack_w_in",
    )(w_in_t)
    return mixers, widths, gates


def _of_layer(a, l):
    return a.reshape(a.shape[1:]) if a.shape[0] == 1 else a[l]


def _layer(x, mem_k, mem_v, conv_buf, s_hg, s_gdn, layer, lb_logits, p):
    bsz, t, d = x.shape
    t_valid = t
    short = t <= SUBLANES
    x2d = x.reshape(bsz * t, d)
    w = HEADS * HEAD_DIM

    hgp, gqkv, gz, mq, ab = _norm_proj(x2d, p["g_pre_mix"], p["w_in"], p["w_in_widths"], "in_proj")
    as3d = lambda a: a.reshape(bsz, t, a.shape[-1])
    if short:
        o_hg, o_gdn, s_hg_new, s_gdn_new = _mixers_short(
            hgp, gqkv, gz, ab, conv_buf, lb_logits, p["g_hg_out"], p["w_conv"],
            p["a_log"], p["dt_bias"], p["g_gdn_out"], s_hg, s_gdn, t_valid, layer)
        o_mem = _mem_attn_cache(mq, mem_k, mem_v, ATTN_DECODE_BLOCK)
    else:
        o_hg, s_hg_new = _hgrn_long(as3d(hgp), lb_logits, p["g_hg_out"], s_hg, layer)
        o_gdn, s_gdn_new = _gdn_long(as3d(gqkv), as3d(gz), as3d(ab), conv_buf, p["w_conv"], p["a_log"],
                                     p["dt_bias"], p["g_gdn_out"], s_gdn)
        o_mem = _mem_attn(as3d(mq), mem_k, mem_v, 1, min(ATTN_TILE, t))
    assert t_valid >= CONV_W - 1
    conv_new = as3d(gqkv)[:, t_valid - (CONV_W - 1):t_valid, :]

    flat = lambda a: a.reshape(bsz * t, w)
    y = _merge_ffn(x2d, flat(o_hg), flat(o_gdn), flat(o_mem), p["g_pre_mix"], p["w_gates"],
                   p["w_br_hg"], p["w_br_gdn"], p["w_br_mem"], p["w_out"], p["g_post_mix"],
                   p["g_pre_ffn"], p["w_ffn_in"], p["w_ffn_out"], p["g_post_ffn"])
    return y.reshape(bsz, t, d), conv_new, s_hg_new, s_gdn_new


def kernel(x_prompt, x_sample, mem_prompt, cache_mem_k, cache_mem_v, state_hgrn, state_gdn, state_gdn_conv, hg_lb_logits, g_pre_mix, w_in, w_conv, a_log, dt_bias, g_hg_out, g_gdn_out, g_mem, w_mem_kv, w_br_hg, w_br_gdn, w_br_mem, w_out, g_post_mix, g_pre_ffn, w_ffn_in, w_ffn_out, g_post_ffn):
    depth = w_in.shape[0]
    bp, _, d = x_prompt.shape
    m = mem_prompt.shape[1]
    w = HEADS * HEAD_DIM
    yp, ys = x_prompt, x_sample
    outs = [[] for _ in range(8)]
    for l in range(depth):
        w_in_packed, widths, w_gates = _pack_w_in(_of_layer(w_in, l).T)
        p = dict(g_pre_mix=g_pre_mix[l], w_in=w_in_packed, w_in_widths=widths, w_gates=w_gates,
                 w_conv=w_conv[l],
                 a_log=a_log[l], dt_bias=dt_bias[l], g_hg_out=g_hg_out[l], g_gdn_out=g_gdn_out[l],
                 w_br_hg=_bf(w_br_hg[l]), w_br_gdn=_bf(w_br_gdn[l]), w_br_mem=_bf(w_br_mem[l]),
                 w_out=_bf(w_out[l]), g_post_mix=g_post_mix[l], g_pre_ffn=g_pre_ffn[l],
                 w_ffn_in=_bf(w_ffn_in[l]), w_ffn_out=_bf(w_ffn_out[l]), g_post_ffn=g_post_ffn[l])
        mk, mv, mk4, mv4 = _mem_kv(mem_prompt.reshape(bp * m, d), g_mem[l], _bf(w_mem_kv[l]))
        mk, mv = mk.reshape(bp, m, w), mv.reshape(bp, m, w)
        zeros_state = jnp.zeros((bp, HEADS, HEAD_DIM, HEAD_DIM), F32)
        yp, cb, sh, sg = _layer(yp, mk, mv, jnp.zeros((bp, CONV_W - 1, 3 * w), F32),
                                zeros_state, zeros_state, l, hg_lb_logits, p)
        ys, cb2, sh2, sg2 = _layer(ys, _of_layer(cache_mem_k, l), _of_layer(cache_mem_v, l),
                                   _of_layer(state_gdn_conv, l), _of_layer(state_hgrn, l),
                                   _of_layer(state_gdn, l), l, hg_lb_logits, p)
        for lst, val in zip(outs, (mk4.reshape(bp, m, HEADS, HEAD_DIM), mv4.reshape(bp, m, HEADS, HEAD_DIM),
                                   sh, sg, cb, sh2, sg2, cb2)):
            lst.append(val)
    return (yp, ys) + tuple(o[0].reshape((1,) + o[0].shape) if depth == 1 else jnp.stack(o) for o in outs)
```
